```python
import math
import jax
import jax.numpy as jnp
from jax import lax
import numpy as np

D_MODEL = 1024
BATCH = 4
SEQ = 8192
DEPTH = 2

GRID_W = 64
CTX_LEN = 256
NORM_EPS = 1e-6

SSM_HEADS = 16
SSM_HEAD_DIM = 64
SSM_INNER = SSM_HEADS * SSM_HEAD_DIM
SSM_STATE = 128
SSM_GROUPS = 2
SSM_GN = SSM_GROUPS * SSM_STATE
SSM_XBC = SSM_INNER + 2 * SSM_GN
SSM_PROJ = SSM_INNER + SSM_XBC + 2 * SSM_HEADS
SSM_CONV_W = 5
SSD_CHUNK = 128
DT_MIN = 0.001
DT_MAX = 0.1

NA_HEADS = 16
NA_HEAD_DIM = 64
NA_INNER = NA_HEADS * NA_HEAD_DIM
NA_ROWS = 8
NA_COLS = 16

AB_PROJ = SSM_PROJ + 3 * NA_INNER
AB_OUT = SSM_INNER + NA_INNER

GQA_HEADS = 16
GQA_KV_HEADS = 4
GQA_HEAD_DIM = 64
GQA_Q = GQA_HEADS * GQA_HEAD_DIM
GQA_KV = GQA_KV_HEADS * GQA_HEAD_DIM
GQA_PROJ = GQA_Q + 2 * GQA_KV
Q_BLOCK = 128
ROPE_THETA = 10000.0

N_EXPERTS = 32
N_GROUPS = 8
EXPERTS_PER_GROUP = N_EXPERTS // N_GROUPS
TOP_K = 2
D_EXPERT = 512
MOE_BLOCK = 128

kernel_name = "hybrid_ssd_natten_gqa_moe_prefix_dit"


def _rmsnorm(x, g):
    xf = x.astype(jnp.float32)
    y = xf * lax.rsqrt(jnp.mean(xf * xf, axis=-1, keepdims=True) + NORM_EPS)
    return (y * g.astype(jnp.float32)).astype(x.dtype)


def _attend(q, k, v):
    s = jnp.einsum('bqkgd,bskd->bkgqs', q, k).astype(jnp.float32) * (q.shape[-1] ** -0.5)
    p = jax.nn.softmax(s, axis=-1).astype(v.dtype)
    return jnp.einsum('bkgqs,bskd->bqkgd', p, v)


def _axial_rope_tables(n_tokens):
    t = jnp.arange(n_tokens)
    row = (t // GRID_W).astype(jnp.float32)
    col = (t % GRID_W).astype(jnp.float32)
    n_freq = GQA_HEAD_DIM // 4
    inv = ROPE_THETA ** (-jnp.arange(n_freq, dtype=jnp.float32) / n_freq)
    ang = jnp.concatenate([row[:, None] * inv, col[:, None] * inv], axis=-1)
    return jnp.cos(ang), jnp.sin(ang)


def _apply_rope(x, cos, sin):
    half = x.shape[-1] // 2
    shape = (1, x.shape[1]) + (1,) * (x.ndim - 3) + (half,)
    cs = cos.reshape(shape).astype(x.dtype)
    sn = sin.reshape(shape).astype(x.dtype)
    x1, x2 = x[..., :half], x[..., half:]
    return jnp.concatenate([x1 * cs - x2 * sn, x1 * sn + x2 * cs], axis=-1)


def _dwconv_centred(u, w, bias):
    ch = u.shape[-1]
    y = lax.conv_general_dilated(u, w[:, None, :].astype(u.dtype), window_strides=(1,),
                                 padding=[(SSM_CONV_W // 2, SSM_CONV_W // 2)],
                                 dimension_numbers=('NWC', 'WIO', 'NWC'), feature_group_count=ch)
    return y + bias.astype(u.dtype)


def _ssd_scan(x, dt, a, bm, cm, init_state):
    f32 = jnp.float32
    b, l, h, p = x.shape
    g, n = bm.shape[-2], bm.shape[-1]
    j = h // g
    L = SSD_CHUNK
    nc = l // L
    xd = (x.astype(f32) * dt[..., None]).reshape(b, nc, L, g, j, p)
    a_dt = jnp.transpose((dt * a).reshape(b, nc, L, g, j), (0, 3, 4, 1, 2))
    bc = bm.astype(f32).reshape(b, nc, L, g, n)
    cc = cm.astype(f32).reshape(b, nc, L, g, n)
    a_cum = jnp.cumsum(a_dt, axis=-1)
    tri = jnp.tril(jnp.ones((L, L), bool))
    decay = jnp.exp(jnp.where(tri, a_cum[..., :, None] - a_cum[..., None, :], -jnp.inf))
    cb = jnp.einsum('bclgn,bcsgn->bgcls', cc, bc)
    y_diag = jnp.einsum('bgjcls,bcsgjp->bclgjp', cb[:, :, None] * decay, xd)
    decay_states = jnp.exp(a_cum[..., -1:] - a_cum)
    states = jnp.einsum('bclgn,bgjcl,bclgjp->bcgjpn', bc, decay_states, xd)
    states = jnp.concatenate([init_state.astype(f32).reshape(b, 1, g, j, p, n), states], axis=1)
    chunk_cum = jnp.cumsum(jnp.pad(a_cum[..., -1], ((0, 0), (0, 0), (0, 0), (1, 0))), axis=-1)
    tri_c = jnp.tril(jnp.ones((nc + 1, nc + 1), bool))
    chunk_decay = jnp.exp(jnp.where(tri_c, chunk_cum[..., :, None] - chunk_cum[..., None, :], -jnp.inf))
    states = jnp.einsum('bgjzc,bcgjpn->bzgjpn', chunk_decay, states)
    y_off = jnp.einsum('bclgn,bcgjpn,bgjcl->bclgjp', cc, states[:, :-1], jnp.exp(a_cum))
    y = (y_diag + y_off).reshape(b, l, h, p)
    return y, states[:, -1].reshape(b, h, p, n)


def _ssd_branch(u, conv_w, conv_b, a_log, dt_bias, d_skip, norm_g, init_f, init_b):
    f32 = jnp.float32
    b, l, _ = u.shape
    z = u[..., :SSM_INNER]
    xbc = jax.nn.silu(_dwconv_centred(u[..., SSM_INNER:SSM_INNER + SSM_XBC], conv_w, conv_b))
    dt_raw = u[..., SSM_INNER + SSM_XBC:].astype(f32)
    xs = xbc[..., :SSM_INNER].reshape(b, l, SSM_HEADS, SSM_HEAD_DIM)
    bm = xbc[..., SSM_INNER:SSM_INNER + SSM_GN].reshape(b, l, SSM_GROUPS, SSM_STATE)
    cm = xbc[..., SSM_INNER + SSM_GN:].reshape(b, l, SSM_GROUPS, SSM_STATE)
    a = -jnp.exp(a_log.astype(f32))
    dtb = dt_bias.astype(f32)
    dt_f = jax.nn.softplus(dt_raw[..., :SSM_HEADS] + dtb[0])
    dt_b = jax.nn.softplus(dt_raw[..., SSM_HEADS:] + dtb[1])
    y_f, s_f = _ssd_scan(xs, dt_f, a[0], bm, cm, init_f)
    flip = lambda t: jnp.flip(t, axis=1)
    y_b, s_b = _ssd_scan(flip(xs), flip(dt_b), a[1], flip(bm), flip(cm), init_b)
    y = y_f + flip(y_b) + xs.astype(f32) * d_skip.astype(f32)[:, None]
    y = y.reshape(b, l, SSM_INNER) * jax.nn.silu(z.astype(f32))
    return _rmsnorm(y, norm_g).astype(u.dtype), s_f, s_b


def _neighbourhood_attention(q, k, v, k_ctx, v_ctx, rpb):
    b, n, h, d = q.shape
    rows = n // GRID_W
    wr = min(NA_ROWS, rows)
    nk = wr * GRID_W
    scale = d ** -0.5
    qg = jnp.moveaxis(q.reshape(b, rows, GRID_W, h, d), 1, 0)
    kg = k.reshape(b, rows, GRID_W, h, d)
    vg = v.reshape(b, rows, GRID_W, h, d)
    qcol = jnp.arange(GRID_W)
    cstart = jnp.clip(qcol - NA_COLS // 2, 0, GRID_W - NA_COLS)
    kcol = jnp.arange(GRID_W)
    col_ok = (kcol[None, :] >= cstart[:, None]) & (kcol[None, :] < cstart[:, None] + NA_COLS)
    col_ok = jnp.broadcast_to(col_ok[:, None, :], (GRID_W, wr, GRID_W)).reshape(GRID_W, nk)
    col_idx = jnp.clip(kcol[None, :] - qcol[:, None] + NA_COLS - 1, 0, 2 * NA_COLS - 2)

    def row_block(args):
        r, q_row = args
        rs = jnp.clip(r - wr // 2, 0, rows - wr)
        k_blk = lax.dynamic_slice_in_dim(kg, rs, wr, axis=1).reshape(b, nk, h, d)
        v_blk = lax.dynamic_slice_in_dim(vg, rs, wr, axis=1).reshape(b, nk, h, d)
        row_idx = rs + jnp.arange(wr) - r + NA_ROWS - 1
        bias = rpb[:, row_idx[:, None, None], col_idx[None, :, :]]
        bias = jnp.transpose(bias, (0, 2, 1, 3)).reshape(h, GRID_W, nk).astype(jnp.float32)
        s_lat = jnp.einsum('bqhd,bkhd->bhqk', q_row, k_blk).astype(jnp.float32) * scale + bias
        s_lat = jnp.where(col_ok, s_lat, -jnp.inf)
        s_ctx = jnp.einsum('bqhd,bchd->bhqc', q_row, k_ctx).astype(jnp.float32) * scale
        p = jax.nn.softmax(jnp.concatenate([s_lat, s_ctx], axis=-1), axis=-1).astype(v.dtype)
        return (jnp.einsum('bhqk,bkhd->bqhd', p[..., :nk], v_blk)
                + jnp.einsum('bhqc,bchd->bqhd', p[..., nk:], v_ctx))

    out = lax.map(row_block, (jnp.arange(rows), qg))
    return jnp.moveaxis(out, 0, 1).reshape(b, n, h, d)


def _mixer_ssd_na(xn, cn, w_in, w_out, conv_w, conv_b, a_log, dt_bias, d_skip, ssm_norm_g, rpb, need_ctx):
    b, n, _ = xn.shape
    lc = cn.shape[1]
    pl = xn @ w_in
    pc = cn @ w_in
    o1, o2 = SSM_PROJ + NA_INNER, SSM_PROJ + 2 * NA_INNER
    zeros = jnp.zeros((b, SSM_HEADS, SSM_HEAD_DIM, SSM_STATE), jnp.float32)
    y_c_ssd, s_f, s_b = _ssd_branch(pc[..., :SSM_PROJ], conv_w, conv_b, a_log, dt_bias, d_skip, ssm_norm_g, zeros, zeros)
    kc = pc[..., SSM_PROJ:o1].reshape(b, lc, NA_HEADS, NA_HEAD_DIM) if False else pc[..., o1:o2].reshape(b, lc, NA_HEADS, NA_HEAD_DIM)
    vc = pc[..., o2:].reshape(b, lc, NA_HEADS, NA_HEAD_DIM)
    y_ssd, _, _ = _ssd_branch(pl[..., :SSM_PROJ], conv_w, conv_b, a_log, dt_bias, d_skip, ssm_norm_g, s_f, s_b)
    q = pl[..., SSM_PROJ:o1].reshape(b, n, NA_HEADS, NA_HEAD_DIM)
    k = pl[..., o1:o2].reshape(b, n, NA_HEADS, NA_HEAD_DIM)
    v = pl[..., o2:].reshape(b, n, NA_HEADS, NA_HEAD_DIM)
    o_na = _neighbourhood_attention(q, k, v, kc, vc, rpb).reshape(b, n, NA_INNER)
    y = jnp.concatenate([y_ssd, o_na], axis=-1) @ w_out
    yc = None
    if need_ctx:
        qc = pc[..., SSM_PROJ:o1].reshape(b, lc, NA_HEADS, 1, NA_HEAD_DIM)
        o_c = _attend(qc, kc, vc).reshape(b, lc, NA_INNER)
        yc = jnp.concatenate([y_c_ssd, o_c], axis=-1) @ w_out
    return y, yc


def _mixer_gqa(xn, cn, w_in, w_out, q_g, k_g, cos, sin, need_ctx):
    b, n, _ = xn.shape
    lc = cn.shape[1]
    g = GQA_HEADS // GQA_KV_HEADS
    p = xn @ w_in
    q = _rmsnorm(p[..., :GQA_Q].reshape(b, n, GQA_KV_HEADS, g, GQA_HEAD_DIM), q_g)
    k = _rmsnorm(p[..., GQA_Q:GQA_Q + GQA_KV].reshape(b, n, GQA_KV_HEADS, GQA_HEAD_DIM), k_g)
    v = p[..., GQA_Q + GQA_KV:].reshape(b, n, GQA_KV_HEADS, GQA_HEAD_DIM)
    q = _apply_rope(q, cos, sin)
    k = _apply_rope(k, cos, sin)
    pkv = cn @ w_in[:, GQA_Q:]
    kc = _rmsnorm(pkv[..., :GQA_KV].reshape(b, lc, GQA_KV_HEADS, GQA_HEAD_DIM), k_g)
    vc = pkv[..., GQA_KV:].reshape(b, lc, GQA_KV_HEADS, GQA_HEAD_DIM)
    k_all = jnp.concatenate([k, kc], axis=1)
    v_all = jnp.concatenate([v, vc], axis=1)
    nb = n // Q_BLOCK
    qb = jnp.moveaxis(q.reshape(b, nb, Q_BLOCK, GQA_KV_HEADS, g, GQA_HEAD_DIM), 1, 0)
    o = lax.map(lambda qq: _attend(qq, k_all, v_all), qb)
    y = jnp.moveaxis(o, 0, 1).reshape(b, n, GQA_Q) @ w_out
    yc = None
    if need_ctx:
        qc = _rmsnorm((cn @ w_in[:, :GQA_Q]).reshape(b, lc, GQA_KV_HEADS, g, GQA_HEAD_DIM), q_g)
        yc = _attend(qc, kc, vc).reshape(b, lc, GQA_Q) @ w_out
    return y, yc


def _moe(xf, router_w, router_bias, w_gate, w_up, w_down):
    t, d = xf.shape
    s = jax.nn.sigmoid(jnp.dot(xf.astype(jnp.float32), router_w.astype(jnp.float32)))
    sb = (s + router_bias.astype(jnp.float32)).reshape(t, N_GROUPS, EXPERTS_PER_GROUP)
    grp = jnp.argmax(lax.top_k(sb, 2)[0].sum(-1), axis=-1)
    in_grp = jnp.take_along_axis(sb, grp[:, None, None], axis=1)[:, 0]
    _, loc = lax.top_k(in_grp, TOP_K)
    eidx = grp[:, None] * EXPERTS_PER_GROUP + loc
    wts = jnp.take_along_axis(s, eidx, axis=1)
    wts = wts / jnp.sum(wts, axis=-1, keepdims=True)
    m = t * TOP_K
    flat_e = eidx.reshape(-1)
    order = jnp.argsort(flat_e)
    e_sorted = flat_e[order]
    tok = order // TOP_K
    counts = jnp.bincount(flat_e, length=N_EXPERTS)
    padded = (counts + MOE_BLOCK - 1) // MOE_BLOCK * MOE_BLOCK
    pad_end = jnp.cumsum(padded)
    pad_start = pad_end - padded
    seg_start = jnp.cumsum(counts) - counts
    dest = pad_start[e_sorted] + jnp.arange(m) - seg_start[e_sorted]
    nb = -(-m // MOE_BLOCK) + N_EXPERTS
    slot_tok = jnp.full((nb * MOE_BLOCK,), t, jnp.int32).at[dest].set(tok.astype(jnp.int32))
    blk_e = jnp.minimum(jnp.searchsorted(pad_end, jnp.arange(nb) * MOE_BLOCK, side='right'), N_EXPERTS - 1)
    xpad = jnp.concatenate([xf, jnp.zeros((1, d), xf.dtype)], axis=0)
    xb = xpad[slot_tok].reshape(nb, MOE_BLOCK, d)

    def expert_block(args):
        xblk, e = args
        hid = jax.nn.silu(xblk @ w_gate[e]) * (xblk @ w_up[e])
        return hid @ w_down[e]

    yb = lax.map(expert_block, (xb, blk_e)).reshape(-1, d)
    y_sorted = yb[dest] * wts.reshape(-1)[order][:, None].astype(xf.dtype)
    return jnp.zeros_like(xf).at[tok].add(y_sorted)


def setup_inputs(seed: int = 0) -> dict:
    key = jax.random.key(seed)
    keys = jax.random.split(key, 32)
    counter = [0]
    f32 = jnp.float32

    def nk():
        kk = keys[counter[0]]
        counter[0] += 1
        return kk

    def nrm(shape, scale):
        return jax.random.normal(nk(), shape, f32) * scale

    n_ab = (DEPTH + 1) // 2
    n_c = DEPTH // 2
    x = nrm((BATCH, SEQ, D_MODEL), 1.0)
    c = nrm((BATCH, D_MODEL), 1.0)
    ctx = nrm((BATCH, CTX_LEN, D_MODEL), 1.0)
    c_ctx = nrm((D_MODEL,), 1.0)
    w_mod = nrm((DEPTH, D_MODEL, 6 * D_MODEL), 0.5 * D_MODEL ** -0.5)
    b_mod = nrm((DEPTH, 6 * D_MODEL), 0.01)
    norm1_g = 1.0 + nrm((DEPTH, D_MODEL), 0.01)
    norm2_g = 1.0 + nrm((DEPTH, D_MODEL), 0.01)
    final_g = 1.0 + nrm((D_MODEL,), 0.01)
    ab_w_in = nrm((n_ab, D_MODEL, AB_PROJ), D_MODEL ** -0.5)
    ab_w_out = nrm((n_ab, AB_OUT, D_MODEL), AB_OUT ** -0.5)
    ssm_conv_w = nrm((n_ab, SSM_CONV_W, SSM_XBC), SSM_CONV_W ** -0.5)
    ssm_conv_b = nrm((n_ab, SSM_XBC), 0.01)
    ssm_A_log = jnp.log(jax.random.uniform(nk(), (n_ab, 2, SSM_HEADS), f32, 1.0, 16.0))
    dt = jnp.exp(jax.random.uniform(nk(), (n_ab, 2, SSM_HEADS), f32)
                 * (math.log(DT_MAX) - math.log(DT_MIN)) + math.log(DT_MIN))
    ssm_dt_bias = dt + jnp.log(-jnp.expm1(-dt))
    ssm_D = 1.0 + nrm((n_ab, SSM_HEADS), 0.01)
    ssm_norm_g = 1.0 + nrm((n_ab, SSM_INNER), 0.01)
    na_rpb = nrm((n_ab, NA_HEADS, 2 * NA_ROWS - 1, 2 * NA_COLS - 1), 0.1)
    gqa_w_in = nrm((n_c, D_MODEL, GQA_PROJ), D_MODEL ** -0.5)
    gqa_w_out = nrm((n_c, GQA_Q, D_MODEL), GQA_Q ** -0.5)
    gqa_q_norm = 1.0 + nrm((n_c, GQA_HEAD_DIM), 0.01)
    gqa_k_norm = 1.0 + nrm((n_c, GQA_HEAD_DIM), 0.01)
    router_w = nrm((D_MODEL, N_EXPERTS), D_MODEL ** -0.5)
    router_bias = nrm((N_EXPERTS,), 0.01)
    moe_w_gate = nrm((DEPTH, N_EXPERTS, D_MODEL, D_EXPERT), D_MODEL ** -0.5)
    moe_w_up = nrm((DEPTH, N_EXPERTS, D_MODEL, D_EXPERT), D_MODEL ** -0.5)
    moe_w_down = nrm((DEPTH, N_EXPERTS, D_EXPERT, D_MODEL), D_EXPERT ** -0.5)
    return {"x": x, "c": c, "ctx": ctx, "c_ctx": c_ctx, "w_mod": w_mod, "b_mod": b_mod,
            "norm1_g": norm1_g, "norm2_g": norm2_g, "final_g": final_g,
            "ab_w_in": ab_w_in, "ab_w_out": ab_w_out, "ssm_conv_w": ssm_conv_w, "ssm_conv_b": ssm_conv_b,
            "ssm_A_log": ssm_A_log, "ssm_dt_bias": ssm_dt_bias, "ssm_D": ssm_D, "ssm_norm_g": ssm_norm_g,
            "na_rpb": na_rpb, "gqa_w_in": gqa_w_in, "gqa_w_out": gqa_w_out,
            "gqa_q_norm": gqa_q_norm, "gqa_k_norm": gqa_k_norm,
            "router_w": router_w, "router_bias": router_bias,
            "moe_w_gate": moe_w_gate, "moe_w_up": moe_w_up, "moe_w_down": moe_w_down}


def reference(x, c, ctx, c_ctx, w_mod, b_mod, norm1_g, norm2_g, final_g,
              ab_w_in, ab_w_out, ssm_conv_w, ssm_conv_b, ssm_A_log, ssm_dt_bias, ssm_D, ssm_norm_g,
              na_rpb, gqa_w_in, gqa_w_out, gqa_q_norm, gqa_k_norm,
              router_w, router_bias, moe_w_gate, moe_w_up, moe_w_down):
    b, n, _ = x.shape
    cos, sin = _axial_rope_tables(n)
    h = x
    hc = ctx
    for i in range(DEPTH):
        need_ctx = i < DEPTH - 1
        m = (jax.nn.silu(c) @ w_mod[i] + b_mod[i]).reshape(b, 6, D_MODEL)
        mc = (jax.nn.silu(c_ctx) @ w_mod[i] + b_mod[i]).reshape(6, D_MODEL)
        xn = _rmsnorm(h, norm1_g[i]) * (1 + m[:, 1, None]) + m[:, 0, None]
        cn = _rmsnorm(hc, norm1_g[i]) * (1 + mc[1]) + mc[0]
        j = i // 2
        if i % 2 == 0:
            y, yc = _mixer_ssd_na(xn, cn, ab_w_in[j], ab_w_out[j], ssm_conv_w[j], ssm_conv_b[j],
                                  ssm_A_log[j], ssm_dt_bias[j], ssm_D[j], ssm_norm_g[j], na_rpb[j], need_ctx)
        else:
            y, yc = _mixer_gqa(xn, cn, gqa_w_in[j], gqa_w_out[j], gqa_q_norm[j], gqa_k_norm[j],
                               cos, sin, need_ctx)
        h = h + m[:, 2, None] * y
        if need_ctx:
            hc = hc + mc[2] * yc
        xn2 = _rmsnorm(h, norm2_g[i]) * (1 + m[:, 4, None]) + m[:, 3, None]
        if need_ctx:
            cn2 = _rmsnorm(hc, norm2_g[i]) * (1 + mc[4]) + mc[3]
            toks = jnp.concatenate([xn2.reshape(-1, D_MODEL), cn2.reshape(-1, D_MODEL)], axis=0)
            f = _moe(toks, router_w, router_bias, moe_w_gate[i], moe_w_up[i], moe_w_down[i])
            h = h + m[:, 5, None] * f[:b * n].reshape(b, n, D_MODEL)
            hc = hc + mc[5] * f[b * n:].reshape(hc.shape)
        else:
            f = _moe(xn2.reshape(-1, D_MODEL), router_w, router_bias, moe_w_gate[i], moe_w_up[i], moe_w_down[i])
            h = h + m[:, 5, None] * f.reshape(b, n, D_MODEL)
    return _rmsnorm(h, final_g)
```

```python
import functools
import math

import jax
import jax.numpy as jnp
from jax import lax
from jax.experimental import pallas as pl
from jax.experimental.pallas import tpu as pltpu

F32 = jnp.float32
BF16 = jnp.bfloat16
I32 = jnp.int32
HIGHEST = lax.Precision.HIGHEST

NORM_EPS = 1e-6
GRID_W = 64
SSM_HEADS = 16
SSM_HEAD_DIM = 64
SSM_INNER = 1024
SSM_STATE = 128
SSM_GROUPS = 2
SSM_CONV_W = 5
SSD_CHUNK = 128
NA_HEADS = 16
NA_ROWS = 8
NA_COLS = 16
HEAD_DIM = 64
GQA_HEADS = 16
GQA_KV_HEADS = 4
ROPE_THETA = 10000.0
N_EXPERTS = 32
N_GROUPS = 8
EXPERTS_PER_GROUP = 4
D_EXPERT = 512

LANES = 128
HALO_ROWS = 16
TM = 256
MOE_BM = 256
DMA_ROWS = 256
NEG = -1e30
VMEM_LIMIT = 48 * 1024 * 1024


def _cparams(n_axes, vmem=None):
    return pltpu.CompilerParams(dimension_semantics=("arbitrary",) * n_axes,
                                vmem_limit_bytes=vmem)


def _silu(x):
    return x * jax.nn.sigmoid(x)


def _norm_mod(x, g, shift, scale):
    ms = jnp.mean(x * x, axis=-1, keepdims=True)
    y = x * lax.rsqrt(ms + NORM_EPS)
    return (y * g) * (1.0 + scale) + shift


def _mod_kernel(c_ref, w_ref, b_ref, o_ref):
    a = _silu(c_ref[...]).astype(BF16)
    o_ref[0] = jnp.dot(a, w_ref[0].astype(BF16), preferred_element_type=F32) + b_ref[0]


def modulation(cc, w_mod, b_mod):
    depth, d, n = w_mod.shape
    tn = 1536
    return pl.pallas_call(
        _mod_kernel,
        out_shape=jax.ShapeDtypeStruct((depth, 8, n), F32),
        grid=(depth, n // tn),
        in_specs=[pl.BlockSpec((8, d), lambda l, j: (0, 0)),
                  pl.BlockSpec((1, d, tn), lambda l, j: (l, 0, j)),
                  pl.BlockSpec((1, 1, tn), lambda l, j: (l, 0, j))],
        out_specs=pl.BlockSpec((1, 8, tn), lambda l, j: (l, 0, j)),
        compiler_params=_cparams(2, VMEM_LIMIT),
        name="modulation",
    )(cc, w_mod, b_mod.reshape(depth, 1, n))


def _norm_matmul_kernel(h_ref, g_ref, mod_ref, w_ref, o_ref, *, shift_idx, scale_idx):
    xn = _norm_mod(h_ref[...], g_ref[...],
                   mod_ref[0, shift_idx:shift_idx + 1, :], mod_ref[0, scale_idx:scale_idx + 1, :])
    o_ref[...] = jnp.dot(xn.astype(BF16), w_ref[...], preferred_element_type=F32).astype(o_ref.dtype)


def norm_matmul(h, gain, modtab, w, *, shift_idx, scale_idx, mod_index, out_dtype):
    t, d = h.shape
    n = w.shape[1]
    kern = functools.partial(_norm_matmul_kernel, shift_idx=shift_idx, scale_idx=scale_idx)
    return pl.pallas_call(
        kern,
        out_shape=jax.ShapeDtypeStruct((t, n), out_dtype),
        grid=(t // TM,),
        in_specs=[pl.BlockSpec((TM, d), lambda i: (i, 0)),
                  pl.BlockSpec((1, d), lambda i: (0, 0)),
                  pl.BlockSpec((1, 6, d), lambda i: (mod_index(i), 0, 0)),
                  pl.BlockSpec((d, n), lambda i: (0, 0))],
        out_specs=pl.BlockSpec((TM, n), lambda i: (i, 0)),
        compiler_params=_cparams(1, VMEM_LIMIT),
        name="norm_matmul",
    )(h, gain.reshape(1, d), modtab, w)


def _resid_matmul_kernel(*refs, n_pairs, gate_idx):
    h_ref, mod_ref = refs[0], refs[1]
    a_refs = refs[2:2 + n_pairs]
    w_refs = refs[2 + n_pairs:2 + 2 * n_pairs]
    o_ref = refs[2 + 2 * n_pairs]
    acc = jnp.dot(a_refs[0][...], w_refs[0][...], preferred_element_type=F32)
    for a_ref, w_ref in zip(a_refs[1:], w_refs[1:]):
        acc = acc + jnp.dot(a_ref[...], w_ref[...], preferred_element_type=F32)
    o_ref[...] = h_ref[...] + mod_ref[0, gate_idx:gate_idx + 1, :] * acc


def resid_matmul(h, modtab, a_list, w_list, *, gate_idx, n_blocks, h_block, mod_index):
    d = h.shape[1]
    n_pairs = len(a_list)
    kern = functools.partial(_resid_matmul_kernel, n_pairs=n_pairs, gate_idx=gate_idx)
    in_specs = [pl.BlockSpec((TM, d), lambda i: (h_block(i), 0)),
                pl.BlockSpec((1, 6, d), lambda i: (mod_index(i), 0, 0))]
    in_specs += [pl.BlockSpec((TM, a.shape[1]), lambda i: (i, 0)) for a in a_list]
    in_specs += [pl.BlockSpec(w.shape, lambda i: (0, 0)) for w in w_list]
    return pl.pallas_call(
        kern,
        out_shape=jax.ShapeDtypeStruct((n_blocks * TM, d), F32),
        grid=(n_blocks,),
        in_specs=in_specs,
        out_specs=pl.BlockSpec((TM, d), lambda i: (i, 0)),
        compiler_params=_cparams(1, VMEM_LIMIT),
        name="resid_matmul",
    )(h, modtab, *a_list, *w_list)


def _dwconv_kernel(cur_ref, prev_ref, next_ref, w_ref, b_ref, o_ref, ext_ref, *, chunks_per_batch, ctx_chunks):
    c = pl.program_id(0) % chunks_per_batch
    is_start = jnp.logical_or(c == 0, c == ctx_chunks)
    is_end = jnp.logical_or(c == ctx_chunks - 1, c == chunks_per_batch - 1)
    half = SSM_CONV_W // 2
    hr = HALO_ROWS
    ext_ref[0:hr, :] = jnp.where(is_start, 0.0, prev_ref[...].astype(F32))
    ext_ref[hr:hr + SSD_CHUNK, :] = cur_ref[...].astype(F32)
    ext_ref[hr + SSD_CHUNK:2 * hr + SSD_CHUNK, :] = jnp.where(is_end, 0.0, next_ref[...].astype(F32))
    acc = b_ref[...] + w_ref[0:1, :] * ext_ref[hr - half:hr - half + SSD_CHUNK, :]
    for tap in range(1, SSM_CONV_W):
        lo = hr - half + tap
        acc = acc + w_ref[tap:tap + 1, :] * ext_ref[lo:lo + SSD_CHUNK, :]
    o_ref[...] = _silu(acc).astype(o_ref.dtype)


def dwconv_silu(p, col_block, width, w, b, *, chunks_per_batch, ctx_chunks):
    t = p.shape[0]
    n_chunks = t // SSD_CHUNK
    sub = SSD_CHUNK // HALO_ROWS
    kern = functools.partial(_dwconv_kernel, chunks_per_batch=chunks_per_batch, ctx_chunks=ctx_chunks)
    return pl.pallas_call(
        kern,
        out_shape=jax.ShapeDtypeStruct((t, width), BF16),
        grid=(n_chunks,),
        in_specs=[pl.BlockSpec((SSD_CHUNK, width), lambda i: (i, col_block)),
                  pl.BlockSpec((HALO_ROWS, width), lambda i: (jnp.maximum(i * sub - 1, 0), col_block)),
                  pl.BlockSpec((HALO_ROWS, width),
                               lambda i: (jnp.minimum((i + 1) * sub, t // HALO_ROWS - 1), col_block)),
                  pl.BlockSpec((SSM_CONV_W, width), lambda i: (0, 0)),
                  pl.BlockSpec((1, width), lambda i: (0, 0))],
        out_specs=pl.BlockSpec((SSD_CHUNK, width), lambda i: (i, 0)),
        scratch_shapes=[pltpu.VMEM((SSD_CHUNK + 2 * HALO_ROWS, width), F32)],
        compiler_params=_cparams(1),
        name="dwconv_silu",
    )(p, p, p, w, b.reshape(1, width))


def _ssd_kernel(*refs, reverse, final):
    if final:
        (x_ref, bc_ref, dt_ref, dtb_ref, a_ref, z_ref, yprev_ref, dskip_ref, ng_ref, y_ref, state_ref) = refs
    else:
        (x_ref, bc_ref, dt_ref, dtb_ref, a_ref, y_ref, state_ref) = refs
    ch = SSD_CHUNK
    n = SSM_STATE
    inner = SSM_INNER
    gw = inner // SSM_GROUPS
    col0 = SSM_HEADS if reverse else 0

    @pl.when(pl.program_id(1) == 0)
    def _():
        state_ref[...] = jnp.zeros_like(state_ref)

    x = x_ref[...].astype(F32)
    bc = bc_ref[...]
    raw = dt_ref[...] + dtb_ref[...]
    dt = jnp.maximum(raw, 0.0) + jnp.log(1.0 + jnp.exp(-jnp.abs(raw)))
    adt = dt * a_ref[...]

    ri = lax.broadcasted_iota(I32, (ch, ch), 0)
    ci = lax.broadcasted_iota(I32, (ch, ch), 1)
    if reverse:
        tri = ci >= ri
        tri_t = ri >= ci
    else:
        tri = ci <= ri
        tri_t = ri <= ci
    acum = jnp.dot(tri.astype(F32), adt, precision=HIGHEST, preferred_element_type=F32)
    acum_t = jnp.dot(adt.T, tri_t.astype(F32), precision=HIGHEST, preferred_element_type=F32)
    last = 0 if reverse else ch - 1
    total = acum[last:last + 1, :]

    hrow = lax.broadcasted_iota(I32, (LANES, inner), 0)
    hcol = lax.broadcasted_iota(I32, (LANES, inner), 1) // SSM_HEAD_DIM
    sel = (hrow == hcol + col0).astype(F32)

    def expand(v):
        return jnp.dot(v, sel, precision=HIGHEST, preferred_element_type=F32)

    dt_x = expand(dt)
    ea_x = expand(jnp.exp(acum))
    ds_x = expand(jnp.exp(total - acum))
    tot_x = expand(jnp.broadcast_to(jnp.exp(total), (8, LANES)))[0:1, :]

    xd = x * dt_x
    xw = (xd * ds_x).astype(BF16)
    xd_b = xd.astype(BF16)
    state = state_ref[...]
    lane = lax.broadcasted_iota(I32, (ch, LANES), 1)
    first_half = lane < SSM_HEAD_DIM
    y_parts = []

    for g in range(SSM_GROUPS):
        b_g = bc[:, g * n:(g + 1) * n]
        c_g = bc[:, (SSM_GROUPS + g) * n:(SSM_GROUPS + g + 1) * n]
        cb = lax.dot_general(c_g, b_g, (((1,), (1,)), ((), ())), preferred_element_type=F32)
        gs = slice(g * gw, (g + 1) * gw)
        y_off = jnp.dot(c_g, state[:, gs].astype(BF16), preferred_element_type=F32) * ea_x[:, gs]
        b_t = b_g.astype(F32).T.astype(BF16)
        new_state = tot_x[:, gs] * state[:, gs] + jnp.dot(b_t, xw[:, gs], preferred_element_type=F32)
        state_ref[:, gs] = new_state
        for pair in range(gw // LANES):
            lanes = slice(g * gw + pair * LANES, g * gw + (pair + 1) * LANES)
            halves = []
            for sub in range(2):
                head = (g * gw + pair * LANES) // SSM_HEAD_DIM + sub
                cidx = col0 + head
                diff = acum[:, cidx:cidx + 1] - acum_t[cidx:cidx + 1, :]
                m_h = (cb * jnp.where(tri, jnp.exp(jnp.where(tri, diff, 0.0)), 0.0)).astype(BF16)
                halves.append(jnp.dot(m_h, xd_b[:, lanes], preferred_element_type=F32))
            y_parts.append(jnp.where(first_half, halves[0], halves[1]) + y_off[:, pair * LANES:(pair + 1) * LANES])

    y = jnp.concatenate(y_parts, axis=1)
    if final:
        y = y + yprev_ref[...] + x * dskip_ref[...]
        y = y * _silu(z_ref[...].astype(F32))
        ms = jnp.mean(y * y, axis=-1, keepdims=True)
        y = (y * lax.rsqrt(ms + NORM_EPS)) * ng_ref[...]
    y_ref[...] = y


def ssd_scan(xc, bcc, dtr, dtb, a_row, *, batch, chunks_per_batch, ctx_chunks, reverse,
             z_src=None, z_col=None, y_prev=None, d_skip=None, norm_g=None):
    t, inner = xc.shape
    final = y_prev is not None
    npb = chunks_per_batch

    def chunk(b, s):
        if not reverse:
            c = s
        else:
            c = jnp.where(s < ctx_chunks, ctx_chunks - 1 - s, npb - 1 + ctx_chunks - s)
        return b * npb + c

    row = lambda b, s: (chunk(b, s), 0)
    const = lambda b, s: (0, 0)
    in_specs = [pl.BlockSpec((SSD_CHUNK, inner), row),
                pl.BlockSpec((SSD_CHUNK, bcc.shape[1]), row),
                pl.BlockSpec((SSD_CHUNK, LANES), row),
                pl.BlockSpec((1, LANES), const),
                pl.BlockSpec((1, LANES), const)]
    args = [xc, bcc, dtr, dtb, a_row]
    if final:
        in_specs += [pl.BlockSpec((SSD_CHUNK, inner), lambda b, s: (chunk(b, s), z_col)),
                     pl.BlockSpec((SSD_CHUNK, inner), row),
                     pl.BlockSpec((1, inner), const),
                     pl.BlockSpec((1, inner), const)]
        args += [z_src, y_prev, d_skip, norm_g]
    kern = functools.partial(_ssd_kernel, reverse=reverse, final=final)
    return pl.pallas_call(
        kern,
        out_shape=jax.ShapeDtypeStruct((t, inner), F32),
        grid=(batch, npb),
        in_specs=in_specs,
        out_specs=pl.BlockSpec((SSD_CHUNK, inner), row),
        scratch_shapes=[pltpu.VMEM((SSM_STATE, inner), F32)],
        compiler_params=_cparams(2, VMEM_LIMIT),
        name="ssd_bwd" if reverse else "ssd_fwd",
    )(*args)


def _na_bias_kernel(rpb_ref, o_ref):
    cfg = pl.program_id(0)
    head = pl.program_id(1)
    w = GRID_W
    qi = lax.broadcasted_iota(I32, (w, w), 0)
    ki = lax.broadcasted_iota(I32, (w, w), 1)
    cstart = jnp.clip(qi - NA_COLS // 2, 0, w - NA_COLS)
    col_ok = jnp.logical_and(ki >= cstart, ki < cstart + NA_COLS)
    col_idx = jnp.clip(ki - qi + NA_COLS - 1, 0, 2 * NA_COLS - 2)
    n_cfg = pl.num_programs(0)
    for j in range(NA_ROWS):
        rho = jnp.clip(j - cfg + NA_ROWS - 1, 0, 2 * NA_ROWS - 2)
        tile = jnp.zeros((w, w), F32)
        for tcol in range(2 * NA_COLS - 1):
            tile = jnp.where(col_idx == tcol, rpb_ref[head, rho, tcol], tile)
        tile = jnp.where(col_ok, tile, NEG)
        tile = jnp.where(cfg == n_cfg - 1, NEG, tile)
        o_ref[0, 0, :, j * w:(j + 1) * w] = tile


def na_bias_tables(rpb):
    heads = rpb.shape[0]
    n_cfg = NA_ROWS + 1
    return pl.pallas_call(
        _na_bias_kernel,
        out_shape=jax.ShapeDtypeStruct((n_cfg, heads, GRID_W, NA_ROWS * GRID_W), F32),
        grid=(n_cfg, heads),
        in_specs=[pl.BlockSpec(memory_space=pltpu.SMEM)],
        out_specs=pl.BlockSpec((1, 1, GRID_W, NA_ROWS * GRID_W), lambda c, h: (c, h, 0, 0)),
        compiler_params=_cparams(2),
        name="na_bias_tables",
    )(rpb)


def _na_kernel(*refs):
    q_ref = refs[0]
    k_refs = refs[1:1 + NA_ROWS]
    v_refs = refs[1 + NA_ROWS:1 + 2 * NA_ROWS]
    kc_ref, vc_ref, bias_ref, o_ref = refs[1 + 2 * NA_ROWS:]
    w = GRID_W
    scale = HEAD_DIM ** -0.5
    lane_q = lax.broadcasted_iota(I32, (w, LANES), 1)
    first_half = lane_q < HEAD_DIM
    nt = (((1,), (1,)), ((), ()))
    for pair in range(NA_HEADS // 2):
        lanes = slice(pair * LANES, (pair + 1) * LANES)
        q2 = q_ref[:, lanes]
        k2 = jnp.concatenate([r[:, lanes] for r in k_refs], axis=0)
        v2 = jnp.concatenate([r[:, lanes] for r in v_refs], axis=0)
        kc2 = kc_ref[:, lanes]
        vc2 = vc_ref[:, lanes]
        outs = []
        for sub in range(2):
            head = 2 * pair + sub
            keep = first_half if sub == 0 else jnp.logical_not(first_half)
            qz = jnp.where(keep, q2, jnp.zeros_like(q2))
            s_lat = lax.dot_general(qz, k2, nt, preferred_element_type=F32) * scale + bias_ref[0, head]
            s_ctx = lax.dot_general(qz, kc2, nt, preferred_element_type=F32) * scale
            m = jnp.maximum(jnp.max(s_lat, axis=-1, keepdims=True), jnp.max(s_ctx, axis=-1, keepdims=True))
            p_lat = jnp.exp(s_lat - m)
            p_ctx = jnp.exp(s_ctx - m)
            denom = jnp.sum(p_lat, axis=-1, keepdims=True) + jnp.sum(p_ctx, axis=-1, keepdims=True)
            o = (jnp.dot(p_lat.astype(BF16), v2, preferred_element_type=F32)
                 + jnp.dot(p_ctx.astype(BF16), vc2, preferred_element_type=F32))
            outs.append(o / denom)
        o_ref[:, lanes] = jnp.where(first_half, outs[0], outs[1]).astype(o_ref.dtype)


def na_attention(p, bias, *, batch, rows_per_batch, ctx_len, q_col, k_col, v_col):
    t = p.shape[0]
    w = GRID_W
    width = NA_HEADS * HEAD_DIM
    blocks_per_batch = rows_per_batch + ctx_len // w
    ctx_blocks = ctx_len // w
    lat_rows = rows_per_batch

    def win_start(st):
        r = st - ctx_blocks
        return jnp.clip(r - NA_ROWS // 2, 0, lat_rows - NA_ROWS)

    def cfg_of(st):
        r = st - ctx_blocks
        return jnp.where(st < ctx_blocks, NA_ROWS, r - win_start(st))

    def kv_spec(j, col):
        return pl.BlockSpec((w, width), lambda b, st: (b * blocks_per_batch + ctx_blocks + win_start(st) + j, col))

    in_specs = [pl.BlockSpec((w, width), lambda b, st: (b * blocks_per_batch + st, q_col))]
    in_specs += [kv_spec(j, k_col) for j in range(NA_ROWS)]
    in_specs += [kv_spec(j, v_col) for j in range(NA_ROWS)]
    blocks_ctx = (rows_per_batch * w + ctx_len) // ctx_len
    in_specs += [pl.BlockSpec((ctx_len, width), lambda b, st: (b * blocks_ctx, k_col)),
                 pl.BlockSpec((ctx_len, width), lambda b, st: (b * blocks_ctx, v_col)),
                 pl.BlockSpec((1, NA_HEADS, w, NA_ROWS * w), lambda b, st: (cfg_of(st), 0, 0, 0))]
    args = [p] * (3 + 2 * NA_ROWS) + [bias]
    return pl.pallas_call(
        _na_kernel,
        out_shape=jax.ShapeDtypeStruct((t, width), BF16),
        grid=(batch, blocks_per_batch),
        in_specs=in_specs,
        out_specs=pl.BlockSpec((w, width), lambda b, st: (b * blocks_per_batch + st, 0)),
        compiler_params=_cparams(2, VMEM_LIMIT),
        name="na_attention",
    )(*args)


def _group_mean_sq(x):
    gi = lax.broadcasted_iota(I32, (LANES, LANES), 0) // HEAD_DIM
    gj = lax.broadcasted_iota(I32, (LANES, LANES), 1) // HEAD_DIM
    ones = (gi == gj).astype(F32)
    return jnp.dot(x * x, ones, precision=HIGHEST, preferred_element_type=F32) * (1.0 / HEAD_DIM)


def _rope(x, cos_f, sin_s):
    lane = lax.broadcasted_iota(I32, x.shape, 1)
    low = (lane % HEAD_DIM) < HEAD_DIM // 2
    partner = jnp.where(low, pltpu.roll(x, LANES - HEAD_DIM // 2, 1), pltpu.roll(x, HEAD_DIM // 2, 1))
    return x * cos_f + partner * sin_s


def _gqa_prep_kernel(p_ref, cos_ref, sin_ref, qg_ref, kg_ref, qt_ref, k_ref, vt_ref, *, ctx_blocks):
    j = pl.program_id(1)
    is_lat = j >= ctx_blocks
    q_w = GQA_HEADS * HEAD_DIM
    kv_w = GQA_KV_HEADS * HEAD_DIM
    cos_f = jnp.where(is_lat, cos_ref[...], 1.0)
    sin_s = jnp.where(is_lat, sin_ref[...], 0.0)

    def normed(cols, gain):
        x = p_ref[:, cols].astype(F32)
        y = x * lax.rsqrt(_group_mean_sq(x) + NORM_EPS)
        return _rope(y * gain, cos_f, sin_s)

    for tile in range(kv_w // LANES):
        cols = slice(q_w + tile * LANES, q_w + (tile + 1) * LANES)
        k_ref[:, tile * LANES:(tile + 1) * LANES] = normed(cols, kg_ref[...]).astype(k_ref.dtype)
    v = p_ref[:, q_w + kv_w:q_w + 2 * kv_w].astype(F32)
    vt_ref[0] = v.T.astype(vt_ref.dtype)

    @pl.when(is_lat)
    def _():
        scale = HEAD_DIM ** -0.5
        for tile in range(q_w // LANES):
            cols = slice(tile * LANES, (tile + 1) * LANES)
            q = normed(cols, qg_ref[...]) * scale
            qt_ref[0, cols, :] = q.T.astype(qt_ref.dtype)


def gqa_prep(p1, cos_f, sin_s, qg, kg, *, batch, blocks_per_batch, ctx_blocks):
    t, n = p1.shape
    q_w = GQA_HEADS * HEAD_DIM
    kv_w = GQA_KV_HEADS * HEAD_DIM
    lb = blocks_per_batch * TM
    lat = (blocks_per_batch - ctx_blocks) * TM
    kern = functools.partial(_gqa_prep_kernel, ctx_blocks=ctx_blocks)
    lat_blk = lambda b, j: jnp.maximum(j - ctx_blocks, 0)
    return pl.pallas_call(
        kern,
        out_shape=(jax.ShapeDtypeStruct((batch, q_w, lat), BF16),
                   jax.ShapeDtypeStruct((t, kv_w), BF16),
                   jax.ShapeDtypeStruct((batch, kv_w, lb), BF16)),
        grid=(batch, blocks_per_batch),
        in_specs=[pl.BlockSpec((TM, n), lambda b, j: (b * blocks_per_batch + j, 0)),
                  pl.BlockSpec((TM, LANES), lambda b, j: (lat_blk(b, j), 0)),
                  pl.BlockSpec((TM, LANES), lambda b, j: (lat_blk(b, j), 0)),
                  pl.BlockSpec((1, LANES), lambda b, j: (0, 0)),
                  pl.BlockSpec((1, LANES), lambda b, j: (0, 0))],
        out_specs=(pl.BlockSpec((1, q_w, TM), lambda b, j: (b, 0, lat_blk(b, j))),
                   pl.BlockSpec((TM, kv_w), lambda b, j: (b * blocks_per_batch + j, 0)),
                   pl.BlockSpec((1, kv_w, TM), lambda b, j: (b, 0, j))),
        compiler_params=_cparams(2, VMEM_LIMIT),
        name="gqa_prep",
    )(p1, cos_f, sin_s, qg, kg)


def _gqa_attn_kernel(qt_ref, k_ref, vt_ref, o_ref, *, tk, n_kblocks):
    kv = pl.program_id(1)
    tq = qt_ref.shape[2]
    group = GQA_HEADS // GQA_KV_HEADS
    second = (kv % 2) == 1
    outs = []
    for hq in range(group):
        q_h = qt_ref[0, hq * HEAD_DIM:(hq + 1) * HEAD_DIM, :]
        zero = jnp.zeros_like(q_h)
        qz = jnp.concatenate([jnp.where(second, zero, q_h), jnp.where(second, q_h, zero)], axis=0)

        def step(kb, carry):
            m, l, acc = carry
            off = pl.multiple_of(kb * tk, tk)
            k_blk = k_ref[0, pl.ds(off, tk), :]
            s = jnp.dot(k_blk, qz, preferred_element_type=F32)
            m_new = jnp.maximum(m, jnp.max(s, axis=0, keepdims=True))
            alpha = jnp.exp(m - m_new)
            p = jnp.exp(s - m_new)
            l = alpha * l + jnp.sum(p, axis=0, keepdims=True)
            v_blk = vt_ref[0, :, pl.ds(off, tk)]
            acc = alpha * acc + jnp.dot(v_blk, p.astype(BF16), preferred_element_type=F32)
            return m_new, l, acc

        init = (jnp.full((1, tq), NEG, F32), jnp.zeros((1, tq), F32), jnp.zeros((HEAD_DIM, tq), F32))
        m, l, acc = lax.fori_loop(0, n_kblocks, step, init)
        outs.append(acc / l)
    for pair in range(group // 2):
        o_t = jnp.concatenate([outs[2 * pair], outs[2 * pair + 1]], axis=0)
        o_ref[:, pair * LANES:(pair + 1) * LANES] = o_t.T.astype(o_ref.dtype)


def gqa_attention(qt, k, vt, *, tq=512, tk=256):
    batch, q_w, lat = qt.shape
    lb = k.shape[1]
    group = GQA_HEADS // GQA_KV_HEADS
    gw = group * HEAD_DIM
    tq = min(tq, lat)
    kern = functools.partial(_gqa_attn_kernel, tk=tk, n_kblocks=lb // tk)
    return pl.pallas_call(
        kern,
        out_shape=jax.ShapeDtypeStruct((batch * lat, q_w), BF16),
        grid=(batch, GQA_KV_HEADS, lat // tq),
        in_specs=[pl.BlockSpec((1, gw, tq), lambda b, h, i: (b, h, i)),
                  pl.BlockSpec((1, lb, LANES), lambda b, h, i: (b, 0, h // 2)),
                  pl.BlockSpec((1, HEAD_DIM, lb), lambda b, h, i: (b, h, 0))],
        out_specs=pl.BlockSpec((tq, gw), lambda b, h, i: (b * (lat // tq) + i, h)),
        compiler_params=_cparams(3, VMEM_LIMIT),
        name="gqa_attention",
    )(qt, k, vt)


def _router_kernel(h_ref, g_ref, mod_ref, rw_ref, rb_ref, xn_ref, eidx_ref, wts_ref, rank_ref, cnt_ref, run_ref):
    @pl.when(pl.program_id(0) == 0)
    def _():
        run_ref[...] = jnp.zeros_like(run_ref)

    xn = _norm_mod(h_ref[...], g_ref[...], mod_ref[0, 3:4, :], mod_ref[0, 4:5, :])
    xn_ref[...] = xn
    tm = xn.shape[0]
    logits = lax.dot_general(rw_ref[...], xn, (((1,), (1,)), ((), ())),
                             precision=HIGHEST, preferred_element_type=F32)
    s = jax.nn.sigmoid(logits)
    sb = s + rb_ref[...]
    row = lambda a, e: a[e:e + 1, :]

    best = None
    for g in range(N_GROUPS):
        a = [row(sb, g * EXPERTS_PER_GROUP + j) for j in range(EXPERTS_PER_GROUP)]
        score = a[0] + a[1]
        for i in range(EXPERTS_PER_GROUP):
            for j in range(i + 1, EXPERTS_PER_GROUP):
                if (i, j) != (0, 1):
                    score = jnp.maximum(score, a[i] + a[j])
        if best is None:
            best, grp = score, jnp.zeros((1, tm), I32)
        else:
            better = score > best
            best = jnp.where(better, score, best)
            grp = jnp.where(better, g, grp)

    biased, plain = [], []
    for j in range(EXPERTS_PER_GROUP):
        vb = row(sb, j)
        vp = row(s, j)
        for g in range(1, N_GROUPS):
            pick = grp == g
            vb = jnp.where(pick, row(sb, g * EXPERTS_PER_GROUP + j), vb)
            vp = jnp.where(pick, row(s, g * EXPERTS_PER_GROUP + j), vp)
        biased.append(vb)
        plain.append(vp)

    def argmax_first(vals):
        top, idx = vals[0], jnp.zeros((1, tm), I32)
        for j in range(1, len(vals)):
            better = vals[j] > top
            top = jnp.where(better, vals[j], top)
            idx = jnp.where(better, j, idx)
        return idx

    i1 = argmax_first(biased)
    i2 = argmax_first([jnp.where(i1 == j, -jnp.inf, biased[j]) for j in range(EXPERTS_PER_GROUP)])

    def pick_plain(idx):
        out = plain[0]
        for j in range(1, EXPERTS_PER_GROUP):
            out = jnp.where(idx == j, plain[j], out)
        return out

    w1, w2 = pick_plain(i1), pick_plain(i2)
    wsum = w1 + w2
    e1 = grp * EXPERTS_PER_GROUP + i1
    e2 = grp * EXPERTS_PER_GROUP + i2
    eidx_ref[...] = jnp.concatenate([e1, e2], axis=0)
    wts_ref[...] = jnp.concatenate([w1 / wsum, w2 / wsum], axis=0)

    erow = lax.broadcasted_iota(I32, (N_EXPERTS, tm), 0)
    hit1 = erow == e1
    hit2 = erow == e2
    onehot = jnp.logical_or(hit1, hit2)
    si = lax.broadcasted_iota(I32, (tm, tm), 0)
    ti = lax.broadcasted_iota(I32, (tm, tm), 1)
    before = jnp.dot(onehot.astype(BF16), (si < ti).astype(BF16), preferred_element_type=F32)
    before = before + run_ref[...][:, 0:1]
    r1 = jnp.sum(jnp.where(hit1, before, 0.0), axis=0, keepdims=True)
    r2 = jnp.sum(jnp.where(hit2, before, 0.0), axis=0, keepdims=True)
    rank_ref[...] = jnp.concatenate([r1, r2], axis=0).astype(I32)
    totals = run_ref[...] + jnp.sum(onehot.astype(F32), axis=1, keepdims=True)
    run_ref[...] = totals
    cnt_ref[...] = totals


def moe_router(h, gain, modtab, rw_t, rb, *, mod_index):
    t, d = h.shape
    return pl.pallas_call(
        _router_kernel,
        out_shape=(jax.ShapeDtypeStruct((t, d), F32),
                   jax.ShapeDtypeStruct((2, t), I32),
                   jax.ShapeDtypeStruct((2, t), F32),
                   jax.ShapeDtypeStruct((2, t), I32),
                   jax.ShapeDtypeStruct((N_EXPERTS, LANES), F32)),
        grid=(t // TM,),
        in_specs=[pl.BlockSpec((TM, d), lambda i: (i, 0)),
                  pl.BlockSpec((1, d), lambda i: (0, 0)),
                  pl.BlockSpec((1, 6, d), lambda i: (mod_index(i), 0, 0)),
                  pl.BlockSpec((N_EXPERTS, d), lambda i: (0, 0)),
                  pl.BlockSpec((N_EXPERTS, 1), lambda i: (0, 0))],
        out_specs=(pl.BlockSpec((TM, d), lambda i: (i, 0)),
                   pl.BlockSpec((2, TM), lambda i: (0, i)),
                   pl.BlockSpec((2, TM), lambda i: (0, i)),
                   pl.BlockSpec((2, TM), lambda i: (0, i)),
                   pl.BlockSpec((N_EXPERTS, LANES), lambda i: (0, 0))),
        scratch_shapes=[pltpu.VMEM((N_EXPERTS, LANES), F32)],
        compiler_params=_cparams(1, VMEM_LIMIT),
        name="moe_router",
    )(h, gain.reshape(1, d), modtab, rw_t, rb)


def _row_copy(src_ref, src_row, dst_ref, dst_row, sem):
    return pltpu.make_async_copy(src_ref.at[pl.ds(src_row, 1)], dst_ref.at[pl.ds(dst_row, 1)], sem)


def _dispatch_kernel(slot_ref, x_ref, xs_in_ref, xs_ref, sem):
    del xs_in_ref
    rows = x_ref.shape[0]

    def issue(r, carry):
        for k in range(2):
            _row_copy(x_ref, r, xs_ref, slot_ref[0, k, r], sem).start()
        return carry

    def drain(r, carry):
        for k in range(2):
            _row_copy(x_ref, 0, xs_ref, 0, sem).wait()
        return carry

    lax.fori_loop(0, rows, issue, 0)
    lax.fori_loop(0, rows, drain, 0)


def moe_dispatch(xn, slot3, n_slots):
    t, d = xn.shape
    xs0 = jnp.zeros((n_slots, d), xn.dtype)
    return pl.pallas_call(
        _dispatch_kernel,
        out_shape=jax.ShapeDtypeStruct((n_slots, d), xn.dtype),
        grid=(t // DMA_ROWS,),
        in_specs=[pl.BlockSpec((1, 2, DMA_ROWS), lambda i: (i, 0, 0), memory_space=pltpu.SMEM),
                  pl.BlockSpec((DMA_ROWS, d), lambda i: (i, 0)),
                  pl.BlockSpec(memory_space=pl.ANY)],
        out_specs=pl.BlockSpec(memory_space=pl.ANY),
        scratch_shapes=[pltpu.SemaphoreType.DMA(())],
        input_output_aliases={2: 0},
        compiler_params=_cparams(1),
        name="moe_dispatch",
    )(slot3, xn, xs0)


def _expert_kernel(blk_e_ref, nused_ref, x_ref, wg_ref, wu_ref, wd_ref, y_ref):
    del blk_e_ref

    @pl.when(pl.program_id(0) < nused_ref[0])
    def _():
        x = x_ref[...].astype(BF16)
        gate = jnp.dot(x, wg_ref[0], preferred_element_type=F32)
        up = jnp.dot(x, wu_ref[0], preferred_element_type=F32)
        hid = (_silu(gate) * up).astype(BF16)
        y_ref[...] = jnp.dot(hid, wd_ref[0], preferred_element_type=F32)

    @pl.when(pl.program_id(0) >= nused_ref[0])
    def _():
        y_ref[...] = jnp.zeros_like(y_ref)


def moe_experts(xs, blk_e, nused, wg, wu, wd):
    n_slots, d = xs.shape
    de = wg.shape[2]
    grid_spec = pltpu.PrefetchScalarGridSpec(
        num_scalar_prefetch=2,
        grid=(n_slots // MOE_BM,),
        in_specs=[pl.BlockSpec((MOE_BM, d), lambda i, be, nu: (i, 0)),
                  pl.BlockSpec((1, d, de), lambda i, be, nu: (be[i], 0, 0)),
                  pl.BlockSpec((1, d, de), lambda i, be, nu: (be[i], 0, 0)),
                  pl.BlockSpec((1, de, d), lambda i, be, nu: (be[i], 0, 0))],
        out_specs=pl.BlockSpec((MOE_BM, d), lambda i, be, nu: (i, 0)),
    )
    return pl.pallas_call(
        _expert_kernel,
        out_shape=jax.ShapeDtypeStruct((n_slots, d), F32),
        grid_spec=grid_spec,
        compiler_params=_cparams(1, VMEM_LIMIT),
        name="moe_experts",
    )(blk_e, nused, xs, wg, wu, wd)


def _combine_kernel(slot_ref, h_ref, mod_ref, w_ref, fg_ref, yb_ref, o_ref, buf_ref, sem, *, final):
    rows = h_ref.shape[0]

    def issue(r, carry):
        for k in range(2):
            _row_copy(yb_ref, slot_ref[0, k, r], buf_ref.at[k], r, sem).start()
        return carry

    def drain(r, carry):
        for k in range(2):
            _row_copy(yb_ref, 0, buf_ref.at[k], 0, sem).wait()
        return carry

    lax.fori_loop(0, rows, issue, 0)
    lax.fori_loop(0, rows, drain, 0)
    f = w_ref[:, 0:1] * buf_ref[0] + w_ref[:, 1:2] * buf_ref[1]
    h = h_ref[...] + mod_ref[0, 5:6, :] * f
    if final:
        ms = jnp.mean(h * h, axis=-1, keepdims=True)
        h = (h * lax.rsqrt(ms + NORM_EPS)) * fg_ref[...]
    o_ref[...] = h


def moe_combine(h, modtab, wts_t, final_g, yb, slot3, *, mod_index, final):
    t, d = h.shape
    kern = functools.partial(_combine_kernel, final=final)
    return pl.pallas_call(
        kern,
        out_shape=jax.ShapeDtypeStruct((t, d), F32),
        grid=(t // DMA_ROWS,),
        in_specs=[pl.BlockSpec((1, 2, DMA_ROWS), lambda i: (i, 0, 0), memory_space=pltpu.SMEM),
                  pl.BlockSpec((DMA_ROWS, d), lambda i: (i, 0)),
                  pl.BlockSpec((1, 6, d), lambda i: (mod_index(i), 0, 0)),
                  pl.BlockSpec((DMA_ROWS, 2), lambda i: (i, 0)),
                  pl.BlockSpec((1, d), lambda i: (0, 0)),
                  pl.BlockSpec(memory_space=pl.ANY)],
        out_specs=pl.BlockSpec((DMA_ROWS, d), lambda i: (i, 0)),
        scratch_shapes=[pltpu.VMEM((2, DMA_ROWS, d), F32), pltpu.SemaphoreType.DMA(())],
        compiler_params=_cparams(1, VMEM_LIMIT),
        name="moe_combine",
    )(slot3, h, modtab, wts_t, final_g.reshape(1, d), yb)


def moe_layer(h, norm_g, modtab, rw_t, rb, wg, wu, wd, final_g, *, mod_index, final):
    t, d = h.shape
    xn, eidx, wts, rank, counts = moe_router(h, norm_g, modtab, rw_t, rb, mod_index=mod_index)
    n_blocks = (2 * t) // MOE_BM + N_EXPERTS
    n_slots = n_blocks * MOE_BM
    cnt = counts[:, 0].astype(I32)
    padded = (cnt + MOE_BM - 1) // MOE_BM * MOE_BM
    pad_end = jnp.cumsum(padded)
    pad_start = pad_end - padded
    slot = pad_start[eidx] + rank
    slot3 = slot.reshape(2, t // DMA_ROWS, DMA_ROWS).transpose(1, 0, 2)
    blk_start = jnp.arange(n_blocks, dtype=I32) * MOE_BM
    blk_e = jnp.minimum(jnp.sum((pad_end[None, :] <= blk_start[:, None]).astype(I32), axis=1), N_EXPERTS - 1)
    nused = (pad_end[-1] // MOE_BM).reshape(1).astype(I32)
    xs = moe_dispatch(xn, slot3, n_slots)
    yb = moe_experts(xs, blk_e.astype(I32), nused, wg, wu, wd)
    return moe_combine(h, modtab, wts.T, final_g, yb, slot3, mod_index=mod_index, final=final)


def _rope_tables(lat):
    t = jnp.arange(lat)
    row = (t // GRID_W).astype(F32)
    col = (t % GRID_W).astype(F32)
    n_freq = HEAD_DIM // 4
    inv = ROPE_THETA ** (-jnp.arange(n_freq, dtype=F32) / n_freq)
    ang = jnp.concatenate([row[:, None] * inv, col[:, None] * inv], axis=-1)
    cos, sin = jnp.cos(ang), jnp.sin(ang)
    cos_f = jnp.tile(cos, (1, LANES // (HEAD_DIM // 2)))
    sin_s = jnp.tile(jnp.concatenate([-sin, sin], axis=-1), (1, LANES // HEAD_DIM))
    return cos_f, sin_s


def kernel(x, c, ctx, c_ctx, w_mod, b_mod, norm1_g, norm2_g, final_g, ab_w_in, ab_w_out, ssm_conv_w, ssm_conv_b, ssm_A_log, ssm_dt_bias, ssm_D, ssm_norm_g, na_rpb, gqa_w_in, gqa_w_out, gqa_q_norm, gqa_k_norm, router_w, router_bias, moe_w_gate, moe_w_up, moe_w_down):
    bsz, lat, d = x.shape
    lc = ctx.shape[1]
    lb = lc + lat
    assert lat % TM == 0 and lc % TM == 0 and lat % GRID_W == 0 and lat // GRID_W >= NA_ROWS
    assert bsz + 1 <= 8 and w_mod.shape[0] == 2
    nb = lb // TM
    cb = lc // TM
    nlb = lat // TM

    def mod_comb(i):
        return jnp.where(i % nb < cb, bsz, i // nb)

    def mod_lat(i):
        return i // nlb

    inner, xbc_w = SSM_INNER, SSM_INNER + 2 * SSM_GROUPS * SSM_STATE
    w_in = ab_w_in[0]
    o_dt = inner + xbc_w
    o_q = o_dt + 2 * SSM_HEADS
    w_z, w_x, w_bc = w_in[:, :inner], w_in[:, inner:2 * inner], w_in[:, 2 * inner:o_dt]
    w_dt = w_in[:, o_dt:o_q]
    w_qkv = w_in[:, o_q:]
    w_in_b = jnp.concatenate([w_qkv, w_z, w_x, w_bc], axis=1).astype(BF16)
    w_dt_b = jnp.pad(w_dt, ((0, 0), (0, LANES - 2 * SSM_HEADS))).astype(BF16)
    conv_w, conv_b = ssm_conv_w[0], ssm_conv_b[0]
    pad32 = lambda v: jnp.pad(v.reshape(1, -1), ((0, 0), (0, LANES - 2 * SSM_HEADS)))
    dtb_row = pad32(ssm_dt_bias[0])
    a_row = pad32(-jnp.exp(ssm_A_log[0].astype(F32)))
    d_skip = jnp.repeat(ssm_D[0], SSM_HEAD_DIM).reshape(1, inner)
    w_out = ab_w_out[0].astype(BF16)
    w_gqa = gqa_w_in[0].astype(BF16)
    w_gqa_out = gqa_w_out[0].astype(BF16)
    qg = jnp.tile(gqa_q_norm[0], LANES // HEAD_DIM).reshape(1, LANES)
    kg = jnp.tile(gqa_k_norm[0], LANES // HEAD_DIM).reshape(1, LANES)
    rw_t = router_w.T
    rb = router_bias.reshape(N_EXPERTS, 1)
    wg, wu, wd = moe_w_gate.astype(BF16), moe_w_up.astype(BF16), moe_w_down.astype(BF16)

    cc = jnp.concatenate([c, c_ctx[None, :], jnp.zeros((8 - bsz - 1, d), F32)], axis=0)
    mod = modulation(cc, w_mod, b_mod)[:, :bsz + 1].reshape(2, bsz + 1, 6, d)

    h0 = jnp.concatenate([ctx, x], axis=1).reshape(bsz * lb, d)

    p = norm_matmul(h0, norm1_g[0], mod[0], w_in_b, shift_idx=0, scale_idx=1, mod_index=mod_comb, out_dtype=BF16)
    dtr = norm_matmul(h0, norm1_g[0], mod[0], w_dt_b, shift_idx=0, scale_idx=1, mod_index=mod_comb, out_dtype=F32)
    chunks = lb // SSD_CHUNK
    ctx_chunks = lc // SSD_CHUNK
    xc = dwconv_silu(p, 4, inner, conv_w[:, :inner], conv_b[:inner], chunks_per_batch=chunks, ctx_chunks=ctx_chunks)
    bcc = dwconv_silu(p, 10, xbc_w - inner, conv_w[:, inner:], conv_b[inner:],
                      chunks_per_batch=chunks, ctx_chunks=ctx_chunks)
    scan = functools.partial(ssd_scan, xc, bcc, dtr, dtb_row, a_row, batch=bsz,
                             chunks_per_batch=chunks, ctx_chunks=ctx_chunks)
    y_f = scan(reverse=False)
    y_ssd = scan(reverse=True, z_src=p, z_col=3, y_prev=y_f, d_skip=d_skip, norm_g=ssm_norm_g[0].reshape(1, inner))
    bias = na_bias_tables(na_rpb[0])
    o_na = na_attention(p, bias, batch=bsz, rows_per_batch=lat // GRID_W, ctx_len=lc, q_col=0, k_col=1, v_col=2)
    ident = lambda i: i
    h1 = resid_matmul(h0, mod[0], [y_ssd.astype(BF16), o_na], [w_out[:inner], w_out[inner:]],
                      gate_idx=2, n_blocks=bsz * nb, h_block=ident, mod_index=mod_comb)
    h2 = moe_layer(h1, norm2_g[0], mod[0], rw_t, rb, wg[0], wu[0], wd[0], final_g, mod_index=mod_comb, final=False)

    p1 = norm_matmul(h2, norm1_g[1], mod[1], w_gqa, shift_idx=0, scale_idx=1, mod_index=mod_comb, out_dtype=BF16)
    cos_f, sin_s = _rope_tables(lat)
    qt, kn, vt = gqa_prep(p1, cos_f, sin_s, qg, kg, batch=bsz, blocks_per_batch=nb, ctx_blocks=cb)
    o = gqa_attention(qt, kn.reshape(bsz, lb, GQA_KV_HEADS * HEAD_DIM), vt)
    lat_block = lambda i: (i // nlb) * nb + cb + i % nlb
    h3 = resid_matmul(h2, mod[1], [o], [w_gqa_out], gate_idx=2, n_blocks=bsz * nlb, h_block=lat_block, mod_index=mod_lat)
    out = moe_layer(h3, norm2_g[1], mod[1], rw_t, rb, wg[1], wu[1], wd[1], final_g, mod_index=mod_lat, final=True)
    return out.reshape(bsz, lat, d)
```

```python
import functools
import math

import jax
import jax.numpy as jnp
from jax import lax
from jax.experimental import pallas as pl
from jax.experimental.pallas import tpu as pltpu

F32 = jnp.float32
BF16 = jnp.bfloat16
I32 = jnp.int32
HIGHEST = lax.Precision.HIGHEST

NORM_EPS = 1e-6
GRID_W = 64
SSM_HEADS = 16
SSM_HEAD_DIM = 64
SSM_INNER = 1024
SSM_STATE = 128
SSM_GROUPS = 2
SSM_CONV_W = 5
SSD_CHUNK = 128
NA_HEADS = 16
NA_ROWS = 8
NA_COLS = 16
HEAD_DIM = 64
GQA_HEADS = 16
GQA_KV_HEADS = 4
ROPE_THETA = 10000.0
N_EXPERTS = 32
N_GROUPS = 8
EXPERTS_PER_GROUP = 4
D_EXPERT = 512

LANES = 128
HALO_ROWS = 16
TM = 256
MOE_BM = 256
DMA_ROWS = TM
NEG = -1e30
VMEM_LIMIT = 48 * 1024 * 1024


def _cparams(n_axes, vmem=None):
    return pltpu.CompilerParams(dimension_semantics=("arbitrary",) * n_axes,
                                vmem_limit_bytes=vmem)


def _silu(x):
    return x * jax.nn.sigmoid(x)


def _norm_mod(x, g, shift, scale):
    ms = jnp.mean(x * x, axis=-1, keepdims=True)
    y = x * lax.rsqrt(ms + NORM_EPS)
    return (y * g) * (1.0 + scale) + shift


def _mod_kernel(c_ref, w_ref, b_ref, o_ref):
    a = _silu(c_ref[...]).astype(BF16)
    o_ref[0] = jnp.dot(a, w_ref[0].astype(BF16), preferred_element_type=F32) + b_ref[0]


def modulation(cc, w_mod, b_mod):
    depth, d, n = w_mod.shape
    tn = 1536
    return pl.pallas_call(
        _mod_kernel,
        out_shape=jax.ShapeDtypeStruct((depth, 8, n), F32),
        grid=(depth, n // tn),
        in_specs=[pl.BlockSpec((8, d), lambda l, j: (0, 0)),
                  pl.BlockSpec((1, d, tn), lambda l, j: (l, 0, j)),
                  pl.BlockSpec((1, 1, tn), lambda l, j: (l, 0, j))],
        out_specs=pl.BlockSpec((1, 8, tn), lambda l, j: (l, 0, j)),
        compiler_params=_cparams(2, VMEM_LIMIT),
        name="modulation",
    )(cc, w_mod, b_mod.reshape(depth, 1, n))


def _norm_matmul_kernel(h_ref, g_ref, mod_ref, w_ref, o_ref, *, shift_idx, scale_idx):
    xn = _norm_mod(h_ref[...], g_ref[...],
                   mod_ref[0, shift_idx:shift_idx + 1, :], mod_ref[0, scale_idx:scale_idx + 1, :])
    o_ref[...] = jnp.dot(xn.astype(BF16), w_ref[...], preferred_element_type=F32).astype(o_ref.dtype)


def norm_matmul(h, gain, modtab, w, *, shift_idx, scale_idx, mod_index, out_dtype):
    t, d = h.shape
    n = w.shape[1]
    kern = functools.partial(_norm_matmul_kernel, shift_idx=shift_idx, scale_idx=scale_idx)
    return pl.pallas_call(
        kern,
        out_shape=jax.ShapeDtypeStruct((t, n), out_dtype),
        grid=(t // TM,),
        in_specs=[pl.BlockSpec((TM, d), lambda i: (i, 0)),
                  pl.BlockSpec((1, d), lambda i: (0, 0)),
                  pl.BlockSpec((1, 6, d), lambda i: (mod_index(i), 0, 0)),
                  pl.BlockSpec((d, n), lambda i: (0, 0))],
        out_specs=pl.BlockSpec((TM, n), lambda i: (i, 0)),
        compiler_params=_cparams(1, VMEM_LIMIT),
        name="norm_matmul",
    )(h, gain.reshape(1, d), modtab, w)


def _resid_matmul_kernel(*refs, n_pairs, gate_idx):
    h_ref, mod_ref = refs[0], refs[1]
    a_refs = refs[2:2 + n_pairs]
    w_refs = refs[2 + n_pairs:2 + 2 * n_pairs]
    o_ref = refs[2 + 2 * n_pairs]
    acc = jnp.dot(a_refs[0][...], w_refs[0][...], preferred_element_type=F32)
    for a_ref, w_ref in zip(a_refs[1:], w_refs[1:]):
        acc = acc + jnp.dot(a_ref[...], w_ref[...], preferred_element_type=F32)
    o_ref[...] = h_ref[...] + mod_ref[0, gate_idx:gate_idx + 1, :] * acc


def resid_matmul(h, modtab, a_list, w_list, *, gate_idx, n_blocks, h_block, mod_index):
    d = h.shape[1]
    n_pairs = len(a_list)
    kern = functools.partial(_resid_matmul_kernel, n_pairs=n_pairs, gate_idx=gate_idx)
    in_specs = [pl.BlockSpec((TM, d), lambda i: (h_block(i), 0)),
                pl.BlockSpec((1, 6, d), lambda i: (mod_index(i), 0, 0))]
    in_specs += [pl.BlockSpec((TM, a.shape[1]), lambda i: (i, 0)) for a in a_list]
    in_specs += [pl.BlockSpec(w.shape, lambda i: (0, 0)) for w in w_list]
    return pl.pallas_call(
        kern,
        out_shape=jax.ShapeDtypeStruct((n_blocks * TM, d), F32),
        grid=(n_blocks,),
        in_specs=in_specs,
        out_specs=pl.BlockSpec((TM, d), lambda i: (i, 0)),
        compiler_params=_cparams(1, VMEM_LIMIT),
        name="resid_matmul",
    )(h, modtab, *a_list, *w_list)


def _dwconv_kernel(cur_ref, prev_ref, next_ref, w_ref, b_ref, o_ref, ext_ref, *, chunks_per_batch, ctx_chunks):
    c = pl.program_id(0) % chunks_per_batch
    is_start = jnp.logical_or(c == 0, c == ctx_chunks)
    is_end = jnp.logical_or(c == ctx_chunks - 1, c == chunks_per_batch - 1)
    half = SSM_CONV_W // 2
    hr = HALO_ROWS
    ext_ref[0:hr, :] = jnp.where(is_start, 0.0, prev_ref[...].astype(F32))
    ext_ref[hr:hr + SSD_CHUNK, :] = cur_ref[...].astype(F32)
    ext_ref[hr + SSD_CHUNK:2 * hr + SSD_CHUNK, :] = jnp.where(is_end, 0.0, next_ref[...].astype(F32))
    acc = b_ref[...] + w_ref[0:1, :] * ext_ref[hr - half:hr - half + SSD_CHUNK, :]
    for tap in range(1, SSM_CONV_W):
        lo = hr - half + tap
        acc = acc + w_ref[tap:tap + 1, :] * ext_ref[lo:lo + SSD_CHUNK, :]
    o_ref[...] = _silu(acc).astype(o_ref.dtype)


def dwconv_silu(p, col_block, width, w, b, *, chunks_per_batch, ctx_chunks):
    t = p.shape[0]
    n_chunks = t // SSD_CHUNK
    sub = SSD_CHUNK // HALO_ROWS
    kern = functools.partial(_dwconv_kernel, chunks_per_batch=chunks_per_batch, ctx_chunks=ctx_chunks)
    return pl.pallas_call(
        kern,
        out_shape=jax.ShapeDtypeStruct((t, width), BF16),
        grid=(n_chunks,),
        in_specs=[pl.BlockSpec((SSD_CHUNK, width), lambda i: (i, col_block)),
                  pl.BlockSpec((HALO_ROWS, width), lambda i: (jnp.maximum(i * sub - 1, 0), col_block)),
                  pl.BlockSpec((HALO_ROWS, width),
                               lambda i: (jnp.minimum((i + 1) * sub, t // HALO_ROWS - 1), col_block)),
                  pl.BlockSpec((SSM_CONV_W, width), lambda i: (0, 0)),
                  pl.BlockSpec((1, width), lambda i: (0, 0))],
        out_specs=pl.BlockSpec((SSD_CHUNK, width), lambda i: (i, 0)),
        scratch_shapes=[pltpu.VMEM((SSD_CHUNK + 2 * HALO_ROWS, width), F32)],
        compiler_params=_cparams(1),
        name="dwconv_silu",
    )(p, p, p, w, b.reshape(1, width))


def _ssd_kernel(*refs, reverse, final):
    if final:
        (x_ref, bc_ref, dt_ref, dtb_ref, a_ref, z_ref, yprev_ref, dskip_ref, ng_ref, y_ref, state_ref) = refs
    else:
        (x_ref, bc_ref, dt_ref, dtb_ref, a_ref, y_ref, state_ref) = refs
    ch = SSD_CHUNK
    n = SSM_STATE
    inner = SSM_INNER
    gw = inner // SSM_GROUPS
    col0 = SSM_HEADS if reverse else 0

    @pl.when(pl.program_id(1) == 0)
    def _():
        state_ref[...] = jnp.zeros_like(state_ref)

    x = x_ref[...].astype(F32)
    bc = bc_ref[...]
    raw = dt_ref[...] + dtb_ref[...]
    dt = jnp.maximum(raw, 0.0) + jnp.log(1.0 + jnp.exp(-jnp.abs(raw)))
    adt = dt * a_ref[...]

    ri = lax.broadcasted_iota(I32, (ch, ch), 0)
    ci = lax.broadcasted_iota(I32, (ch, ch), 1)
    if reverse:
        tri = ci >= ri
        tri_t = ri >= ci
    else:
        tri = ci <= ri
        tri_t = ri <= ci
    acum = jnp.dot(tri.astype(F32), adt, precision=HIGHEST, preferred_element_type=F32)
    acum_t = jnp.dot(adt.T, tri_t.astype(F32), precision=HIGHEST, preferred_element_type=F32)
    last = 0 if reverse else ch - 1
    total = acum[last:last + 1, :]

    hrow = lax.broadcasted_iota(I32, (LANES, inner), 0)
    hcol = lax.broadcasted_iota(I32, (LANES, inner), 1) // SSM_HEAD_DIM
    sel = (hrow == hcol + col0).astype(F32)

    def expand(v):
        return jnp.dot(v, sel, precision=HIGHEST, preferred_element_type=F32)

    dt_x = expand(dt)
    ea_x = expand(jnp.exp(acum))
    ds_x = expand(jnp.exp(total - acum))
    tot_x = expand(jnp.broadcast_to(jnp.exp(total), (8, LANES)))[0:1, :]

    xd = x * dt_x
    xw = (xd * ds_x).astype(BF16)
    xd_b = xd.astype(BF16)
    state = state_ref[...]
    lane = lax.broadcasted_iota(I32, (ch, LANES), 1)
    first_half = lane < SSM_HEAD_DIM
    y_parts = []

    for g in range(SSM_GROUPS):
        b_g = bc[:, g * n:(g + 1) * n]
        c_g = bc[:, (SSM_GROUPS + g) * n:(SSM_GROUPS + g + 1) * n]
        cb = lax.dot_general(c_g, b_g, (((1,), (1,)), ((), ())), preferred_element_type=F32)
        gs = slice(g * gw, (g + 1) * gw)
        y_off = jnp.dot(c_g, state[:, gs].astype(BF16), preferred_element_type=F32) * ea_x[:, gs]
        b_t = b_g.astype(F32).T.astype(BF16)
        new_state = tot_x[:, gs] * state[:, gs] + jnp.dot(b_t, xw[:, gs], preferred_element_type=F32)
        state_ref[:, gs] = new_state
        for pair in range(gw // LANES):
            lanes = slice(g * gw + pair * LANES, g * gw + (pair + 1) * LANES)
            halves = []
            for sub in range(2):
                head = (g * gw + pair * LANES) // SSM_HEAD_DIM + sub
                cidx = col0 + head
                diff = acum[:, cidx:cidx + 1] - acum_t[cidx:cidx + 1, :]
                m_h = (cb * jnp.where(tri, jnp.exp(jnp.where(tri, diff, 0.0)), 0.0)).astype(BF16)
                halves.append(jnp.dot(m_h, xd_b[:, lanes], preferred_element_type=F32))
            y_parts.append(jnp.where(first_half, halves[0], halves[1]) + y_off[:, pair * LANES:(pair + 1) * LANES])

    y = jnp.concatenate(y_parts, axis=1)
    if final:
        y = y + yprev_ref[...] + x * dskip_ref[...]
        y = y * _silu(z_ref[...].astype(F32))
        ms = jnp.mean(y * y, axis=-1, keepdims=True)
        y = (y * lax.rsqrt(ms + NORM_EPS)) * ng_ref[...]
    y_ref[...] = y


def ssd_scan(xc, bcc, dtr, dtb, a_row, *, batch, chunks_per_batch, ctx_chunks, reverse,
             z_src=None, z_col=None, y_prev=None, d_skip=None, norm_g=None):
    t, inner = xc.shape
    final = y_prev is not None
    npb = chunks_per_batch

    def chunk(b, s):
        if not reverse:
            c = s
        else:
            c = jnp.where(s < ctx_chunks, ctx_chunks - 1 - s, npb - 1 + ctx_chunks - s)
        return b * npb + c

    row = lambda b, s: (chunk(b, s), 0)
    const = lambda b, s: (0, 0)
    in_specs = [pl.BlockSpec((SSD_CHUNK, inner), row),
                pl.BlockSpec((SSD_CHUNK, bcc.shape[1]), row),
                pl.BlockSpec((SSD_CHUNK, LANES), row),
                pl.BlockSpec((1, LANES), const),
                pl.BlockSpec((1, LANES), const)]
    args = [xc, bcc, dtr, dtb, a_row]
    if final:
        in_specs += [pl.BlockSpec((SSD_CHUNK, inner), lambda b, s: (chunk(b, s), z_col)),
                     pl.BlockSpec((SSD_CHUNK, inner), row),
                     pl.BlockSpec((1, inner), const),
                     pl.BlockSpec((1, inner), const)]
        args += [z_src, y_prev, d_skip, norm_g]
    kern = functools.partial(_ssd_kernel, reverse=reverse, final=final)
    return pl.pallas_call(
        kern,
        out_shape=jax.ShapeDtypeStruct((t, inner), F32),
        grid=(batch, npb),
        in_specs=in_specs,
        out_specs=pl.BlockSpec((SSD_CHUNK, inner), row),
        scratch_shapes=[pltpu.VMEM((SSM_STATE, inner), F32)],
        compiler_params=_cparams(2, VMEM_LIMIT),
        name="ssd_bwd" if reverse else "ssd_fwd",
    )(*args)


def _na_bias_kernel(rpb_ref, o_ref):
    cfg = pl.program_id(0)
    head = pl.program_id(1)
    w = GRID_W
    qi = lax.broadcasted_iota(I32, (w, w), 0)
    ki = lax.broadcasted_iota(I32, (w, w), 1)
    cstart = jnp.clip(qi - NA_COLS // 2, 0, w - NA_COLS)
    col_ok = jnp.logical_and(ki >= cstart, ki < cstart + NA_COLS)
    col_idx = jnp.clip(ki - qi + NA_COLS - 1, 0, 2 * NA_COLS - 2)
    n_cfg = pl.num_programs(0)
    for j in range(NA_ROWS):
        rho = jnp.clip(j - cfg + NA_ROWS - 1, 0, 2 * NA_ROWS - 2)
        tile = jnp.zeros((w, w), F32)
        for tcol in range(2 * NA_COLS - 1):
            tile = jnp.where(col_idx == tcol, rpb_ref[head, rho, tcol], tile)
        tile = jnp.where(col_ok, tile, NEG)
        tile = jnp.where(cfg == n_cfg - 1, NEG, tile)
        o_ref[0, 0, :, j * w:(j + 1) * w] = tile


def na_bias_tables(rpb):
    heads = rpb.shape[0]
    n_cfg = NA_ROWS + 1
    return pl.pallas_call(
        _na_bias_kernel,
        out_shape=jax.ShapeDtypeStruct((n_cfg, heads, GRID_W, NA_ROWS * GRID_W), F32),
        grid=(n_cfg, heads),
        in_specs=[pl.BlockSpec(memory_space=pltpu.SMEM)],
        out_specs=pl.BlockSpec((1, 1, GRID_W, NA_ROWS * GRID_W), lambda c, h: (c, h, 0, 0)),
        compiler_params=_cparams(2),
        name="na_bias_tables",
    )(rpb)


def _na_kernel(*refs):
    q_ref = refs[0]
    k_refs = refs[1:1 + NA_ROWS]
    v_refs = refs[1 + NA_ROWS:1 + 2 * NA_ROWS]
    kc_ref, vc_ref, bias_ref, o_ref, s_ref, p_ref = refs[1 + 2 * NA_ROWS:]
    w = GRID_W
    n_lat = NA_ROWS * w
    scale = HEAD_DIM ** -0.5
    lane_q = lax.broadcasted_iota(I32, (w, LANES), 1)
    first_half = lane_q < HEAD_DIM
    nt = (((1,), (1,)), ((), ()))
    for pair in range(NA_HEADS // 2):
        lanes = slice(pair * LANES, (pair + 1) * LANES)
        q2 = q_ref[:, lanes] * scale
        kk = jnp.concatenate([r[:, lanes] for r in k_refs] + [kc_ref[:, lanes]], axis=0)
        for sub in range(2):
            keep = first_half if sub == 0 else jnp.logical_not(first_half)
            qz = jnp.where(keep, q2, jnp.zeros_like(q2))
            s_ref[2 * pair + sub] = lax.dot_general(qz, kk, nt, preferred_element_type=F32)
    inv_den = []
    for head in range(NA_HEADS):
        s_lat = s_ref[head, :, 0:n_lat] + bias_ref[0, head]
        s_ctx = s_ref[head, :, n_lat:]
        m = jnp.maximum(jnp.max(s_lat, axis=-1, keepdims=True), jnp.max(s_ctx, axis=-1, keepdims=True))
        p_lat = jnp.exp(s_lat - m)
        p_ctx = jnp.exp(s_ctx - m)
        inv_den.append(1.0 / (jnp.sum(p_lat, axis=-1, keepdims=True) + jnp.sum(p_ctx, axis=-1, keepdims=True)))
        p_ref[head, :, 0:n_lat] = p_lat.astype(BF16)
        p_ref[head, :, n_lat:] = p_ctx.astype(BF16)
    for pair in range(NA_HEADS // 2):
        lanes = slice(pair * LANES, (pair + 1) * LANES)
        vv = jnp.concatenate([r[:, lanes] for r in v_refs] + [vc_ref[:, lanes]], axis=0)
        o0 = jnp.dot(p_ref[2 * pair], vv, preferred_element_type=F32) * inv_den[2 * pair]
        o1 = jnp.dot(p_ref[2 * pair + 1], vv, preferred_element_type=F32) * inv_den[2 * pair + 1]
        o_ref[:, lanes] = jnp.where(first_half, o0, o1).astype(o_ref.dtype)


def na_attention(p, bias, *, batch, rows_per_batch, ctx_len, q_col, k_col, v_col):
    t = p.shape[0]
    w = GRID_W
    width = NA_HEADS * HEAD_DIM
    blocks_per_batch = rows_per_batch + ctx_len // w
    ctx_blocks = ctx_len // w
    lat_rows = rows_per_batch

    def win_start(st):
        r = st - ctx_blocks
        return jnp.clip(r - NA_ROWS // 2, 0, lat_rows - NA_ROWS)

    def cfg_of(st):
        r = st - ctx_blocks
        return jnp.where(st < ctx_blocks, NA_ROWS, r - win_start(st))

    def kv_spec(j, col):
        return pl.BlockSpec((w, width), lambda b, st: (b * blocks_per_batch + ctx_blocks + win_start(st) + j, col))

    in_specs = [pl.BlockSpec((w, width), lambda b, st: (b * blocks_per_batch + st, q_col))]
    in_specs += [kv_spec(j, k_col) for j in range(NA_ROWS)]
    in_specs += [kv_spec(j, v_col) for j in range(NA_ROWS)]
    blocks_ctx = (rows_per_batch * w + ctx_len) // ctx_len
    in_specs += [pl.BlockSpec((ctx_len, width), lambda b, st: (b * blocks_ctx, k_col)),
                 pl.BlockSpec((ctx_len, width), lambda b, st: (b * blocks_ctx, v_col)),
                 pl.BlockSpec((1, NA_HEADS, w, NA_ROWS * w), lambda b, st: (cfg_of(st), 0, 0, 0))]
    args = [p] * (3 + 2 * NA_ROWS) + [bias]
    return pl.pallas_call(
        _na_kernel,
        out_shape=jax.ShapeDtypeStruct((t, width), BF16),
        grid=(batch, blocks_per_batch),
        in_specs=in_specs,
        out_specs=pl.BlockSpec((w, width), lambda b, st: (b * blocks_per_batch + st, 0)),
        scratch_shapes=[pltpu.VMEM((NA_HEADS, w, NA_ROWS * w + ctx_len), F32),
                        pltpu.VMEM((NA_HEADS, w, NA_ROWS * w + ctx_len), BF16)],
        compiler_params=_cparams(2, VMEM_LIMIT),
        name="na_attention",
    )(*args)


def _group_mean_sq(x):
    gi = lax.broadcasted_iota(I32, (LANES, LANES), 0) // HEAD_DIM
    gj = lax.broadcasted_iota(I32, (LANES, LANES), 1) // HEAD_DIM
    ones = (gi == gj).astype(F32)
    return jnp.dot(x * x, ones, precision=HIGHEST, preferred_element_type=F32) * (1.0 / HEAD_DIM)


def _rope(x, cos_f, sin_s):
    lane = lax.broadcasted_iota(I32, x.shape, 1)
    low = (lane % HEAD_DIM) < HEAD_DIM // 2
    partner = jnp.where(low, pltpu.roll(x, LANES - HEAD_DIM // 2, 1), pltpu.roll(x, HEAD_DIM // 2, 1))
    return x * cos_f + partner * sin_s


def _gqa_prep_kernel(p_ref, cos_ref, sin_ref, qg_ref, kg_ref, qt_ref, k_ref, vt_ref, *, ctx_blocks):
    j = pl.program_id(1)
    is_lat = j >= ctx_blocks
    q_w = GQA_HEADS * HEAD_DIM
    kv_w = GQA_KV_HEADS * HEAD_DIM
    cos_f = jnp.where(is_lat, cos_ref[...], 1.0)
    sin_s = jnp.where(is_lat, sin_ref[...], 0.0)

    def normed(cols, gain):
        x = p_ref[:, cols].astype(F32)
        y = x * lax.rsqrt(_group_mean_sq(x) + NORM_EPS)
        return _rope(y * gain, cos_f, sin_s)

    for tile in range(kv_w // LANES):
        cols = slice(q_w + tile * LANES, q_w + (tile + 1) * LANES)
        k_ref[:, tile * LANES:(tile + 1) * LANES] = normed(cols, kg_ref[...]).astype(k_ref.dtype)
    v = p_ref[:, q_w + kv_w:q_w + 2 * kv_w].astype(F32)
    vt_ref[0] = v.T.astype(vt_ref.dtype)

    @pl.when(is_lat)
    def _():
        scale = HEAD_DIM ** -0.5 * math.log2(math.e)
        for tile in range(q_w // LANES):
            cols = slice(tile * LANES, (tile + 1) * LANES)
            q = normed(cols, qg_ref[...]) * scale
            qt_ref[0, cols, :] = q.T.astype(qt_ref.dtype)


def gqa_prep(p1, cos_f, sin_s, qg, kg, *, batch, blocks_per_batch, ctx_blocks):
    t, n = p1.shape
    q_w = GQA_HEADS * HEAD_DIM
    kv_w = GQA_KV_HEADS * HEAD_DIM
    lb = blocks_per_batch * TM
    lat = (blocks_per_batch - ctx_blocks) * TM
    kern = functools.partial(_gqa_prep_kernel, ctx_blocks=ctx_blocks)
    lat_blk = lambda b, j: jnp.maximum(j - ctx_blocks, 0)
    return pl.pallas_call(
        kern,
        out_shape=(jax.ShapeDtypeStruct((batch, q_w, lat), BF16),
                   jax.ShapeDtypeStruct((t, kv_w), BF16),
                   jax.ShapeDtypeStruct((batch, kv_w, lb), BF16)),
        grid=(batch, blocks_per_batch),
        in_specs=[pl.BlockSpec((TM, n), lambda b, j: (b * blocks_per_batch + j, 0)),
                  pl.BlockSpec((TM, LANES), lambda b, j: (lat_blk(b, j), 0)),
                  pl.BlockSpec((TM, LANES), lambda b, j: (lat_blk(b, j), 0)),
                  pl.BlockSpec((1, LANES), lambda b, j: (0, 0)),
                  pl.BlockSpec((1, LANES), lambda b, j: (0, 0))],
        out_specs=(pl.BlockSpec((1, q_w, TM), lambda b, j: (b, 0, lat_blk(b, j))),
                   pl.BlockSpec((TM, kv_w), lambda b, j: (b * blocks_per_batch + j, 0)),
                   pl.BlockSpec((1, kv_w, TM), lambda b, j: (b, 0, j))),
        compiler_params=_cparams(2, VMEM_LIMIT),
        name="gqa_prep",
    )(p1, cos_f, sin_s, qg, kg)


def _gqa_attn_kernel(qt_ref, k_ref, vt_ref, o_ref, qz_ref, s_ref, m_ref, l_ref, acc_ref, *, tk, n_kblocks):
    kv = pl.program_id(1)
    tq = qt_ref.shape[2]
    group = GQA_HEADS // GQA_KV_HEADS
    second = (kv % 2) == 1
    for hq in range(group):
        q_h = qt_ref[0, hq * HEAD_DIM:(hq + 1) * HEAD_DIM, :]
        zero = jnp.zeros_like(q_h)
        qz_ref[0:HEAD_DIM, hq * tq:(hq + 1) * tq] = jnp.where(second, zero, q_h)
        qz_ref[HEAD_DIM:2 * HEAD_DIM, hq * tq:(hq + 1) * tq] = jnp.where(second, q_h, zero)
    m_ref[...] = jnp.full_like(m_ref, NEG)
    l_ref[...] = jnp.zeros_like(l_ref)
    acc_ref[...] = jnp.zeros_like(acc_ref)

    def scores(kb, slot):
        off = pl.multiple_of(kb * tk, tk)
        s_ref[slot] = jnp.dot(k_ref[0, pl.ds(off, tk), :], qz_ref[...], preferred_element_type=F32)

    def update(kb, slot):
        off = pl.multiple_of(kb * tk, tk)
        s = s_ref[slot]
        m_old = m_ref[...]
        m_new = jnp.maximum(m_old, jnp.max(s, axis=0, keepdims=True))
        alpha = jnp.exp2(m_old - m_new)
        p = jnp.exp2(s - m_new)
        l_ref[...] = alpha * l_ref[...] + jnp.sum(p, axis=0, keepdims=True)
        pv = jnp.dot(vt_ref[0, :, pl.ds(off, tk)], p.astype(BF16), preferred_element_type=F32)
        acc_ref[...] = alpha * acc_ref[...] + pv
        m_ref[...] = m_new

    scores(0, 0)
    trips = (n_kblocks - 1) // 2

    def two_blocks(i, carry):
        kb = 2 * i
        scores(kb + 1, 1)
        update(kb, 0)
        scores(kb + 2, 0)
        update(kb + 1, 1)
        return carry

    lax.fori_loop(0, trips, two_blocks, 0)
    if n_kblocks % 2 == 1:
        update(n_kblocks - 1, 0)
    else:
        scores(n_kblocks - 1, 1)
        update(n_kblocks - 2, 0)
        update(n_kblocks - 1, 1)

    inv_l = 1.0 / l_ref[...]
    for pair in range(group // 2):
        lo, hi = 2 * pair * tq, (2 * pair + 1) * tq
        o_t = jnp.concatenate([acc_ref[:, lo:lo + tq] * inv_l[:, lo:lo + tq],
                               acc_ref[:, hi:hi + tq] * inv_l[:, hi:hi + tq]], axis=0)
        o_ref[:, pair * LANES:(pair + 1) * LANES] = o_t.T.astype(o_ref.dtype)


def gqa_attention(qt, k, vt, *, tq=512, tk=256):
    batch, q_w, lat = qt.shape
    lb = k.shape[1]
    group = GQA_HEADS // GQA_KV_HEADS
    gw = group * HEAD_DIM
    tq = min(tq, lat)
    kern = functools.partial(_gqa_attn_kernel, tk=tk, n_kblocks=lb // tk)
    return pl.pallas_call(
        kern,
        out_shape=jax.ShapeDtypeStruct((batch * lat, q_w), BF16),
        grid=(batch, GQA_KV_HEADS, lat // tq),
        in_specs=[pl.BlockSpec((1, gw, tq), lambda b, h, i: (b, h, i)),
                  pl.BlockSpec((1, lb, LANES), lambda b, h, i: (b, 0, h // 2)),
                  pl.BlockSpec((1, HEAD_DIM, lb), lambda b, h, i: (b, h, 0))],
        out_specs=pl.BlockSpec((tq, gw), lambda b, h, i: (b * (lat // tq) + i, h)),
        scratch_shapes=[pltpu.VMEM((2 * HEAD_DIM, group * tq), BF16),
                        pltpu.VMEM((2, tk, group * tq), F32),
                        pltpu.VMEM((1, group * tq), F32),
                        pltpu.VMEM((1, group * tq), F32),
                        pltpu.VMEM((HEAD_DIM, group * tq), F32)],
        compiler_params=_cparams(3, VMEM_LIMIT),
        name="gqa_attention",
    )(qt, k, vt)


def _router_kernel(h_ref, g_ref, mod_ref, rw_ref, rb_ref, xn_ref, eidx_ref, wts_ref, rank_ref, cnt_ref, run_ref):
    @pl.when(pl.program_id(0) == 0)
    def _():
        run_ref[...] = jnp.zeros_like(run_ref)

    xn = _norm_mod(h_ref[...], g_ref[...], mod_ref[0, 3:4, :], mod_ref[0, 4:5, :])
    xn_ref[...] = xn
    tm = xn.shape[0]
    logits = lax.dot_general(rw_ref[...], xn, (((1,), (1,)), ((), ())),
                             precision=HIGHEST, preferred_element_type=F32)
    s = jax.nn.sigmoid(logits)
    sb = s + rb_ref[...]
    row = lambda a, e: a[e:e + 1, :]

    best = None
    for g in range(N_GROUPS):
        a = [row(sb, g * EXPERTS_PER_GROUP + j) for j in range(EXPERTS_PER_GROUP)]
        score = a[0] + a[1]
        for i in range(EXPERTS_PER_GROUP):
            for j in range(i + 1, EXPERTS_PER_GROUP):
                if (i, j) != (0, 1):
                    score = jnp.maximum(score, a[i] + a[j])
        if best is None:
            best, grp = score, jnp.zeros((1, tm), I32)
        else:
            better = score > best
            best = jnp.where(better, score, best)
            grp = jnp.where(better, g, grp)

    biased, plain = [], []
    for j in range(EXPERTS_PER_GROUP):
        vb = row(sb, j)
        vp = row(s, j)
        for g in range(1, N_GROUPS):
            pick = grp == g
            vb = jnp.where(pick, row(sb, g * EXPERTS_PER_GROUP + j), vb)
            vp = jnp.where(pick, row(s, g * EXPERTS_PER_GROUP + j), vp)
        biased.append(vb)
        plain.append(vp)

    def argmax_first(vals):
        top, idx = vals[0], jnp.zeros((1, tm), I32)
        for j in range(1, len(vals)):
            better = vals[j] > top
            top = jnp.where(better, vals[j], top)
            idx = jnp.where(better, j, idx)
        return idx

    i1 = argmax_first(biased)
    i2 = argmax_first([jnp.where(i1 == j, -jnp.inf, biased[j]) for j in range(EXPERTS_PER_GROUP)])

    def pick_plain(idx):
        out = plain[0]
        for j in range(1, EXPERTS_PER_GROUP):
            out = jnp.where(idx == j, plain[j], out)
        return out

    w1, w2 = pick_plain(i1), pick_plain(i2)
    wsum = w1 + w2
    e1 = grp * EXPERTS_PER_GROUP + i1
    e2 = grp * EXPERTS_PER_GROUP + i2
    eidx_ref[0] = jnp.concatenate([e1, e2], axis=0)
    wts_ref[...] = jnp.concatenate([w1 / wsum, w2 / wsum], axis=0)

    erow = lax.broadcasted_iota(I32, (N_EXPERTS, tm), 0)
    hit1 = erow == e1
    hit2 = erow == e2
    onehot = jnp.logical_or(hit1, hit2)
    si = lax.broadcasted_iota(I32, (tm, tm), 0)
    ti = lax.broadcasted_iota(I32, (tm, tm), 1)
    before = jnp.dot(onehot.astype(BF16), (si < ti).astype(BF16), preferred_element_type=F32)
    before = before + run_ref[...][:, 0:1]
    r1 = jnp.sum(jnp.where(hit1, before, 0.0), axis=0, keepdims=True)
    r2 = jnp.sum(jnp.where(hit2, before, 0.0), axis=0, keepdims=True)
    rank_ref[0] = jnp.concatenate([r1, r2], axis=0).astype(I32)
    totals = run_ref[...] + jnp.sum(onehot.astype(F32), axis=1, keepdims=True)
    run_ref[...] = totals
    cnt_ref[...] = totals


def moe_router(h, gain, modtab, rw_t, rb, *, mod_index):
    t, d = h.shape
    return pl.pallas_call(
        _router_kernel,
        out_shape=(jax.ShapeDtypeStruct((t, d), F32),
                   jax.ShapeDtypeStruct((t // TM, 2, TM), I32),
                   jax.ShapeDtypeStruct((2, t), F32),
                   jax.ShapeDtypeStruct((t // TM, 2, TM), I32),
                   jax.ShapeDtypeStruct((N_EXPERTS, LANES), F32)),
        grid=(t // TM,),
        in_specs=[pl.BlockSpec((TM, d), lambda i: (i, 0)),
                  pl.BlockSpec((1, d), lambda i: (0, 0)),
                  pl.BlockSpec((1, 6, d), lambda i: (mod_index(i), 0, 0)),
                  pl.BlockSpec((N_EXPERTS, d), lambda i: (0, 0)),
                  pl.BlockSpec((N_EXPERTS, 1), lambda i: (0, 0))],
        out_specs=(pl.BlockSpec((TM, d), lambda i: (i, 0)),
                   pl.BlockSpec((1, 2, TM), lambda i: (i, 0, 0)),
                   pl.BlockSpec((2, TM), lambda i: (0, i)),
                   pl.BlockSpec((1, 2, TM), lambda i: (i, 0, 0)),
                   pl.BlockSpec((N_EXPERTS, LANES), lambda i: (0, 0))),
        scratch_shapes=[pltpu.VMEM((N_EXPERTS, LANES), F32)],
        compiler_params=_cparams(1, VMEM_LIMIT),
        name="moe_router",
    )(h, gain.reshape(1, d), modtab, rw_t, rb)


def _row_copy(src_ref, src_row, dst_ref, dst_row, sem):
    return pltpu.make_async_copy(src_ref.at[pl.ds(src_row, 1)], dst_ref.at[pl.ds(dst_row, 1)], sem)


def _issue_and_drain(rows, make_copy):
    def issue(r, carry):
        for k in range(2):
            make_copy(r, k).start()
        return carry

    lax.fori_loop(0, rows, issue, 0, unroll=8)
    for _ in range(rows):
        for k in range(2):
            make_copy(0, k).wait()


def _dispatch_kernel(eidx_ref, rank_ref, start_ref, x_ref, xs_in_ref, xs_ref, sem):
    del xs_in_ref

    def make_copy(r, k):
        slot = start_ref[eidx_ref[0, k, r]] + rank_ref[0, k, r]
        return _row_copy(x_ref, r, xs_ref, slot, sem)

    _issue_and_drain(x_ref.shape[0], make_copy)


def moe_dispatch(xn, eidx3, rank3, pad_start, n_slots):
    t, d = xn.shape
    xs0 = jnp.zeros((n_slots, d), xn.dtype)
    idx_spec = pl.BlockSpec((1, 2, DMA_ROWS), lambda i: (i, 0, 0), memory_space=pltpu.SMEM)
    return pl.pallas_call(
        _dispatch_kernel,
        out_shape=jax.ShapeDtypeStruct((n_slots, d), xn.dtype),
        grid=(t // DMA_ROWS,),
        in_specs=[idx_spec, idx_spec,
                  pl.BlockSpec(memory_space=pltpu.SMEM),
                  pl.BlockSpec((DMA_ROWS, d), lambda i: (i, 0)),
                  pl.BlockSpec(memory_space=pl.ANY)],
        out_specs=pl.BlockSpec(memory_space=pl.ANY),
        scratch_shapes=[pltpu.SemaphoreType.DMA(())],
        input_output_aliases={4: 0},
        compiler_params=_cparams(1),
        name="moe_dispatch",
    )(eidx3, rank3, pad_start, xn, xs0)


def _expert_kernel(blk_e_ref, nused_ref, x_ref, wg_ref, wu_ref, wd_ref, y_ref):
    del blk_e_ref

    @pl.when(pl.program_id(0) < nused_ref[0])
    def _():
        x = x_ref[...].astype(BF16)
        gate = jnp.dot(x, wg_ref[0], preferred_element_type=F32)
        up = jnp.dot(x, wu_ref[0], preferred_element_type=F32)
        hid = (_silu(gate) * up).astype(BF16)
        y_ref[...] = jnp.dot(hid, wd_ref[0], preferred_element_type=F32)

    @pl.when(pl.program_id(0) >= nused_ref[0])
    def _():
        y_ref[...] = jnp.zeros_like(y_ref)


def moe_experts(xs, blk_e, nused, wg, wu, wd):
    n_slots, d = xs.shape
    de = wg.shape[2]
    grid_spec = pltpu.PrefetchScalarGridSpec(
        num_scalar_prefetch=2,
        grid=(n_slots // MOE_BM,),
        in_specs=[pl.BlockSpec((MOE_BM, d), lambda i, be, nu: (i, 0)),
                  pl.BlockSpec((1, d, de), lambda i, be, nu: (be[i], 0, 0)),
                  pl.BlockSpec((1, d, de), lambda i, be, nu: (be[i], 0, 0)),
                  pl.BlockSpec((1, de, d), lambda i, be, nu: (be[i], 0, 0))],
        out_specs=pl.BlockSpec((MOE_BM, d), lambda i, be, nu: (i, 0)),
    )
    return pl.pallas_call(
        _expert_kernel,
        out_shape=jax.ShapeDtypeStruct((n_slots, d), F32),
        grid_spec=grid_spec,
        compiler_params=_cparams(1, VMEM_LIMIT),
        name="moe_experts",
    )(blk_e, nused, xs, wg, wu, wd)


def _combine_kernel(eidx_ref, rank_ref, start_ref, h_ref, mod_ref, w_ref, fg_ref, yb_ref, o_ref, buf_ref, sem,
                    *, final):
    def make_copy(r, k):
        slot = start_ref[eidx_ref[0, k, r]] + rank_ref[0, k, r]
        return _row_copy(yb_ref, slot, buf_ref.at[k], r, sem)

    _issue_and_drain(h_ref.shape[0], make_copy)
    f = w_ref[:, 0:1] * buf_ref[0] + w_ref[:, 1:2] * buf_ref[1]
    h = h_ref[...] + mod_ref[0, 5:6, :] * f
    if final:
        ms = jnp.mean(h * h, axis=-1, keepdims=True)
        h = (h * lax.rsqrt(ms + NORM_EPS)) * fg_ref[...]
    o_ref[...] = h


def moe_combine(h, modtab, wts_t, final_g, yb, eidx3, rank3, pad_start, *, mod_index, final):
    t, d = h.shape
    kern = functools.partial(_combine_kernel, final=final)
    idx_spec = pl.BlockSpec((1, 2, DMA_ROWS), lambda i: (i, 0, 0), memory_space=pltpu.SMEM)
    return pl.pallas_call(
        kern,
        out_shape=jax.ShapeDtypeStruct((t, d), F32),
        grid=(t // DMA_ROWS,),
        in_specs=[idx_spec, idx_spec,
                  pl.BlockSpec(memory_space=pltpu.SMEM),
                  pl.BlockSpec((DMA_ROWS, d), lambda i: (i, 0)),
                  pl.BlockSpec((1, 6, d), lambda i: (mod_index(i), 0, 0)),
                  pl.BlockSpec((DMA_ROWS, 2), lambda i: (i, 0)),
                  pl.BlockSpec((1, d), lambda i: (0, 0)),
                  pl.BlockSpec(memory_space=pl.ANY)],
        out_specs=pl.BlockSpec((DMA_ROWS, d), lambda i: (i, 0)),
        scratch_shapes=[pltpu.VMEM((2, DMA_ROWS, d), F32), pltpu.SemaphoreType.DMA(())],
        compiler_params=_cparams(1, VMEM_LIMIT),
        name="moe_combine",
    )(eidx3, rank3, pad_start, h, modtab, wts_t, final_g.reshape(1, d), yb)


def moe_layer(h, norm_g, modtab, rw_t, rb, wg, wu, wd, final_g, *, mod_index, final):
    t, d = h.shape
    xn, eidx3, wts, rank3, counts = moe_router(h, norm_g, modtab, rw_t, rb, mod_index=mod_index)
    n_blocks = (2 * t) // MOE_BM + N_EXPERTS
    n_slots = n_blocks * MOE_BM
    cnt = counts[:, 0].astype(I32)
    padded = (cnt + MOE_BM - 1) // MOE_BM * MOE_BM
    pad_end = jnp.cumsum(padded)
    pad_start = (pad_end - padded).astype(I32)
    blk_start = jnp.arange(n_blocks, dtype=I32) * MOE_BM
    blk_e = jnp.minimum(jnp.sum((pad_end[None, :] <= blk_start[:, None]).astype(I32), axis=1), N_EXPERTS - 1)
    nused = (pad_end[-1] // MOE_BM).reshape(1).astype(I32)
    xs = moe_dispatch(xn, eidx3, rank3, pad_start, n_slots)
    yb = moe_experts(xs, blk_e.astype(I32), nused, wg, wu, wd)
    return moe_combine(h, modtab, wts.T, final_g, yb, eidx3, rank3, pad_start, mod_index=mod_index, final=final)


def _rope_tables(lat):
    t = jnp.arange(lat)
    row = (t // GRID_W).astype(F32)
    col = (t % GRID_W).astype(F32)
    n_freq = HEAD_DIM // 4
    inv = ROPE_THETA ** (-jnp.arange(n_freq, dtype=F32) / n_freq)
    ang = jnp.concatenate([row[:, None] * inv, col[:, None] * inv], axis=-1)
    cos, sin = jnp.cos(ang), jnp.sin(ang)
    cos_f = jnp.tile(cos, (1, LANES // (HEAD_DIM // 2)))
    sin_s = jnp.tile(jnp.concatenate([-sin, sin], axis=-1), (1, LANES // HEAD_DIM))
    return cos_f, sin_s


def kernel(x, c, ctx, c_ctx, w_mod, b_mod, norm1_g, norm2_g, final_g, ab_w_in, ab_w_out, ssm_conv_w, ssm_conv_b, ssm_A_log, ssm_dt_bias, ssm_D, ssm_norm_g, na_rpb, gqa_w_in, gqa_w_out, gqa_q_norm, gqa_k_norm, router_w, router_bias, moe_w_gate, moe_w_up, moe_w_down):
    bsz, lat, d = x.shape
    lc = ctx.shape[1]
    lb = lc + lat
    assert lat % TM == 0 and lc % TM == 0 and lat % GRID_W == 0 and lat // GRID_W >= NA_ROWS
    assert bsz + 1 <= 8 and w_mod.shape[0] == 2
    nb = lb // TM
    cb = lc // TM
    nlb = lat // TM

    def mod_comb(i):
        return jnp.where(i % nb < cb, bsz, i // nb)

    def mod_lat(i):
        return i // nlb

    inner, xbc_w = SSM_INNER, SSM_INNER + 2 * SSM_GROUPS * SSM_STATE
    w_in = ab_w_in[0]
    o_dt = inner + xbc_w
    o_q = o_dt + 2 * SSM_HEADS
    w_z, w_x, w_bc = w_in[:, :inner], w_in[:, inner:2 * inner], w_in[:, 2 * inner:o_dt]
    w_dt = w_in[:, o_dt:o_q]
    w_qkv = w_in[:, o_q:]
    w_in_b = jnp.concatenate([w_qkv, w_z, w_x, w_bc], axis=1).astype(BF16)
    w_dt_b = jnp.pad(w_dt, ((0, 0), (0, LANES - 2 * SSM_HEADS))).astype(BF16)
    conv_w, conv_b = ssm_conv_w[0], ssm_conv_b[0]
    pad32 = lambda v: jnp.pad(v.reshape(1, -1), ((0, 0), (0, LANES - 2 * SSM_HEADS)))
    dtb_row = pad32(ssm_dt_bias[0])
    a_row = pad32(-jnp.exp(ssm_A_log[0].astype(F32)))
    d_skip = jnp.repeat(ssm_D[0], SSM_HEAD_DIM).reshape(1, inner)
    w_out = ab_w_out[0].astype(BF16)
    w_gqa = gqa_w_in[0].astype(BF16)
    w_gqa_out = gqa_w_out[0].astype(BF16)
    qg = jnp.tile(gqa_q_norm[0], LANES // HEAD_DIM).reshape(1, LANES)
    kg = jnp.tile(gqa_k_norm[0], LANES // HEAD_DIM).reshape(1, LANES)
    rw_t = router_w.T
    rb = router_bias.reshape(N_EXPERTS, 1)
    wg, wu, wd = moe_w_gate.astype(BF16), moe_w_up.astype(BF16), moe_w_down.astype(BF16)

    cc = jnp.concatenate([c, c_ctx[None, :], jnp.zeros((8 - bsz - 1, d), F32)], axis=0)
    mod = modulation(cc, w_mod, b_mod)[:, :bsz + 1].reshape(2, bsz + 1, 6, d)

    h0 = jnp.concatenate([ctx, x], axis=1).reshape(bsz * lb, d)

    p = norm_matmul(h0, norm1_g[0], mod[0], w_in_b, shift_idx=0, scale_idx=1, mod_index=mod_comb, out_dtype=BF16)
    dtr = norm_matmul(h0, norm1_g[0], mod[0], w_dt_b, shift_idx=0, scale_idx=1, mod_index=mod_comb, out_dtype=F32)
    chunks = lb // SSD_CHUNK
    ctx_chunks = lc // SSD_CHUNK
    xc = dwconv_silu(p, 4, inner, conv_w[:, :inner], conv_b[:inner], chunks_per_batch=chunks, ctx_chunks=ctx_chunks)
    bcc = dwconv_silu(p, 10, xbc_w - inner, conv_w[:, inner:], conv_b[inner:],
                      chunks_per_batch=chunks, ctx_chunks=ctx_chunks)
    scan = functools.partial(ssd_scan, xc, bcc, dtr, dtb_row, a_row, batch=bsz,
                             chunks_per_batch=chunks, ctx_chunks=ctx_chunks)
    y_f = scan(reverse=False)
    y_ssd = scan(reverse=True, z_src=p, z_col=3, y_prev=y_f, d_skip=d_skip, norm_g=ssm_norm_g[0].reshape(1, inner))
    bias = na_bias_tables(na_rpb[0])
    o_na = na_attention(p, bias, batch=bsz, rows_per_batch=lat // GRID_W, ctx_len=lc, q_col=0, k_col=1, v_col=2)
    ident = lambda i: i
    h1 = resid_matmul(h0, mod[0], [y_ssd.astype(BF16), o_na], [w_out[:inner], w_out[inner:]],
                      gate_idx=2, n_blocks=bsz * nb, h_block=ident, mod_index=mod_comb)
    h2 = moe_layer(h1, norm2_g[0], mod[0], rw_t, rb, wg[0], wu[0], wd[0], final_g, mod_index=mod_comb, final=False)

    p1 = norm_matmul(h2, norm1_g[1], mod[1], w_gqa, shift_idx=0, scale_idx=1, mod_index=mod_comb, out_dtype=BF16)
    cos_f, sin_s = _rope_tables(lat)
    qt, kn, vt = gqa_prep(p1, cos_f, sin_s, qg, kg, batch=bsz, blocks_per_batch=nb, ctx_blocks=cb)
    o = gqa_attention(qt, kn.reshape(bsz, lb, GQA_KV_HEADS * HEAD_DIM), vt)
    lat_block = lambda i: (i // nlb) * nb + cb + i % nlb
    h3 = resid_matmul(h2, mod[1], [o], [w_gqa_out], gate_idx=2, n_blocks=bsz * nlb, h_block=lat_block, mod_index=mod_lat)
    out = moe_layer(h3, norm2_g[1], mod[1], rw_t, rb, wg[1], wu[1], wd[1], final_g, mod_index=mod_lat, final=True)
    return out.reshape(bsz, lat, d)
```

```python
import functools
import math

import jax
import jax.numpy as jnp
from jax import lax
from jax.experimental import pallas as pl
from jax.experimental.pallas import tpu as pltpu

F32 = jnp.float32
BF16 = jnp.bfloat16
I32 = jnp.int32
HIGHEST = lax.Precision.HIGHEST

NORM_EPS = 1e-6
GRID_W = 64
SSM_HEADS = 16
SSM_HEAD_DIM = 64
SSM_INNER = 1024
SSM_STATE = 128
SSM_GROUPS = 2
SSM_CONV_W = 5
SSD_CHUNK = 128
NA_HEADS = 16
NA_ROWS = 8
NA_COLS = 16
HEAD_DIM = 64
GQA_HEADS = 16
GQA_KV_HEADS = 4
ROPE_THETA = 10000.0
N_EXPERTS = 32
N_GROUPS = 8
EXPERTS_PER_GROUP = 4
D_EXPERT = 512

LANES = 128
HALO_ROWS = 16
ONES_ROWS = 16
GQA_BLOCKS_PER_TRIP = 4
TM = 256
MOE_BM = 256
DMA_ROWS = TM
NEG = -1e30
VMEM_LIMIT = 48 * 1024 * 1024


def _cparams(n_axes, vmem=None):
    return pltpu.CompilerParams(dimension_semantics=("arbitrary",) * n_axes,
                                vmem_limit_bytes=vmem)


def _silu(x):
    return x * jax.nn.sigmoid(x)


def _split3(v):
    hi = v.astype(BF16)
    r1 = v - hi.astype(F32)
    mid = r1.astype(BF16)
    lo = (r1 - mid.astype(F32)).astype(BF16)
    return hi, mid, lo


def _dot_01_right(v, m01):
    lhs = jnp.concatenate(_split3(v), axis=1)
    rhs = jnp.concatenate([m01, m01, m01], axis=0)
    return jnp.dot(lhs, rhs, preferred_element_type=F32)


def _dot_01_left(m01, v):
    lhs = jnp.concatenate([m01, m01, m01], axis=1)
    rhs = jnp.concatenate(_split3(v), axis=0)
    return jnp.dot(lhs, rhs, preferred_element_type=F32)


def _norm_mod(x, g, shift, scale):
    ms = jnp.mean(x * x, axis=-1, keepdims=True)
    y = x * lax.rsqrt(ms + NORM_EPS)
    return (y * g) * (1.0 + scale) + shift


def _mod_kernel(c_ref, w_ref, b_ref, o_ref):
    a = _silu(c_ref[...]).astype(BF16)
    o_ref[0] = jnp.dot(a, w_ref[0].astype(BF16), preferred_element_type=F32) + b_ref[0]


def modulation(cc, w_mod, b_mod):
    depth, d, n = w_mod.shape
    tn = 1536
    return pl.pallas_call(
        _mod_kernel,
        out_shape=jax.ShapeDtypeStruct((depth, 8, n), F32),
        grid=(depth, n // tn),
        in_specs=[pl.BlockSpec((8, d), lambda l, j: (0, 0)),
                  pl.BlockSpec((1, d, tn), lambda l, j: (l, 0, j)),
                  pl.BlockSpec((1, 1, tn), lambda l, j: (l, 0, j))],
        out_specs=pl.BlockSpec((1, 8, tn), lambda l, j: (l, 0, j)),
        compiler_params=_cparams(2, VMEM_LIMIT),
        name="modulation",
    )(cc, w_mod, b_mod.reshape(depth, 1, n))


def _norm_matmul_kernel(h_ref, g_ref, mod_ref, w_ref, o_ref, *, shift_idx, scale_idx):
    xn = _norm_mod(h_ref[...], g_ref[...],
                   mod_ref[0, shift_idx:shift_idx + 1, :], mod_ref[0, scale_idx:scale_idx + 1, :])
    o_ref[...] = jnp.dot(xn.astype(BF16), w_ref[...], preferred_element_type=F32).astype(o_ref.dtype)


def norm_matmul(h, gain, modtab, w, *, shift_idx, scale_idx, mod_index, out_dtype):
    t, d = h.shape
    n = w.shape[1]
    kern = functools.partial(_norm_matmul_kernel, shift_idx=shift_idx, scale_idx=scale_idx)
    return pl.pallas_call(
        kern,
        out_shape=jax.ShapeDtypeStruct((t, n), out_dtype),
        grid=(t // TM,),
        in_specs=[pl.BlockSpec((TM, d), lambda i: (i, 0)),
                  pl.BlockSpec((1, d), lambda i: (0, 0)),
                  pl.BlockSpec((1, 6, d), lambda i: (mod_index(i), 0, 0)),
                  pl.BlockSpec((d, n), lambda i: (0, 0))],
        out_specs=pl.BlockSpec((TM, n), lambda i: (i, 0)),
        compiler_params=_cparams(1, VMEM_LIMIT),
        name="norm_matmul",
    )(h, gain.reshape(1, d), modtab, w)


def _resid_matmul_kernel(*refs, n_pairs, gate_idx):
    h_ref, mod_ref = refs[0], refs[1]
    a_refs = refs[2:2 + n_pairs]
    w_refs = refs[2 + n_pairs:2 + 2 * n_pairs]
    o_ref = refs[2 + 2 * n_pairs]
    acc = jnp.dot(a_refs[0][...], w_refs[0][...], preferred_element_type=F32)
    for a_ref, w_ref in zip(a_refs[1:], w_refs[1:]):
        acc = acc + jnp.dot(a_ref[...], w_ref[...], preferred_element_type=F32)
    o_ref[...] = h_ref[...] + mod_ref[0, gate_idx:gate_idx + 1, :] * acc


def resid_matmul(h, modtab, a_list, w_list, *, gate_idx, n_blocks, h_block, mod_index):
    d = h.shape[1]
    n_pairs = len(a_list)
    kern = functools.partial(_resid_matmul_kernel, n_pairs=n_pairs, gate_idx=gate_idx)
    in_specs = [pl.BlockSpec((TM, d), lambda i: (h_block(i), 0)),
                pl.BlockSpec((1, 6, d), lambda i: (mod_index(i), 0, 0))]
    in_specs += [pl.BlockSpec((TM, a.shape[1]), lambda i: (i, 0)) for a in a_list]
    in_specs += [pl.BlockSpec(w.shape, lambda i: (0, 0)) for w in w_list]
    return pl.pallas_call(
        kern,
        out_shape=jax.ShapeDtypeStruct((n_blocks * TM, d), F32),
        grid=(n_blocks,),
        in_specs=in_specs,
        out_specs=pl.BlockSpec((TM, d), lambda i: (i, 0)),
        compiler_params=_cparams(1, VMEM_LIMIT),
        name="resid_matmul",
    )(h, modtab, *a_list, *w_list)


def _dwconv_kernel(cur_ref, prev_ref, next_ref, w_ref, b_ref, o_ref, ext_ref, *, chunks_per_batch, ctx_chunks):
    c = pl.program_id(0) % chunks_per_batch
    is_start = jnp.logical_or(c == 0, c == ctx_chunks)
    is_end = jnp.logical_or(c == ctx_chunks - 1, c == chunks_per_batch - 1)
    half = SSM_CONV_W // 2
    hr = HALO_ROWS
    ext_ref[0:hr, :] = jnp.where(is_start, 0.0, prev_ref[...].astype(F32))
    ext_ref[hr:hr + SSD_CHUNK, :] = cur_ref[...].astype(F32)
    ext_ref[hr + SSD_CHUNK:2 * hr + SSD_CHUNK, :] = jnp.where(is_end, 0.0, next_ref[...].astype(F32))
    acc = b_ref[...] + w_ref[0:1, :] * ext_ref[hr - half:hr - half + SSD_CHUNK, :]
    for tap in range(1, SSM_CONV_W):
        lo = hr - half + tap
        acc = acc + w_ref[tap:tap + 1, :] * ext_ref[lo:lo + SSD_CHUNK, :]
    o_ref[...] = _silu(acc).astype(o_ref.dtype)


def dwconv_silu(p, col_block, width, w, b, *, chunks_per_batch, ctx_chunks):
    t = p.shape[0]
    n_chunks = t // SSD_CHUNK
    sub = SSD_CHUNK // HALO_ROWS
    kern = functools.partial(_dwconv_kernel, chunks_per_batch=chunks_per_batch, ctx_chunks=ctx_chunks)
    return pl.pallas_call(
        kern,
        out_shape=jax.ShapeDtypeStruct((t, width), BF16),
        grid=(n_chunks,),
        in_specs=[pl.BlockSpec((SSD_CHUNK, width), lambda i: (i, col_block)),
                  pl.BlockSpec((HALO_ROWS, width), lambda i: (jnp.maximum(i * sub - 1, 0), col_block)),
                  pl.BlockSpec((HALO_ROWS, width),
                               lambda i: (jnp.minimum((i + 1) * sub, t // HALO_ROWS - 1), col_block)),
                  pl.BlockSpec((SSM_CONV_W, width), lambda i: (0, 0)),
                  pl.BlockSpec((1, width), lambda i: (0, 0))],
        out_specs=pl.BlockSpec((SSD_CHUNK, width), lambda i: (i, 0)),
        scratch_shapes=[pltpu.VMEM((SSD_CHUNK + 2 * HALO_ROWS, width), F32)],
        compiler_params=_cparams(1),
        name="dwconv_silu",
    )(p, p, p, w, b.reshape(1, width))


def _ssd_kernel(*refs, reverse, final):
    if final:
        (x_ref, bc_ref, dt_ref, dtb_ref, a_ref, z_ref, yprev_ref, dskip_ref, ng_ref, y_ref, state_ref) = refs
    else:
        (x_ref, bc_ref, dt_ref, dtb_ref, a_ref, y_ref, state_ref) = refs
    ch = SSD_CHUNK
    n = SSM_STATE
    inner = SSM_INNER
    gw = inner // SSM_GROUPS
    col0 = SSM_HEADS if reverse else 0

    @pl.when(pl.program_id(1) == 0)
    def _():
        state_ref[...] = jnp.zeros_like(state_ref)

    x = x_ref[...].astype(F32)
    bc = bc_ref[...]
    raw = dt_ref[...] + dtb_ref[...]
    dt = jnp.maximum(raw, 0.0) + jnp.log(1.0 + jnp.exp(-jnp.abs(raw)))
    adt = dt * a_ref[...]

    ri = lax.broadcasted_iota(I32, (ch, ch), 0)
    ci = lax.broadcasted_iota(I32, (ch, ch), 1)
    if reverse:
        tri = ci >= ri
        tri_t = ri >= ci
    else:
        tri = ci <= ri
        tri_t = ri <= ci
    acum = _dot_01_left(tri.astype(BF16), adt)
    acum_t = _dot_01_right(adt.T, tri_t.astype(BF16))
    last = 0 if reverse else ch - 1
    total = acum[last:last + 1, :]

    hrow = lax.broadcasted_iota(I32, (LANES, inner), 0)
    hcol = lax.broadcasted_iota(I32, (LANES, inner), 1) // SSM_HEAD_DIM
    sel = (hrow == hcol + col0).astype(BF16)
    stack = jnp.concatenate([dt, jnp.exp(acum), jnp.exp(total - acum),
                             jnp.broadcast_to(jnp.exp(total), (8, LANES))], axis=0)
    expanded = _dot_01_right(stack, sel)
    dt_x = expanded[0:ch]
    ea_x = expanded[ch:2 * ch]
    ds_x = expanded[2 * ch:3 * ch]
    tot_x = expanded[3 * ch:3 * ch + 1]

    xd = x * dt_x
    xw = (xd * ds_x).astype(BF16)
    xd_b = xd.astype(BF16)
    state = state_ref[...]
    lane = lax.broadcasted_iota(I32, (ch, LANES), 1)
    first_half = lane < SSM_HEAD_DIM
    y_parts = []

    for g in range(SSM_GROUPS):
        b_g = bc[:, g * n:(g + 1) * n]
        c_g = bc[:, (SSM_GROUPS + g) * n:(SSM_GROUPS + g + 1) * n]
        cb = lax.dot_general(c_g, b_g, (((1,), (1,)), ((), ())), preferred_element_type=F32)
        gs = slice(g * gw, (g + 1) * gw)
        y_off = jnp.dot(c_g, state[:, gs].astype(BF16), preferred_element_type=F32) * ea_x[:, gs]
        b_t = b_g.astype(F32).T.astype(BF16)
        new_state = tot_x[:, gs] * state[:, gs] + jnp.dot(b_t, xw[:, gs], preferred_element_type=F32)
        state_ref[:, gs] = new_state
        for pair in range(gw // LANES):
            lanes = slice(g * gw + pair * LANES, g * gw + (pair + 1) * LANES)
            halves = []
            for sub in range(2):
                head = (g * gw + pair * LANES) // SSM_HEAD_DIM + sub
                cidx = col0 + head
                diff = acum[:, cidx:cidx + 1] - acum_t[cidx:cidx + 1, :]
                m_h = (cb * jnp.where(tri, jnp.exp(jnp.where(tri, diff, 0.0)), 0.0)).astype(BF16)
                halves.append(jnp.dot(m_h, xd_b[:, lanes], preferred_element_type=F32))
            y_parts.append(jnp.where(first_half, halves[0], halves[1]) + y_off[:, pair * LANES:(pair + 1) * LANES])

    y = jnp.concatenate(y_parts, axis=1)
    if final:
        y = y + yprev_ref[...] + x * dskip_ref[...]
        y = y * _silu(z_ref[...].astype(F32))
        ms = jnp.mean(y * y, axis=-1, keepdims=True)
        y = (y * lax.rsqrt(ms + NORM_EPS)) * ng_ref[...]
    y_ref[...] = y.astype(y_ref.dtype)


def ssd_scan(xc, bcc, dtr, dtb, a_row, *, batch, chunks_per_batch, ctx_chunks, reverse,
             z_src=None, z_col=None, y_prev=None, d_skip=None, norm_g=None):
    t, inner = xc.shape
    final = y_prev is not None
    npb = chunks_per_batch

    def chunk(b, s):
        if not reverse:
            c = s
        else:
            c = jnp.where(s < ctx_chunks, ctx_chunks - 1 - s, npb - 1 + ctx_chunks - s)
        return b * npb + c

    row = lambda b, s: (chunk(b, s), 0)
    const = lambda b, s: (0, 0)
    in_specs = [pl.BlockSpec((SSD_CHUNK, inner), row),
                pl.BlockSpec((SSD_CHUNK, bcc.shape[1]), row),
                pl.BlockSpec((SSD_CHUNK, LANES), row),
                pl.BlockSpec((1, LANES), const),
                pl.BlockSpec((1, LANES), const)]
    args = [xc, bcc, dtr, dtb, a_row]
    if final:
        in_specs += [pl.BlockSpec((SSD_CHUNK, inner), lambda b, s: (chunk(b, s), z_col)),
                     pl.BlockSpec((SSD_CHUNK, inner), row),
                     pl.BlockSpec((1, inner), const),
                     pl.BlockSpec((1, inner), const)]
        args += [z_src, y_prev, d_skip, norm_g]
    kern = functools.partial(_ssd_kernel, reverse=reverse, final=final)
    return pl.pallas_call(
        kern,
        out_shape=jax.ShapeDtypeStruct((t, inner), BF16 if final else F32),
        grid=(batch, npb),
        in_specs=in_specs,
        out_specs=pl.BlockSpec((SSD_CHUNK, inner), row),
        scratch_shapes=[pltpu.VMEM((SSM_STATE, inner), F32)],
        compiler_params=_cparams(2, VMEM_LIMIT),
        name="ssd_bwd" if reverse else "ssd_fwd",
    )(*args)


def _na_bias_kernel(rpb_ref, o_ref):
    cfg = pl.program_id(0)
    head = pl.program_id(1)
    w = GRID_W
    qi = lax.broadcasted_iota(I32, (w, w), 0)
    ki = lax.broadcasted_iota(I32, (w, w), 1)
    cstart = jnp.clip(qi - NA_COLS // 2, 0, w - NA_COLS)
    col_ok = jnp.logical_and(ki >= cstart, ki < cstart + NA_COLS)
    col_idx = jnp.clip(ki - qi + NA_COLS - 1, 0, 2 * NA_COLS - 2)
    n_cfg = pl.num_programs(0)
    for j in range(NA_ROWS):
        rho = jnp.clip(j - cfg + NA_ROWS - 1, 0, 2 * NA_ROWS - 2)
        tile = jnp.zeros((w, w), F32)
        for tcol in range(2 * NA_COLS - 1):
            tile = jnp.where(col_idx == tcol, rpb_ref[head, rho, tcol], tile)
        tile = jnp.where(col_ok, tile, NEG)
        tile = jnp.where(cfg == n_cfg - 1, NEG, tile)
        o_ref[0, 0, :, j * w:(j + 1) * w] = tile


def na_bias_tables(rpb):
    heads = rpb.shape[0]
    n_cfg = NA_ROWS + 1
    return pl.pallas_call(
        _na_bias_kernel,
        out_shape=jax.ShapeDtypeStruct((n_cfg, heads, GRID_W, NA_ROWS * GRID_W), F32),
        grid=(n_cfg, heads),
        in_specs=[pl.BlockSpec(memory_space=pltpu.SMEM)],
        out_specs=pl.BlockSpec((1, 1, GRID_W, NA_ROWS * GRID_W), lambda c, h: (c, h, 0, 0)),
        compiler_params=_cparams(2),
        name="na_bias_tables",
    )(rpb)


def _na_kernel(*refs):
    q_ref = refs[0]
    k_refs = refs[1:1 + NA_ROWS]
    v_refs = refs[1 + NA_ROWS:1 + 2 * NA_ROWS]
    kc_ref, vc_ref, bias_ref, o_ref, s_ref, p_ref = refs[1 + 2 * NA_ROWS:]
    w = GRID_W
    n_lat = NA_ROWS * w
    scale = HEAD_DIM ** -0.5
    lane_q = lax.broadcasted_iota(I32, (w, LANES), 1)
    first_half = lane_q < HEAD_DIM
    nt = (((1,), (1,)), ((), ()))
    for pair in range(NA_HEADS // 2):
        lanes = slice(pair * LANES, (pair + 1) * LANES)
        q2 = q_ref[:, lanes] * scale
        kk = jnp.concatenate([r[:, lanes] for r in k_refs] + [kc_ref[:, lanes]], axis=0)
        for sub in range(2):
            keep = first_half if sub == 0 else jnp.logical_not(first_half)
            qz = jnp.where(keep, q2, jnp.zeros_like(q2))
            s_ref[2 * pair + sub] = lax.dot_general(qz, kk, nt, preferred_element_type=F32)
    inv_den = []
    for head in range(NA_HEADS):
        s_lat = s_ref[head, :, 0:n_lat] + bias_ref[0, head]
        s_ctx = s_ref[head, :, n_lat:]
        m = jnp.maximum(jnp.max(s_lat, axis=-1, keepdims=True), jnp.max(s_ctx, axis=-1, keepdims=True))
        p_lat = jnp.exp(s_lat - m)
        p_ctx = jnp.exp(s_ctx - m)
        inv_den.append(1.0 / (jnp.sum(p_lat, axis=-1, keepdims=True) + jnp.sum(p_ctx, axis=-1, keepdims=True)))
        p_ref[head, :, 0:n_lat] = p_lat.astype(BF16)
        p_ref[head, :, n_lat:] = p_ctx.astype(BF16)
    for pair in range(NA_HEADS // 2):
        lanes = slice(pair * LANES, (pair + 1) * LANES)
        vv = jnp.concatenate([r[:, lanes] for r in v_refs] + [vc_ref[:, lanes]], axis=0)
        o0 = jnp.dot(p_ref[2 * pair], vv, preferred_element_type=F32) * inv_den[2 * pair]
        o1 = jnp.dot(p_ref[2 * pair + 1], vv, preferred_element_type=F32) * inv_den[2 * pair + 1]
        o_ref[:, lanes] = jnp.where(first_half, o0, o1).astype(o_ref.dtype)


def na_attention(p, bias, *, batch, rows_per_batch, ctx_len, q_col, k_col, v_col):
    t = p.shape[0]
    w = GRID_W
    width = NA_HEADS * HEAD_DIM
    blocks_per_batch = rows_per_batch + ctx_len // w
    ctx_blocks = ctx_len // w
    lat_rows = rows_per_batch

    def win_start(st):
        r = st - ctx_blocks
        return jnp.clip(r - NA_ROWS // 2, 0, lat_rows - NA_ROWS)

    def cfg_of(st):
        r = st - ctx_blocks
        return jnp.where(st < ctx_blocks, NA_ROWS, r - win_start(st))

    def kv_spec(j, col):
        return pl.BlockSpec((w, width), lambda b, st: (b * blocks_per_batch + ctx_blocks + win_start(st) + j, col))

    in_specs = [pl.BlockSpec((w, width), lambda b, st: (b * blocks_per_batch + st, q_col))]
    in_specs += [kv_spec(j, k_col) for j in range(NA_ROWS)]
    in_specs += [kv_spec(j, v_col) for j in range(NA_ROWS)]
    blocks_ctx = (rows_per_batch * w + ctx_len) // ctx_len
    in_specs += [pl.BlockSpec((ctx_len, width), lambda b, st: (b * blocks_ctx, k_col)),
                 pl.BlockSpec((ctx_len, width), lambda b, st: (b * blocks_ctx, v_col)),
                 pl.BlockSpec((1, NA_HEADS, w, NA_ROWS * w), lambda b, st: (cfg_of(st), 0, 0, 0))]
    args = [p] * (3 + 2 * NA_ROWS) + [bias]
    return pl.pallas_call(
        _na_kernel,
        out_shape=jax.ShapeDtypeStruct((t, width), BF16),
        grid=(batch, blocks_per_batch),
        in_specs=in_specs,
        out_specs=pl.BlockSpec((w, width), lambda b, st: (b * blocks_per_batch + st, 0)),
        scratch_shapes=[pltpu.VMEM((NA_HEADS, w, NA_ROWS * w + ctx_len), F32),
                        pltpu.VMEM((NA_HEADS, w, NA_ROWS * w + ctx_len), BF16)],
        compiler_params=_cparams(2, VMEM_LIMIT),
        name="na_attention",
    )(*args)


def _group_mean_sq(x):
    gi = lax.broadcasted_iota(I32, (LANES, LANES), 0) // HEAD_DIM
    gj = lax.broadcasted_iota(I32, (LANES, LANES), 1) // HEAD_DIM
    ones = (gi == gj).astype(BF16)
    return _dot_01_right(x * x, ones) * (1.0 / HEAD_DIM)


def _rope(x, cos_f, sin_s):
    lane = lax.broadcasted_iota(I32, x.shape, 1)
    low = (lane % HEAD_DIM) < HEAD_DIM // 2
    partner = jnp.where(low, pltpu.roll(x, LANES - HEAD_DIM // 2, 1), pltpu.roll(x, HEAD_DIM // 2, 1))
    return x * cos_f + partner * sin_s


def _gqa_prep_kernel(p_ref, cos_ref, sin_ref, qg_ref, kg_ref, qt_ref, k_ref, vt_ref, *, ctx_blocks):
    j = pl.program_id(1)
    is_lat = j >= ctx_blocks
    q_w = GQA_HEADS * HEAD_DIM
    kv_w = GQA_KV_HEADS * HEAD_DIM
    cos_f = jnp.where(is_lat, cos_ref[...], 1.0)
    sin_s = jnp.where(is_lat, sin_ref[...], 0.0)

    def normed(cols, gain):
        x = p_ref[:, cols].astype(F32)
        y = x * lax.rsqrt(_group_mean_sq(x) + NORM_EPS)
        return _rope(y * gain, cos_f, sin_s)

    for tile in range(kv_w // LANES):
        cols = slice(q_w + tile * LANES, q_w + (tile + 1) * LANES)
        k_ref[:, tile * LANES:(tile + 1) * LANES] = normed(cols, kg_ref[...]).astype(k_ref.dtype)
    v = p_ref[:, q_w + kv_w:q_w + 2 * kv_w].astype(F32)
    vt_ref[0] = v.T.astype(vt_ref.dtype)

    @pl.when(is_lat)
    def _():
        scale = HEAD_DIM ** -0.5 * math.log2(math.e)
        for tile in range(q_w // LANES):
            cols = slice(tile * LANES, (tile + 1) * LANES)
            q = normed(cols, qg_ref[...]) * scale
            qt_ref[0, cols, :] = q.T.astype(qt_ref.dtype)


def gqa_prep(p1, cos_f, sin_s, qg, kg, *, batch, blocks_per_batch, ctx_blocks):
    t, n = p1.shape
    q_w = GQA_HEADS * HEAD_DIM
    kv_w = GQA_KV_HEADS * HEAD_DIM
    lb = blocks_per_batch * TM
    lat = (blocks_per_batch - ctx_blocks) * TM
    kern = functools.partial(_gqa_prep_kernel, ctx_blocks=ctx_blocks)
    lat_blk = lambda b, j: jnp.maximum(j - ctx_blocks, 0)
    return pl.pallas_call(
        kern,
        out_shape=(jax.ShapeDtypeStruct((batch, q_w, lat), BF16),
                   jax.ShapeDtypeStruct((t, kv_w), BF16),
                   jax.ShapeDtypeStruct((batch, kv_w, lb), BF16)),
        grid=(batch, blocks_per_batch),
        in_specs=[pl.BlockSpec((TM, n), lambda b, j: (b * blocks_per_batch + j, 0)),
                  pl.BlockSpec((TM, LANES), lambda b, j: (lat_blk(b, j), 0)),
                  pl.BlockSpec((TM, LANES), lambda b, j: (lat_blk(b, j), 0)),
                  pl.BlockSpec((1, LANES), lambda b, j: (0, 0)),
                  pl.BlockSpec((1, LANES), lambda b, j: (0, 0))],
        out_specs=(pl.BlockSpec((1, q_w, TM), lambda b, j: (b, 0, lat_blk(b, j))),
                   pl.BlockSpec((TM, kv_w), lambda b, j: (b * blocks_per_batch + j, 0)),
                   pl.BlockSpec((1, kv_w, TM), lambda b, j: (b, 0, j))),
        compiler_params=_cparams(2, VMEM_LIMIT),
        name="gqa_prep",
    )(p1, cos_f, sin_s, qg, kg)


def _gqa_attn_kernel(qt_ref, k_ref, vt_ref, o_ref, qz_ref, s_ref, bmax_ref, m_ref, l_ref, acc_ref, *, tk, n_kblocks):
    kv = pl.program_id(1)
    tq = qt_ref.shape[2]
    group = GQA_HEADS // GQA_KV_HEADS
    second = (kv % 2) == 1
    for hq in range(group):
        q_h = qt_ref[0, hq * HEAD_DIM:(hq + 1) * HEAD_DIM, :]
        zero = jnp.zeros_like(q_h)
        qz_ref[0:HEAD_DIM, hq * tq:(hq + 1) * tq] = jnp.where(second, zero, q_h)
        qz_ref[HEAD_DIM:2 * HEAD_DIM, hq * tq:(hq + 1) * tq] = jnp.where(second, q_h, zero)
    m_ref[...] = jnp.full_like(m_ref, NEG)
    l_ref[...] = jnp.zeros_like(l_ref)
    acc_ref[...] = jnp.zeros_like(acc_ref)

    def scores(kb, slot):
        off = pl.multiple_of(kb * tk, tk)
        s = jnp.dot(k_ref[0, pl.ds(off, tk), :], qz_ref[...], preferred_element_type=F32)
        s_ref[slot] = s
        bmax_ref[slot] = jnp.max(s, axis=0, keepdims=True)

    def update(kb, slot):
        off = pl.multiple_of(kb * tk, tk)
        s = s_ref[slot]
        m_old = m_ref[...]
        m_new = jnp.maximum(m_old, bmax_ref[slot])
        alpha = jnp.exp2(m_old - m_new)
        p = jnp.exp2(s - m_new).astype(BF16)
        v_aug = jnp.concatenate([vt_ref[0, :, pl.ds(off, tk)], jnp.ones((ONES_ROWS, tk), BF16)], axis=0)
        pv = jnp.dot(v_aug, p, preferred_element_type=F32)
        acc_ref[...] = alpha * acc_ref[...] + pv[0:HEAD_DIM]
        l_ref[...] = alpha * l_ref[...] + pv[HEAD_DIM:HEAD_DIM + 1]
        m_ref[...] = m_new

    per_trip = GQA_BLOCKS_PER_TRIP
    scores(0, 0)
    trips = (n_kblocks - 1) // per_trip

    def one_trip(i, carry):
        kb = per_trip * i
        for j in range(per_trip):
            scores(kb + j + 1, (j + 1) % 2)
            update(kb + j, j % 2)
        return carry

    lax.fori_loop(0, trips, one_trip, 0)
    base = trips * per_trip
    for j in range(n_kblocks - base):
        if base + j + 1 < n_kblocks:
            scores(base + j + 1, (j + 1) % 2)
        update(base + j, j % 2)

    inv_l = 1.0 / l_ref[...]
    for pair in range(group // 2):
        lo, hi = 2 * pair * tq, (2 * pair + 1) * tq
        o_t = jnp.concatenate([acc_ref[:, lo:lo + tq] * inv_l[:, lo:lo + tq],
                               acc_ref[:, hi:hi + tq] * inv_l[:, hi:hi + tq]], axis=0)
        o_ref[:, pair * LANES:(pair + 1) * LANES] = o_t.T.astype(o_ref.dtype)


def gqa_attention(qt, k, vt, *, tq=512, tk=256):
    batch, q_w, lat = qt.shape
    lb = k.shape[1]
    group = GQA_HEADS // GQA_KV_HEADS
    gw = group * HEAD_DIM
    tq = min(tq, lat)
    kern = functools.partial(_gqa_attn_kernel, tk=tk, n_kblocks=lb // tk)
    return pl.pallas_call(
        kern,
        out_shape=jax.ShapeDtypeStruct((batch * lat, q_w), BF16),
        grid=(batch, GQA_KV_HEADS, lat // tq),
        in_specs=[pl.BlockSpec((1, gw, tq), lambda b, h, i: (b, h, i)),
                  pl.BlockSpec((1, lb, LANES), lambda b, h, i: (b, 0, h // 2)),
                  pl.BlockSpec((1, HEAD_DIM, lb), lambda b, h, i: (b, h, 0))],
        out_specs=pl.BlockSpec((tq, gw), lambda b, h, i: (b * (lat // tq) + i, h)),
        scratch_shapes=[pltpu.VMEM((2 * HEAD_DIM, group * tq), BF16),
                        pltpu.VMEM((2, tk, group * tq), F32),
                        pltpu.VMEM((2, 1, group * tq), F32),
                        pltpu.VMEM((1, group * tq), F32),
                        pltpu.VMEM((1, group * tq), F32),
                        pltpu.VMEM((HEAD_DIM, group * tq), F32)],
        compiler_params=_cparams(3, VMEM_LIMIT),
        name="gqa_attention",
    )(qt, k, vt)


def _router_kernel(h_ref, g_ref, mod_ref, rw_ref, rb_ref, xn_ref, eidx_ref, wts_ref, rank_ref, cnt_ref, run_ref):
    @pl.when(pl.program_id(0) == 0)
    def _():
        run_ref[...] = jnp.zeros_like(run_ref)

    xn = _norm_mod(h_ref[...], g_ref[...], mod_ref[0, 3:4, :], mod_ref[0, 4:5, :])
    xn_ref[...] = xn
    tm = xn.shape[0]
    logits = lax.dot_general(rw_ref[...], xn, (((1,), (1,)), ((), ())),
                             precision=HIGHEST, preferred_element_type=F32)
    s = jax.nn.sigmoid(logits)
    sb = s + rb_ref[...]
    row = lambda a, e: a[e:e + 1, :]

    best = None
    for g in range(N_GROUPS):
        a = [row(sb, g * EXPERTS_PER_GROUP + j) for j in range(EXPERTS_PER_GROUP)]
        score = a[0] + a[1]
        for i in range(EXPERTS_PER_GROUP):
            for j in range(i + 1, EXPERTS_PER_GROUP):
                if (i, j) != (0, 1):
                    score = jnp.maximum(score, a[i] + a[j])
        if best is None:
            best, grp = score, jnp.zeros((1, tm), I32)
        else:
            better = score > best
            best = jnp.where(better, score, best)
            grp = jnp.where(better, g, grp)

    biased, plain = [], []
    for j in range(EXPERTS_PER_GROUP):
        vb = row(sb, j)
        vp = row(s, j)
        for g in range(1, N_GROUPS):
            pick = grp == g
            vb = jnp.where(pick, row(sb, g * EXPERTS_PER_GROUP + j), vb)
            vp = jnp.where(pick, row(s, g * EXPERTS_PER_GROUP + j), vp)
        biased.append(vb)
        plain.append(vp)

    def argmax_first(vals):
        top, idx = vals[0], jnp.zeros((1, tm), I32)
        for j in range(1, len(vals)):
            better = vals[j] > top
            top = jnp.where(better, vals[j], top)
            idx = jnp.where(better, j, idx)
        return idx

    i1 = argmax_first(biased)
    i2 = argmax_first([jnp.where(i1 == j, -jnp.inf, biased[j]) for j in range(EXPERTS_PER_GROUP)])

    def pick_plain(idx):
        out = plain[0]
        for j in range(1, EXPERTS_PER_GROUP):
            out = jnp.where(idx == j, plain[j], out)
        return out

    w1, w2 = pick_plain(i1), pick_plain(i2)
    wsum = w1 + w2
    e1 = grp * EXPERTS_PER_GROUP + i1
    e2 = grp * EXPERTS_PER_GROUP + i2
    eidx_ref[0] = jnp.concatenate([e1, e2], axis=0)
    wts_ref[...] = jnp.concatenate([w1 / wsum, w2 / wsum], axis=0)

    erow = lax.broadcasted_iota(I32, (N_EXPERTS, tm), 0)
    hit1 = erow == e1
    hit2 = erow == e2
    onehot = jnp.logical_or(hit1, hit2)
    si = lax.broadcasted_iota(I32, (tm, tm), 0)
    ti = lax.broadcasted_iota(I32, (tm, tm), 1)
    before = jnp.dot(onehot.astype(BF16), (si < ti).astype(BF16), preferred_element_type=F32)
    before = before + run_ref[...][:, 0:1]
    r1 = jnp.sum(jnp.where(hit1, before, 0.0), axis=0, keepdims=True)
    r2 = jnp.sum(jnp.where(hit2, before, 0.0), axis=0, keepdims=True)
    rank_ref[0] = jnp.concatenate([r1, r2], axis=0).astype(I32)
    totals = run_ref[...] + jnp.sum(onehot.astype(F32), axis=1, keepdims=True)
    run_ref[...] = totals
    cnt_ref[...] = totals


def moe_router(h, gain, modtab, rw_t, rb, *, mod_index):
    t, d = h.shape
    return pl.pallas_call(
        _router_kernel,
        out_shape=(jax.ShapeDtypeStruct((t, d), F32),
                   jax.ShapeDtypeStruct((t // TM, 2, TM), I32),
                   jax.ShapeDtypeStruct((2, t), F32),
                   jax.ShapeDtypeStruct((t // TM, 2, TM), I32),
                   jax.ShapeDtypeStruct((N_EXPERTS, LANES), F32)),
        grid=(t // TM,),
        in_specs=[pl.BlockSpec((TM, d), lambda i: (i, 0)),
                  pl.BlockSpec((1, d), lambda i: (0, 0)),
                  pl.BlockSpec((1, 6, d), lambda i: (mod_index(i), 0, 0)),
                  pl.BlockSpec((N_EXPERTS, d), lambda i: (0, 0)),
                  pl.BlockSpec((N_EXPERTS, 1), lambda i: (0, 0))],
        out_specs=(pl.BlockSpec((TM, d), lambda i: (i, 0)),
                   pl.BlockSpec((1, 2, TM), lambda i: (i, 0, 0)),
                   pl.BlockSpec((2, TM), lambda i: (0, i)),
                   pl.BlockSpec((1, 2, TM), lambda i: (i, 0, 0)),
                   pl.BlockSpec((N_EXPERTS, LANES), lambda i: (0, 0))),
        scratch_shapes=[pltpu.VMEM((N_EXPERTS, LANES), F32)],
        compiler_params=_cparams(1, VMEM_LIMIT),
        name="moe_router",
    )(h, gain.reshape(1, d), modtab, rw_t, rb)


def _row_copy(src_ref, src_row, dst_ref, dst_row, sem):
    return pltpu.make_async_copy(src_ref.at[pl.ds(src_row, 1)], dst_ref.at[pl.ds(dst_row, 1)], sem)


def _issue_and_drain(rows, make_copy):
    def issue(r, carry):
        for k in range(2):
            make_copy(r, k).start()
        return carry

    lax.fori_loop(0, rows, issue, 0, unroll=8)
    for _ in range(rows):
        for k in range(2):
            make_copy(0, k).wait()


def _dispatch_kernel(eidx_ref, rank_ref, start_ref, x_ref, xs_in_ref, xs_ref, sem):
    del xs_in_ref

    def make_copy(r, k):
        slot = start_ref[eidx_ref[0, k, r]] + rank_ref[0, k, r]
        return _row_copy(x_ref, r, xs_ref, slot, sem)

    _issue_and_drain(x_ref.shape[0], make_copy)


def moe_dispatch(xn, eidx3, rank3, pad_start, n_slots):
    t, d = xn.shape
    xs0 = jnp.zeros((n_slots, d), xn.dtype)
    idx_spec = pl.BlockSpec((1, 2, DMA_ROWS), lambda i: (i, 0, 0), memory_space=pltpu.SMEM)
    return pl.pallas_call(
        _dispatch_kernel,
        out_shape=jax.ShapeDtypeStruct((n_slots, d), xn.dtype),
        grid=(t // DMA_ROWS,),
        in_specs=[idx_spec, idx_spec,
                  pl.BlockSpec(memory_space=pltpu.SMEM),
                  pl.BlockSpec((DMA_ROWS, d), lambda i: (i, 0)),
                  pl.BlockSpec(memory_space=pl.ANY)],
        out_specs=pl.BlockSpec(memory_space=pl.ANY),
        scratch_shapes=[pltpu.SemaphoreType.DMA(())],
        input_output_aliases={4: 0},
        compiler_params=_cparams(1),
        name="moe_dispatch",
    )(eidx3, rank3, pad_start, xn, xs0)


def _expert_kernel(blk_e_ref, nused_ref, x_ref, wg_ref, wu_ref, wd_ref, y_ref):
    del blk_e_ref

    @pl.when(pl.program_id(0) < nused_ref[0])
    def _():
        x = x_ref[...].astype(BF16)
        gate = jnp.dot(x, wg_ref[0].astype(BF16), preferred_element_type=F32)
        up = jnp.dot(x, wu_ref[0].astype(BF16), preferred_element_type=F32)
        hid = (_silu(gate) * up).astype(BF16)
        y_ref[...] = jnp.dot(hid, wd_ref[0].astype(BF16), preferred_element_type=F32)

    @pl.when(pl.program_id(0) >= nused_ref[0])
    def _():
        y_ref[...] = jnp.zeros_like(y_ref)


def moe_experts(xs, blk_e, nused, wg, wu, wd):
    n_slots, d = xs.shape
    de = wg.shape[2]
    grid_spec = pltpu.PrefetchScalarGridSpec(
        num_scalar_prefetch=2,
        grid=(n_slots // MOE_BM,),
        in_specs=[pl.BlockSpec((MOE_BM, d), lambda i, be, nu: (i, 0)),
                  pl.BlockSpec((1, d, de), lambda i, be, nu: (be[i], 0, 0)),
                  pl.BlockSpec((1, d, de), lambda i, be, nu: (be[i], 0, 0)),
                  pl.BlockSpec((1, de, d), lambda i, be, nu: (be[i], 0, 0))],
        out_specs=pl.BlockSpec((MOE_BM, d), lambda i, be, nu: (i, 0)),
    )
    return pl.pallas_call(
        _expert_kernel,
        out_shape=jax.ShapeDtypeStruct((n_slots, d), F32),
        grid_spec=grid_spec,
        compiler_params=_cparams(1, VMEM_LIMIT),
        name="moe_experts",
    )(blk_e, nused, xs, wg, wu, wd)


def _combine_kernel(eidx_ref, rank_ref, eidx_nxt_ref, rank_nxt_ref, start_ref, h_ref, mod_ref, w_ref, fg_ref,
                    yb_ref, o_ref, buf_ref, sems, *, final):
    step = pl.program_id(0)
    rows = h_ref.shape[0]

    def start_gathers(e_ref, r_ref, half):
        def issue(r, carry):
            for k in range(2):
                slot = start_ref[e_ref[0, k, r]] + r_ref[0, k, r]
                _row_copy(yb_ref, slot, buf_ref.at[half, k], r, sems.at[half]).start()
            return carry

        lax.fori_loop(0, rows, issue, 0, unroll=8)

    cur = step % 2

    @pl.when(step == 0)
    def _():
        start_gathers(eidx_ref, rank_ref, 0)

    @pl.when(step + 1 < pl.num_programs(0))
    def _():
        start_gathers(eidx_nxt_ref, rank_nxt_ref, 1 - cur)

    for _ in range(rows):
        for k in range(2):
            _row_copy(yb_ref, 0, buf_ref.at[cur, k], 0, sems.at[cur]).wait()
    f = w_ref[:, 0:1] * buf_ref[cur, 0] + w_ref[:, 1:2] * buf_ref[cur, 1]
    h = h_ref[...] + mod_ref[0, 5:6, :] * f
    if final:
        ms = jnp.mean(h * h, axis=-1, keepdims=True)
        h = (h * lax.rsqrt(ms + NORM_EPS)) * fg_ref[...]
    o_ref[...] = h


def moe_combine(h, modtab, wts_t, final_g, yb, eidx3, rank3, pad_start, *, mod_index, final):
    t, d = h.shape
    kern = functools.partial(_combine_kernel, final=final)
    n_steps = t // DMA_ROWS
    idx_spec = pl.BlockSpec((1, 2, DMA_ROWS), lambda i: (i, 0, 0), memory_space=pltpu.SMEM)
    nxt_spec = pl.BlockSpec((1, 2, DMA_ROWS), lambda i: (jnp.minimum(i + 1, n_steps - 1), 0, 0),
                            memory_space=pltpu.SMEM)
    return pl.pallas_call(
        kern,
        out_shape=jax.ShapeDtypeStruct((t, d), F32),
        grid=(n_steps,),
        in_specs=[idx_spec, idx_spec, nxt_spec, nxt_spec,
                  pl.BlockSpec(memory_space=pltpu.SMEM),
                  pl.BlockSpec((DMA_ROWS, d), lambda i: (i, 0)),
                  pl.BlockSpec((1, 6, d), lambda i: (mod_index(i), 0, 0)),
                  pl.BlockSpec((DMA_ROWS, 2), lambda i: (i, 0)),
                  pl.BlockSpec((1, d), lambda i: (0, 0)),
                  pl.BlockSpec(memory_space=pl.ANY)],
        out_specs=pl.BlockSpec((DMA_ROWS, d), lambda i: (i, 0)),
        scratch_shapes=[pltpu.VMEM((2, 2, DMA_ROWS, d), F32), pltpu.SemaphoreType.DMA((2,))],
        compiler_params=_cparams(1, VMEM_LIMIT),
        name="moe_combine",
    )(eidx3, rank3, eidx3, rank3, pad_start, h, modtab, wts_t, final_g.reshape(1, d), yb)


def moe_layer(h, norm_g, modtab, rw_t, rb, wg, wu, wd, final_g, *, mod_index, final):
    t, d = h.shape
    xn, eidx3, wts, rank3, counts = moe_router(h, norm_g, modtab, rw_t, rb, mod_index=mod_index)
    n_blocks = (2 * t) // MOE_BM + N_EXPERTS
    n_slots = n_blocks * MOE_BM
    cnt = counts[:, 0].astype(I32)
    padded = (cnt + MOE_BM - 1) // MOE_BM * MOE_BM
    pad_end = jnp.cumsum(padded)
    pad_start = (pad_end - padded).astype(I32)
    blk_start = jnp.arange(n_blocks, dtype=I32) * MOE_BM
    blk_e = jnp.minimum(jnp.sum((pad_end[None, :] <= blk_start[:, None]).astype(I32), axis=1), N_EXPERTS - 1)
    nused = (pad_end[-1] // MOE_BM).reshape(1).astype(I32)
    xs = moe_dispatch(xn, eidx3, rank3, pad_start, n_slots)
    yb = moe_experts(xs, blk_e.astype(I32), nused, wg, wu, wd)
    return moe_combine(h, modtab, wts.T, final_g, yb, eidx3, rank3, pad_start, mod_index=mod_index, final=final)


def _rope_tables(lat):
    t = jnp.arange(lat)
    row = (t // GRID_W).astype(F32)
    col = (t % GRID_W).astype(F32)
    n_freq = HEAD_DIM // 4
    inv = ROPE_THETA ** (-jnp.arange(n_freq, dtype=F32) / n_freq)
    ang = jnp.concatenate([row[:, None] * inv, col[:, None] * inv], axis=-1)
    cos, sin = jnp.cos(ang), jnp.sin(ang)
    cos_f = jnp.tile(cos, (1, LANES // (HEAD_DIM // 2)))
    sin_s = jnp.tile(jnp.concatenate([-sin, sin], axis=-1), (1, LANES // HEAD_DIM))
    return cos_f, sin_s


def kernel(x, c, ctx, c_ctx, w_mod, b_mod, norm1_g, norm2_g, final_g, ab_w_in, ab_w_out, ssm_conv_w, ssm_conv_b, ssm_A_log, ssm_dt_bias, ssm_D, ssm_norm_g, na_rpb, gqa_w_in, gqa_w_out, gqa_q_norm, gqa_k_norm, router_w, router_bias, moe_w_gate, moe_w_up, moe_w_down):
    bsz, lat, d = x.shape
    lc = ctx.shape[1]
    lb = lc + lat
    assert lat % TM == 0 and lc % TM == 0 and lat % GRID_W == 0 and lat // GRID_W >= NA_ROWS
    assert bsz + 1 <= 8 and w_mod.shape[0] == 2
    nb = lb // TM
    cb = lc // TM
    nlb = lat // TM

    def mod_comb(i):
        return jnp.where(i % nb < cb, bsz, i // nb)

    def mod_lat(i):
        return i // nlb

    inner, xbc_w = SSM_INNER, SSM_INNER + 2 * SSM_GROUPS * SSM_STATE
    w_in = ab_w_in[0]
    o_dt = inner + xbc_w
    o_q = o_dt + 2 * SSM_HEADS
    w_z, w_x, w_bc = w_in[:, :inner], w_in[:, inner:2 * inner], w_in[:, 2 * inner:o_dt]
    w_dt = w_in[:, o_dt:o_q]
    w_qkv = w_in[:, o_q:]
    w_in_b = jnp.concatenate([w_qkv, w_z, w_x, w_bc], axis=1).astype(BF16)
    w_dt_b = jnp.pad(w_dt, ((0, 0), (0, LANES - 2 * SSM_HEADS))).astype(BF16)
    conv_w, conv_b = ssm_conv_w[0], ssm_conv_b[0]
    pad32 = lambda v: jnp.pad(v.reshape(1, -1), ((0, 0), (0, LANES - 2 * SSM_HEADS)))
    dtb_row = pad32(ssm_dt_bias[0])
    a_row = pad32(-jnp.exp(ssm_A_log[0].astype(F32)))
    d_skip = jnp.repeat(ssm_D[0], SSM_HEAD_DIM).reshape(1, inner)
    w_out = ab_w_out[0].astype(BF16)
    w_gqa = gqa_w_in[0].astype(BF16)
    w_gqa_out = gqa_w_out[0].astype(BF16)
    qg = jnp.tile(gqa_q_norm[0], LANES // HEAD_DIM).reshape(1, LANES)
    kg = jnp.tile(gqa_k_norm[0], LANES // HEAD_DIM).reshape(1, LANES)
    rw_t = router_w.T
    rb = router_bias.reshape(N_EXPERTS, 1)
    wg, wu, wd = moe_w_gate, moe_w_up, moe_w_down

    cc = jnp.concatenate([c, c_ctx[None, :], jnp.zeros((8 - bsz - 1, d), F32)], axis=0)
    mod = modulation(cc, w_mod, b_mod)[:, :bsz + 1].reshape(2, bsz + 1, 6, d)

    h0 = jnp.concatenate([ctx, x], axis=1).reshape(bsz * lb, d)

    p = norm_matmul(h0, norm1_g[0], mod[0], w_in_b, shift_idx=0, scale_idx=1, mod_index=mod_comb, out_dtype=BF16)
    dtr = norm_matmul(h0, norm1_g[0], mod[0], w_dt_b, shift_idx=0, scale_idx=1, mod_index=mod_comb, out_dtype=F32)
    chunks = lb // SSD_CHUNK
    ctx_chunks = lc // SSD_CHUNK
    xc = dwconv_silu(p, 4, inner, conv_w[:, :inner], conv_b[:inner], chunks_per_batch=chunks, ctx_chunks=ctx_chunks)
    bcc = dwconv_silu(p, 10, xbc_w - inner, conv_w[:, inner:], conv_b[inner:],
                      chunks_per_batch=chunks, ctx_chunks=ctx_chunks)
    scan = functools.partial(ssd_scan, xc, bcc, dtr, dtb_row, a_row, batch=bsz,
                             chunks_per_batch=chunks, ctx_chunks=ctx_chunks)
    y_f = scan(reverse=False)
    y_ssd = scan(reverse=True, z_src=p, z_col=3, y_prev=y_f, d_skip=d_skip, norm_g=ssm_norm_g[0].reshape(1, inner))
    bias = na_bias_tables(na_rpb[0])
    o_na = na_attention(p, bias, batch=bsz, rows_per_batch=lat // GRID_W, ctx_len=lc, q_col=0, k_col=1, v_col=2)
    ident = lambda i: i
    h1 = resid_matmul(h0, mod[0], [y_ssd, o_na], [w_out[:inner], w_out[inner:]],
                      gate_idx=2, n_blocks=bsz * nb, h_block=ident, mod_index=mod_comb)
    h2 = moe_layer(h1, norm2_g[0], mod[0], rw_t, rb, wg[0], wu[0], wd[0], final_g, mod_index=mod_comb, final=False)

    p1 = norm_matmul(h2, norm1_g[1], mod[1], w_gqa, shift_idx=0, scale_idx=1, mod_index=mod_comb, out_dtype=BF16)
    cos_f, sin_s = _rope_tables(lat)
    qt, kn, vt = gqa_prep(p1, cos_f, sin_s, qg, kg, batch=bsz, blocks_per_batch=nb, ctx_blocks=cb)
    o = gqa_attention(qt, kn.reshape(bsz, lb, GQA_KV_HEADS * HEAD_DIM), vt)
    lat_block = lambda i: (i // nlb) * nb + cb + i % nlb
    h3 = resid_matmul(h2, mod[1], [o], [w_gqa_out], gate_idx=2, n_blocks=bsz * nlb, h_block=lat_block, mod_index=mod_lat)
    out = moe_layer(h3, norm2_g[1], mod[1], rw_t, rb, wg[1], wu[1], wd[1], final_g, mod_index=mod_lat, final=True)
    return out.reshape(bsz, lat, d)
```

```python
import functools
import math

import jax
import jax.numpy as jnp
from jax import lax
from jax.experimental import pallas as pl
from jax.experimental.pallas import tpu as pltpu

F32 = jnp.float32
BF16 = jnp.bfloat16
I32 = jnp.int32
HIGHEST = lax.Precision.HIGHEST

NORM_EPS = 1e-6
GRID_W = 64
SSM_HEADS = 16
SSM_HEAD_DIM = 64
SSM_INNER = 1024
SSM_STATE = 128
SSM_GROUPS = 2
SSM_CONV_W = 5
SSD_CHUNK = 128
NA_HEADS = 16
NA_ROWS = 8
NA_COLS = 16
NA_PAIR = 2
NA_WIN_ROWS = 10
HEAD_DIM = 64
GQA_HEADS = 16
GQA_KV_HEADS = 4
ROPE_THETA = 10000.0
N_EXPERTS = 32
N_GROUPS = 8
EXPERTS_PER_GROUP = 4
D_EXPERT = 512

LANES = 128
HALO_ROWS = 16
ONES_ROWS = 16
GQA_BLOCKS_PER_TRIP = 4
TM = 256
MOE_BM = 256
DMA_ROWS = TM
NEG = -1e30
VMEM_LIMIT = 48 * 1024 * 1024


def _cparams(n_axes, vmem=None):
    return pltpu.CompilerParams(dimension_semantics=("arbitrary",) * n_axes,
                                vmem_limit_bytes=vmem)


def _silu(x):
    return x * jax.nn.sigmoid(x)


def _split3(v):
    hi = v.astype(BF16)
    r1 = v - hi.astype(F32)
    mid = r1.astype(BF16)
    lo = (r1 - mid.astype(F32)).astype(BF16)
    return hi, mid, lo


def _dot_01_right(v, m01):
    lhs = jnp.concatenate(_split3(v), axis=1)
    rhs = jnp.concatenate([m01, m01, m01], axis=0)
    return jnp.dot(lhs, rhs, preferred_element_type=F32)


def _dot_01_left(m01, v):
    lhs = jnp.concatenate([m01, m01, m01], axis=1)
    rhs = jnp.concatenate(_split3(v), axis=0)
    return jnp.dot(lhs, rhs, preferred_element_type=F32)


def _norm_mod(x, g, shift, scale):
    ms = jnp.mean(x * x, axis=-1, keepdims=True)
    y = x * lax.rsqrt(ms + NORM_EPS)
    return (y * g) * (1.0 + scale) + shift


def _mod_kernel(c_ref, w_ref, b_ref, o_ref):
    a = _silu(c_ref[...]).astype(BF16)
    o_ref[0] = jnp.dot(a, w_ref[0].astype(BF16), preferred_element_type=F32) + b_ref[0]


def modulation(cc, w_mod, b_mod):
    depth, d, n = w_mod.shape
    tn = 1536
    return pl.pallas_call(
        _mod_kernel,
        out_shape=jax.ShapeDtypeStruct((depth, 8, n), F32),
        grid=(depth, n // tn),
        in_specs=[pl.BlockSpec((8, d), lambda l, j: (0, 0)),
                  pl.BlockSpec((1, d, tn), lambda l, j: (l, 0, j)),
                  pl.BlockSpec((1, 1, tn), lambda l, j: (l, 0, j))],
        out_specs=pl.BlockSpec((1, 8, tn), lambda l, j: (l, 0, j)),
        compiler_params=_cparams(2, VMEM_LIMIT),
        name="modulation",
    )(cc, w_mod, b_mod.reshape(depth, 1, n))


def _norm_matmul_kernel(h_ref, g_ref, mod_ref, w_ref, o_ref, *, shift_idx, scale_idx):
    xn = _norm_mod(h_ref[...], g_ref[...],
                   mod_ref[0, shift_idx:shift_idx + 1, :], mod_ref[0, scale_idx:scale_idx + 1, :])
    o_ref[...] = jnp.dot(xn.astype(BF16), w_ref[...], preferred_element_type=F32).astype(o_ref.dtype)


def norm_matmul(h, gain, modtab, w, *, shift_idx, scale_idx, mod_index, out_dtype):
    t, d = h.shape
    n = w.shape[1]
    kern = functools.partial(_norm_matmul_kernel, shift_idx=shift_idx, scale_idx=scale_idx)
    return pl.pallas_call(
        kern,
        out_shape=jax.ShapeDtypeStruct((t, n), out_dtype),
        grid=(t // TM,),
        in_specs=[pl.BlockSpec((TM, d), lambda i: (i, 0)),
                  pl.BlockSpec((1, d), lambda i: (0, 0)),
                  pl.BlockSpec((1, 6, d), lambda i: (mod_index(i), 0, 0)),
                  pl.BlockSpec((d, n), lambda i: (0, 0))],
        out_specs=pl.BlockSpec((TM, n), lambda i: (i, 0)),
        compiler_params=_cparams(1, VMEM_LIMIT),
        name="norm_matmul",
    )(h, gain.reshape(1, d), modtab, w)


def _resid_matmul_kernel(*refs, n_pairs, gate_idx):
    h_ref, mod_ref = refs[0], refs[1]
    a_refs = refs[2:2 + n_pairs]
    w_refs = refs[2 + n_pairs:2 + 2 * n_pairs]
    o_ref = refs[2 + 2 * n_pairs]
    acc = jnp.dot(a_refs[0][...], w_refs[0][...], preferred_element_type=F32)
    for a_ref, w_ref in zip(a_refs[1:], w_refs[1:]):
        acc = acc + jnp.dot(a_ref[...], w_ref[...], preferred_element_type=F32)
    o_ref[...] = h_ref[...] + mod_ref[0, gate_idx:gate_idx + 1, :] * acc


def resid_matmul(h, modtab, a_list, w_list, *, gate_idx, n_blocks, h_block, mod_index):
    d = h.shape[1]
    n_pairs = len(a_list)
    kern = functools.partial(_resid_matmul_kernel, n_pairs=n_pairs, gate_idx=gate_idx)
    in_specs = [pl.BlockSpec((TM, d), lambda i: (h_block(i), 0)),
                pl.BlockSpec((1, 6, d), lambda i: (mod_index(i), 0, 0))]
    in_specs += [pl.BlockSpec((TM, a.shape[1]), lambda i: (i, 0)) for a in a_list]
    in_specs += [pl.BlockSpec(w.shape, lambda i: (0, 0)) for w in w_list]
    return pl.pallas_call(
        kern,
        out_shape=jax.ShapeDtypeStruct((n_blocks * TM, d), F32),
        grid=(n_blocks,),
        in_specs=in_specs,
        out_specs=pl.BlockSpec((TM, d), lambda i: (i, 0)),
        compiler_params=_cparams(1, VMEM_LIMIT),
        name="resid_matmul",
    )(h, modtab, *a_list, *w_list)


def _dwconv_kernel(cur_ref, prev_ref, next_ref, w_ref, b_ref, o_ref, ext_ref, *, chunks_per_batch, ctx_chunks):
    c = pl.program_id(0) % chunks_per_batch
    is_start = jnp.logical_or(c == 0, c == ctx_chunks)
    is_end = jnp.logical_or(c == ctx_chunks - 1, c == chunks_per_batch - 1)
    half = SSM_CONV_W // 2
    hr = HALO_ROWS
    ext_ref[0:hr, :] = jnp.where(is_start, 0.0, prev_ref[...].astype(F32))
    ext_ref[hr:hr + SSD_CHUNK, :] = cur_ref[...].astype(F32)
    ext_ref[hr + SSD_CHUNK:2 * hr + SSD_CHUNK, :] = jnp.where(is_end, 0.0, next_ref[...].astype(F32))
    acc = b_ref[...] + w_ref[0:1, :] * ext_ref[hr - half:hr - half + SSD_CHUNK, :]
    for tap in range(1, SSM_CONV_W):
        lo = hr - half + tap
        acc = acc + w_ref[tap:tap + 1, :] * ext_ref[lo:lo + SSD_CHUNK, :]
    o_ref[...] = _silu(acc).astype(o_ref.dtype)


def dwconv_silu(p, col_block, width, w, b, *, chunks_per_batch, ctx_chunks):
    t = p.shape[0]
    n_chunks = t // SSD_CHUNK
    sub = SSD_CHUNK // HALO_ROWS
    kern = functools.partial(_dwconv_kernel, chunks_per_batch=chunks_per_batch, ctx_chunks=ctx_chunks)
    return pl.pallas_call(
        kern,
        out_shape=jax.ShapeDtypeStruct((t, width), BF16),
        grid=(n_chunks,),
        in_specs=[pl.BlockSpec((SSD_CHUNK, width), lambda i: (i, col_block)),
                  pl.BlockSpec((HALO_ROWS, width), lambda i: (jnp.maximum(i * sub - 1, 0), col_block)),
                  pl.BlockSpec((HALO_ROWS, width),
                               lambda i: (jnp.minimum((i + 1) * sub, t // HALO_ROWS - 1), col_block)),
                  pl.BlockSpec((SSM_CONV_W, width), lambda i: (0, 0)),
                  pl.BlockSpec((1, width), lambda i: (0, 0))],
        out_specs=pl.BlockSpec((SSD_CHUNK, width), lambda i: (i, 0)),
        scratch_shapes=[pltpu.VMEM((SSD_CHUNK + 2 * HALO_ROWS, width), F32)],
        compiler_params=_cparams(1),
        name="dwconv_silu",
    )(p, p, p, w, b.reshape(1, width))


def _ssd_kernel(*refs, reverse, final):
    if final:
        (x_ref, bc_ref, dt_ref, dtb_ref, a_ref, z_ref, yprev_ref, dskip_ref, ng_ref, y_ref, state_ref) = refs
    else:
        (x_ref, bc_ref, dt_ref, dtb_ref, a_ref, y_ref, state_ref) = refs
    ch = SSD_CHUNK
    n = SSM_STATE
    inner = SSM_INNER
    gw = inner // SSM_GROUPS
    col0 = SSM_HEADS if reverse else 0

    @pl.when(pl.program_id(1) == 0)
    def _():
        state_ref[...] = jnp.zeros_like(state_ref)

    x = x_ref[...].astype(F32)
    bc = bc_ref[...]
    raw = dt_ref[...] + dtb_ref[...]
    dt = jnp.maximum(raw, 0.0) + jnp.log(1.0 + jnp.exp(-jnp.abs(raw)))
    adt = dt * a_ref[...]

    ri = lax.broadcasted_iota(I32, (ch, ch), 0)
    ci = lax.broadcasted_iota(I32, (ch, ch), 1)
    if reverse:
        tri = ci >= ri
        tri_t = ri >= ci
    else:
        tri = ci <= ri
        tri_t = ri <= ci
    acum = _dot_01_left(tri.astype(BF16), adt)
    acum_t = _dot_01_right(adt.T, tri_t.astype(BF16))
    last = 0 if reverse else ch - 1
    total = acum[last:last + 1, :]

    hrow = lax.broadcasted_iota(I32, (LANES, inner), 0)
    hcol = lax.broadcasted_iota(I32, (LANES, inner), 1) // SSM_HEAD_DIM
    sel = (hrow == hcol + col0).astype(BF16)
    stack = jnp.concatenate([dt, jnp.exp(acum), jnp.exp(total - acum),
                             jnp.broadcast_to(jnp.exp(total), (8, LANES))], axis=0)
    expanded = _dot_01_right(stack, sel)
    dt_x = expanded[0:ch]
    ea_x = expanded[ch:2 * ch]
    ds_x = expanded[2 * ch:3 * ch]
    tot_x = expanded[3 * ch:3 * ch + 1]

    xd = x * dt_x
    xw = (xd * ds_x).astype(BF16)
    xd_b = xd.astype(BF16)
    state = state_ref[...]
    lane = lax.broadcasted_iota(I32, (ch, LANES), 1)
    first_half = lane < SSM_HEAD_DIM
    y_parts = []

    for g in range(SSM_GROUPS):
        b_g = bc[:, g * n:(g + 1) * n]
        c_g = bc[:, (SSM_GROUPS + g) * n:(SSM_GROUPS + g + 1) * n]
        cb = lax.dot_general(c_g, b_g, (((1,), (1,)), ((), ())), preferred_element_type=F32)
        gs = slice(g * gw, (g + 1) * gw)
        y_off = jnp.dot(c_g, state[:, gs].astype(BF16), preferred_element_type=F32) * ea_x[:, gs]
        b_t = b_g.astype(F32).T.astype(BF16)
        new_state = tot_x[:, gs] * state[:, gs] + jnp.dot(b_t, xw[:, gs], preferred_element_type=F32)
        state_ref[:, gs] = new_state
        for pair in range(gw // LANES):
            lanes = slice(g * gw + pair * LANES, g * gw + (pair + 1) * LANES)
            halves = []
            for sub in range(2):
                head = (g * gw + pair * LANES) // SSM_HEAD_DIM + sub
                cidx = col0 + head
                diff = acum[:, cidx:cidx + 1] - acum_t[cidx:cidx + 1, :]
                m_h = (cb * jnp.where(tri, jnp.exp(jnp.where(tri, diff, 0.0)), 0.0)).astype(BF16)
                halves.append(jnp.dot(m_h, xd_b[:, lanes], preferred_element_type=F32))
            y_parts.append(jnp.where(first_half, halves[0], halves[1]) + y_off[:, pair * LANES:(pair + 1) * LANES])

    y = jnp.concatenate(y_parts, axis=1)
    if final:
        y = y + yprev_ref[...] + x * dskip_ref[...]
        y = y * _silu(z_ref[...].astype(F32))
        ms = jnp.mean(y * y, axis=-1, keepdims=True)
        y = (y * lax.rsqrt(ms + NORM_EPS)) * ng_ref[...]
    y_ref[...] = y.astype(y_ref.dtype)


def ssd_scan(xc, bcc, dtr, dtb, a_row, *, batch, chunks_per_batch, ctx_chunks, reverse,
             z_src=None, z_col=None, y_prev=None, d_skip=None, norm_g=None):
    t, inner = xc.shape
    final = y_prev is not None
    npb = chunks_per_batch

    def chunk(b, s):
        if not reverse:
            c = s
        else:
            c = jnp.where(s < ctx_chunks, ctx_chunks - 1 - s, npb - 1 + ctx_chunks - s)
        return b * npb + c

    row = lambda b, s: (chunk(b, s), 0)
    const = lambda b, s: (0, 0)
    in_specs = [pl.BlockSpec((SSD_CHUNK, inner), row),
                pl.BlockSpec((SSD_CHUNK, bcc.shape[1]), row),
                pl.BlockSpec((SSD_CHUNK, LANES), row),
                pl.BlockSpec((1, LANES), const),
                pl.BlockSpec((1, LANES), const)]
    args = [xc, bcc, dtr, dtb, a_row]
    if final:
        in_specs += [pl.BlockSpec((SSD_CHUNK, inner), lambda b, s: (chunk(b, s), z_col)),
                     pl.BlockSpec((SSD_CHUNK, inner), row),
                     pl.BlockSpec((1, inner), const),
                     pl.BlockSpec((1, inner), const)]
        args += [z_src, y_prev, d_skip, norm_g]
    kern = functools.partial(_ssd_kernel, reverse=reverse, final=final)
    return pl.pallas_call(
        kern,
        out_shape=jax.ShapeDtypeStruct((t, inner), BF16 if final else F32),
        grid=(batch, npb),
        in_specs=in_specs,
        out_specs=pl.BlockSpec((SSD_CHUNK, inner), row),
        scratch_shapes=[pltpu.VMEM((SSM_STATE, inner), F32)],
        compiler_params=_cparams(2, VMEM_LIMIT),
        name="ssd_bwd" if reverse else "ssd_fwd",
    )(*args)


def _na_window_base(t, lat_rows):
    return jnp.minimum(jnp.clip(2 * t - NA_ROWS // 2, 0, lat_rows - NA_ROWS), lat_rows - NA_WIN_ROWS)


def _na_pair_config(t, lat_rows):
    clip = lambda v, lo, hi: max(lo, min(v, hi))
    starts = [clip(r - NA_ROWS // 2, 0, lat_rows - NA_ROWS) for r in (2 * t, 2 * t + 1)]
    base = min(starts[0], lat_rows - NA_WIN_ROWS)
    return (starts[0] - base, starts[1] - base, 2 * t - starts[0], 2 * t + 1 - starts[1])


NA_CONFIGS = ((0, 1, 4, 4), (0, 0, 0, 1), (0, 0, 2, 3), (2, 2, 4, 5), (2, 2, 6, 7), (99, 99, 0, 0))


def _na_config_index(t, n_pairs):
    return jnp.where(t < 0, 5, jnp.where(t == 0, 1, jnp.where(t == 1, 2, jnp.where(
        t == n_pairs - 2, 3, jnp.where(t == n_pairs - 1, 4, 0)))))


def _na_bias_kernel(rpb_ref, o_ref, toe_ref):
    head = pl.program_id(0)
    w = GRID_W
    ki = lax.broadcasted_iota(I32, (w, w), 0)
    qi = lax.broadcasted_iota(I32, (w, w), 1)
    cstart = jnp.clip(qi - NA_COLS // 2, 0, w - NA_COLS)
    col_ok = jnp.logical_and(ki >= cstart, ki < cstart + NA_COLS)
    col_idx = jnp.clip(ki - qi + NA_COLS - 1, 0, 2 * NA_COLS - 2)
    for rho in range(2 * NA_ROWS - 1):
        tile = jnp.zeros((w, w), F32)
        for tcol in range(2 * NA_COLS - 1):
            tile = jnp.where(col_idx == tcol, rpb_ref[head, rho, tcol], tile)
        toe_ref[rho] = jnp.where(col_ok, tile, NEG)
    masked = jnp.full((w, w), NEG, F32)
    for cfg, offsets in enumerate(NA_CONFIGS):
        for j in range(NA_WIN_ROWS):
            for qr in range(NA_PAIR):
                rel = j - offsets[qr]
                inside = 0 <= rel < NA_ROWS
                tile = toe_ref[rel - offsets[NA_PAIR + qr] + NA_ROWS - 1] if inside else masked
                o_ref[cfg, 0, j * w:(j + 1) * w, qr * w:(qr + 1) * w] = tile


def na_bias_tables(rpb, lat_rows):
    heads = rpb.shape[0]
    n_pairs = lat_rows // NA_PAIR
    assert lat_rows % NA_PAIR == 0 and n_pairs >= 5
    for t in range(n_pairs):
        idx = 1 if t == 0 else 2 if t == 1 else 3 if t == n_pairs - 2 else 4 if t == n_pairs - 1 else 0
        assert _na_pair_config(t, lat_rows) == NA_CONFIGS[idx]
    n_cfg = len(NA_CONFIGS)
    return pl.pallas_call(
        _na_bias_kernel,
        out_shape=jax.ShapeDtypeStruct((n_cfg, heads, NA_WIN_ROWS * GRID_W, NA_PAIR * GRID_W), F32),
        grid=(heads,),
        in_specs=[pl.BlockSpec(memory_space=pltpu.SMEM)],
        out_specs=pl.BlockSpec((n_cfg, 1, NA_WIN_ROWS * GRID_W, NA_PAIR * GRID_W), lambda h: (0, h, 0, 0)),
        scratch_shapes=[pltpu.VMEM((2 * NA_ROWS - 1, GRID_W, GRID_W), F32)],
        compiler_params=_cparams(1),
        name="na_bias_tables",
    )(rpb)


def _transpose_kernel(x_ref, o_ref):
    for tile in range(x_ref.shape[1] // LANES):
        lanes = slice(tile * LANES, (tile + 1) * LANES)
        o_ref[lanes, :] = x_ref[:, lanes].astype(F32).T.astype(o_ref.dtype)


def transpose_cols(p, col_block, width):
    t = p.shape[0]
    return pl.pallas_call(
        _transpose_kernel,
        out_shape=jax.ShapeDtypeStruct((width, t), p.dtype),
        grid=(t // TM,),
        in_specs=[pl.BlockSpec((TM, width), lambda i: (i, col_block))],
        out_specs=pl.BlockSpec((width, TM), lambda i: (0, i)),
        compiler_params=_cparams(1),
        name="transpose_cols",
    )(p)


def _na_kernel(*refs):
    nb = NA_WIN_ROWS * GRID_W // LANES
    q_ref = refs[0]
    k_refs = refs[1:1 + nb]
    kc_ref = refs[1 + nb]
    vt_refs = refs[2 + nb:2 + 2 * nb]
    vct_ref, bias_ref, o_ref, s_ref, p_ref = refs[2 + 2 * nb:]
    nq = q_ref.shape[0]
    n_lat = NA_WIN_ROWS * GRID_W
    n_keys = n_lat + kc_ref.shape[0]
    scale = HEAD_DIM ** -0.5
    top = lax.broadcasted_iota(I32, (LANES, nq), 0) < HEAD_DIM
    for pair in range(NA_HEADS // 2):
        lanes = slice(pair * LANES, (pair + 1) * LANES)
        qt = (q_ref[:, lanes] * scale).astype(F32).T
        kk = jnp.concatenate([r[:, lanes] for r in k_refs] + [kc_ref[:, lanes]], axis=0)
        for sub in range(2):
            keep = top if sub == 0 else jnp.logical_not(top)
            qz = jnp.where(keep, qt, 0.0).astype(BF16)
            s_ref[2 * pair + sub] = jnp.dot(kk, qz, preferred_element_type=F32)
    for head in range(NA_HEADS):
        s_lat = s_ref[head, 0:n_lat, :] + bias_ref[0, head]
        s_ctx = s_ref[head, n_lat:, :]
        m = jnp.maximum(jnp.max(s_lat, axis=0, keepdims=True), jnp.max(s_ctx, axis=0, keepdims=True))
        p_ref[head, 0:n_lat, :] = jnp.exp(s_lat - m).astype(BF16)
        p_ref[head, n_lat:, :] = jnp.exp(s_ctx - m).astype(BF16)
    ones = jnp.ones((ONES_ROWS, n_keys), BF16)
    for pair in range(NA_HEADS // 2):
        halves = []
        for sub in range(2):
            head = 2 * pair + sub
            rows = slice(head * HEAD_DIM, (head + 1) * HEAD_DIM)
            vt = jnp.concatenate([r[rows, :] for r in vt_refs] + [vct_ref[rows, :]], axis=1)
            pv = jnp.dot(jnp.concatenate([vt, ones], axis=0), p_ref[head], preferred_element_type=F32)
            halves.append(pv[0:HEAD_DIM] * (1.0 / pv[HEAD_DIM:HEAD_DIM + 1]))
        o_t = jnp.concatenate(halves, axis=0)
        o_ref[:, pair * LANES:(pair + 1) * LANES] = o_t.T.astype(o_ref.dtype)


def na_attention(p, vt_all, bias, *, batch, lat_rows, ctx_len, q_col, k_col):
    t = p.shape[0]
    width = NA_HEADS * HEAD_DIM
    nq = NA_PAIR * GRID_W
    assert nq == LANES and ctx_len % nq == 0
    nb = NA_WIN_ROWS * GRID_W // LANES
    ctx_steps = ctx_len // nq
    n_pairs = lat_rows // NA_PAIR
    steps = ctx_steps + n_pairs
    blocks_ctx = steps * nq // ctx_len

    def key_block(b, st, j):
        base = _na_window_base(st - ctx_steps, lat_rows)
        return b * steps + ctx_steps + base // NA_PAIR + j

    in_specs = [pl.BlockSpec((nq, width), lambda b, st: (b * steps + st, q_col))]
    in_specs += [pl.BlockSpec((LANES, width), functools.partial(lambda b, st, j: (key_block(b, st, j), k_col), j=j))
                 for j in range(nb)]
    in_specs += [pl.BlockSpec((ctx_len, width), lambda b, st: (b * blocks_ctx, k_col))]
    in_specs += [pl.BlockSpec((width, LANES), functools.partial(lambda b, st, j: (0, key_block(b, st, j)), j=j))
                 for j in range(nb)]
    in_specs += [pl.BlockSpec((width, ctx_len), lambda b, st: (0, b * blocks_ctx)),
                 pl.BlockSpec((1, NA_HEADS, NA_WIN_ROWS * GRID_W, nq),
                              lambda b, st: (_na_config_index(st - ctx_steps, n_pairs), 0, 0, 0))]
    args = [p] * (2 + nb) + [vt_all] * (1 + nb) + [bias]
    n_keys = NA_WIN_ROWS * GRID_W + ctx_len
    return pl.pallas_call(
        _na_kernel,
        out_shape=jax.ShapeDtypeStruct((t, width), BF16),
        grid=(batch, steps),
        in_specs=in_specs,
        out_specs=pl.BlockSpec((nq, width), lambda b, st: (b * steps + st, 0)),
        scratch_shapes=[pltpu.VMEM((NA_HEADS, n_keys, nq), F32),
                        pltpu.VMEM((NA_HEADS, n_keys, nq), BF16)],
        compiler_params=_cparams(2, VMEM_LIMIT),
        name="na_attention",
    )(*args)


def _group_mean_sq(x):
    gi = lax.broadcasted_iota(I32, (LANES, LANES), 0) // HEAD_DIM
    gj = lax.broadcasted_iota(I32, (LANES, LANES), 1) // HEAD_DIM
    ones = (gi == gj).astype(BF16)
    return _dot_01_right(x * x, ones) * (1.0 / HEAD_DIM)


def _rope(x, cos_f, sin_s):
    lane = lax.broadcasted_iota(I32, x.shape, 1)
    low = (lane % HEAD_DIM) < HEAD_DIM // 2
    partner = jnp.where(low, pltpu.roll(x, LANES - HEAD_DIM // 2, 1), pltpu.roll(x, HEAD_DIM // 2, 1))
    return x * cos_f + partner * sin_s


def _gqa_prep_kernel(p_ref, cos_ref, sin_ref, qg_ref, kg_ref, qt_ref, k_ref, vt_ref, *, ctx_blocks):
    j = pl.program_id(1)
    is_lat = j >= ctx_blocks
    q_w = GQA_HEADS * HEAD_DIM
    kv_w = GQA_KV_HEADS * HEAD_DIM
    cos_f = jnp.where(is_lat, cos_ref[...], 1.0)
    sin_s = jnp.where(is_lat, sin_ref[...], 0.0)

    def normed(cols, gain):
        x = p_ref[:, cols].astype(F32)
        y = x * lax.rsqrt(_group_mean_sq(x) + NORM_EPS)
        return _rope(y * gain, cos_f, sin_s)

    for tile in range(kv_w // LANES):
        cols = slice(q_w + tile * LANES, q_w + (tile + 1) * LANES)
        k_ref[:, tile * LANES:(tile + 1) * LANES] = normed(cols, kg_ref[...]).astype(k_ref.dtype)
    v = p_ref[:, q_w + kv_w:q_w + 2 * kv_w].astype(F32)
    vt_ref[0] = v.T.astype(vt_ref.dtype)

    @pl.when(is_lat)
    def _():
        scale = HEAD_DIM ** -0.5 * math.log2(math.e)
        for tile in range(q_w // LANES):
            cols = slice(tile * LANES, (tile + 1) * LANES)
            q = normed(cols, qg_ref[...]) * scale
            qt_ref[0, cols, :] = q.T.astype(qt_ref.dtype)


def gqa_prep(p1, cos_f, sin_s, qg, kg, *, batch, blocks_per_batch, ctx_blocks):
    t, n = p1.shape
    q_w = GQA_HEADS * HEAD_DIM
    kv_w = GQA_KV_HEADS * HEAD_DIM
    lb = blocks_per_batch * TM
    lat = (blocks_per_batch - ctx_blocks) * TM
    kern = functools.partial(_gqa_prep_kernel, ctx_blocks=ctx_blocks)
    lat_blk = lambda b, j: jnp.maximum(j - ctx_blocks, 0)
    return pl.pallas_call(
        kern,
        out_shape=(jax.ShapeDtypeStruct((batch, q_w, lat), BF16),
                   jax.ShapeDtypeStruct((t, kv_w), BF16),
                   jax.ShapeDtypeStruct((batch, kv_w, lb), BF16)),
        grid=(batch, blocks_per_batch),
        in_specs=[pl.BlockSpec((TM, n), lambda b, j: (b * blocks_per_batch + j, 0)),
                  pl.BlockSpec((TM, LANES), lambda b, j: (lat_blk(b, j), 0)),
                  pl.BlockSpec((TM, LANES), lambda b, j: (lat_blk(b, j), 0)),
                  pl.BlockSpec((1, LANES), lambda b, j: (0, 0)),
                  pl.BlockSpec((1, LANES), lambda b, j: (0, 0))],
        out_specs=(pl.BlockSpec((1, q_w, TM), lambda b, j: (b, 0, lat_blk(b, j))),
                   pl.BlockSpec((TM, kv_w), lambda b, j: (b * blocks_per_batch + j, 0)),
                   pl.BlockSpec((1, kv_w, TM), lambda b, j: (b, 0, j))),
        compiler_params=_cparams(2, VMEM_LIMIT),
        name="gqa_prep",
    )(p1, cos_f, sin_s, qg, kg)


def _gqa_attn_kernel(qt_ref, k_ref, vt_ref, o_ref, qz_ref, s_ref, bmax_ref, m_ref, l_ref, acc_ref, *, tk, n_kblocks):
    kv = pl.program_id(1)
    tq = qt_ref.shape[2]
    group = GQA_HEADS // GQA_KV_HEADS
    second = (kv % 2) == 1
    for hq in range(group):
        q_h = qt_ref[0, hq * HEAD_DIM:(hq + 1) * HEAD_DIM, :]
        zero = jnp.zeros_like(q_h)
        qz_ref[0:HEAD_DIM, hq * tq:(hq + 1) * tq] = jnp.where(second, zero, q_h)
        qz_ref[HEAD_DIM:2 * HEAD_DIM, hq * tq:(hq + 1) * tq] = jnp.where(second, q_h, zero)
    m_ref[...] = jnp.full_like(m_ref, NEG)
    l_ref[...] = jnp.zeros_like(l_ref)
    acc_ref[...] = jnp.zeros_like(acc_ref)

    def scores(kb, slot):
        off = pl.multiple_of(kb * tk, tk)
        s = jnp.dot(k_ref[0, pl.ds(off, tk), :], qz_ref[...], preferred_element_type=F32)
        s_ref[slot] = s
        bmax_ref[slot] = jnp.max(s, axis=0, keepdims=True)

    def update(kb, slot):
        off = pl.multiple_of(kb * tk, tk)
        s = s_ref[slot]
        m_old = m_ref[...]
        m_new = jnp.maximum(m_old, bmax_ref[slot])
        alpha = jnp.exp2(m_old - m_new)
        p = jnp.exp2(s - m_new).astype(BF16)
        v_aug = jnp.concatenate([vt_ref[0, :, pl.ds(off, tk)], jnp.ones((ONES_ROWS, tk), BF16)], axis=0)
        pv = jnp.dot(v_aug, p, preferred_element_type=F32)
        acc_ref[...] = alpha * acc_ref[...] + pv[0:HEAD_DIM]
        l_ref[...] = alpha * l_ref[...] + pv[HEAD_DIM:HEAD_DIM + 1]
        m_ref[...] = m_new

    per_trip = GQA_BLOCKS_PER_TRIP
    scores(0, 0)
    trips = (n_kblocks - 1) // per_trip

    def one_trip(i, carry):
        kb = per_trip * i
        for j in range(per_trip):
            scores(kb + j + 1, (j + 1) % 2)
            update(kb + j, j % 2)
        return carry

    lax.fori_loop(0, trips, one_trip, 0)
    base = trips * per_trip
    for j in range(n_kblocks - base):
        if base + j + 1 < n_kblocks:
            scores(base + j + 1, (j + 1) % 2)
        update(base + j, j % 2)

    inv_l = 1.0 / l_ref[...]
    for pair in range(group // 2):
        lo, hi = 2 * pair * tq, (2 * pair + 1) * tq
        o_t = jnp.concatenate([acc_ref[:, lo:lo + tq] * inv_l[:, lo:lo + tq],
                               acc_ref[:, hi:hi + tq] * inv_l[:, hi:hi + tq]], axis=0)
        o_ref[:, pair * LANES:(pair + 1) * LANES] = o_t.T.astype(o_ref.dtype)


def gqa_attention(qt, k, vt, *, tq=512, tk=256):
    batch, q_w, lat = qt.shape
    lb = k.shape[1]
    group = GQA_HEADS // GQA_KV_HEADS
    gw = group * HEAD_DIM
    tq = min(tq, lat)
    kern = functools.partial(_gqa_attn_kernel, tk=tk, n_kblocks=lb // tk)
    return pl.pallas_call(
        kern,
        out_shape=jax.ShapeDtypeStruct((batch * lat, q_w), BF16),
        grid=(batch, GQA_KV_HEADS, lat // tq),
        in_specs=[pl.BlockSpec((1, gw, tq), lambda b, h, i: (b, h, i)),
                  pl.BlockSpec((1, lb, LANES), lambda b, h, i: (b, 0, h // 2)),
                  pl.BlockSpec((1, HEAD_DIM, lb), lambda b, h, i: (b, h, 0))],
        out_specs=pl.BlockSpec((tq, gw), lambda b, h, i: (b * (lat // tq) + i, h)),
        scratch_shapes=[pltpu.VMEM((2 * HEAD_DIM, group * tq), BF16),
                        pltpu.VMEM((2, tk, group * tq), F32),
                        pltpu.VMEM((2, 1, group * tq), F32),
                        pltpu.VMEM((1, group * tq), F32),
                        pltpu.VMEM((1, group * tq), F32),
                        pltpu.VMEM((HEAD_DIM, group * tq), F32)],
        compiler_params=_cparams(3, VMEM_LIMIT),
        name="gqa_attention",
    )(qt, k, vt)


def _router_kernel(h_ref, g_ref, mod_ref, rw_ref, rb_ref, xn_ref, eidx_ref, wts_ref, rank_ref, cnt_ref, run_ref):
    @pl.when(pl.program_id(0) == 0)
    def _():
        run_ref[...] = jnp.zeros_like(run_ref)

    xn = _norm_mod(h_ref[...], g_ref[...], mod_ref[0, 3:4, :], mod_ref[0, 4:5, :])
    xn_ref[...] = xn
    tm = xn.shape[0]
    logits = lax.dot_general(rw_ref[...], xn, (((1,), (1,)), ((), ())),
                             precision=HIGHEST, preferred_element_type=F32)
    s = jax.nn.sigmoid(logits)
    sb = s + rb_ref[...]
    row = lambda a, e: a[e:e + 1, :]

    best = None
    for g in range(N_GROUPS):
        a = [row(sb, g * EXPERTS_PER_GROUP + j) for j in range(EXPERTS_PER_GROUP)]
        score = a[0] + a[1]
        for i in range(EXPERTS_PER_GROUP):
            for j in range(i + 1, EXPERTS_PER_GROUP):
                if (i, j) != (0, 1):
                    score = jnp.maximum(score, a[i] + a[j])
        if best is None:
            best, grp = score, jnp.zeros((1, tm), I32)
        else:
            better = score > best
            best = jnp.where(better, score, best)
            grp = jnp.where(better, g, grp)

    biased, plain = [], []
    for j in range(EXPERTS_PER_GROUP):
        vb = row(sb, j)
        vp = row(s, j)
        for g in range(1, N_GROUPS):
            pick = grp == g
            vb = jnp.where(pick, row(sb, g * EXPERTS_PER_GROUP + j), vb)
            vp = jnp.where(pick, row(s, g * EXPERTS_PER_GROUP + j), vp)
        biased.append(vb)
        plain.append(vp)

    def argmax_first(vals):
        top, idx = vals[0], jnp.zeros((1, tm), I32)
        for j in range(1, len(vals)):
            better = vals[j] > top
            top = jnp.where(better, vals[j], top)
            idx = jnp.where(better, j, idx)
        return idx

    i1 = argmax_first(biased)
    i2 = argmax_first([jnp.where(i1 == j, -jnp.inf, biased[j]) for j in range(EXPERTS_PER_GROUP)])

    def pick_plain(idx):
        out = plain[0]
        for j in range(1, EXPERTS_PER_GROUP):
            out = jnp.where(idx == j, plain[j], out)
        return out

    w1, w2 = pick_plain(i1), pick_plain(i2)
    wsum = w1 + w2
    e1 = grp * EXPERTS_PER_GROUP + i1
    e2 = grp * EXPERTS_PER_GROUP + i2
    eidx_ref[0] = jnp.concatenate([e1, e2], axis=0)
    wts_ref[...] = jnp.concatenate([w1 / wsum, w2 / wsum], axis=0)

    erow = lax.broadcasted_iota(I32, (N_EXPERTS, tm), 0)
    hit1 = erow == e1
    hit2 = erow == e2
    onehot = jnp.logical_or(hit1, hit2)
    si = lax.broadcasted_iota(I32, (tm, tm), 0)
    ti = lax.broadcasted_iota(I32, (tm, tm), 1)
    before = jnp.dot(onehot.astype(BF16), (si < ti).astype(BF16), preferred_element_type=F32)
    before = before + run_ref[...][:, 0:1]
    r1 = jnp.sum(jnp.where(hit1, before, 0.0), axis=0, keepdims=True)
    r2 = jnp.sum(jnp.where(hit2, before, 0.0), axis=0, keepdims=True)
    rank_ref[0] = jnp.concatenate([r1, r2], axis=0).astype(I32)
    totals = run_ref[...] + jnp.sum(onehot.astype(F32), axis=1, keepdims=True)
    run_ref[...] = totals
    cnt_ref[...] = totals


def moe_router(h, gain, modtab, rw_t, rb, *, mod_index):
    t, d = h.shape
    return pl.pallas_call(
        _router_kernel,
        out_shape=(jax.ShapeDtypeStruct((t, d), F32),
                   jax.ShapeDtypeStruct((t // TM, 2, TM), I32),
                   jax.ShapeDtypeStruct((2, t), F32),
                   jax.ShapeDtypeStruct((t // TM, 2, TM), I32),
                   jax.ShapeDtypeStruct((N_EXPERTS, LANES), F32)),
        grid=(t // TM,),
        in_specs=[pl.BlockSpec((TM, d), lambda i: (i, 0)),
                  pl.BlockSpec((1, d), lambda i: (0, 0)),
                  pl.BlockSpec((1, 6, d), lambda i: (mod_index(i), 0, 0)),
                  pl.BlockSpec((N_EXPERTS, d), lambda i: (0, 0)),
                  pl.BlockSpec((N_EXPERTS, 1), lambda i: (0, 0))],
        out_specs=(pl.BlockSpec((TM, d), lambda i: (i, 0)),
                   pl.BlockSpec((1, 2, TM), lambda i: (i, 0, 0)),
                   pl.BlockSpec((2, TM), lambda i: (0, i)),
                   pl.BlockSpec((1, 2, TM), lambda i: (i, 0, 0)),
                   pl.BlockSpec((N_EXPERTS, LANES), lambda i: (0, 0))),
        scratch_shapes=[pltpu.VMEM((N_EXPERTS, LANES), F32)],
        compiler_params=_cparams(1, VMEM_LIMIT),
        name="moe_router",
    )(h, gain.reshape(1, d), modtab, rw_t, rb)


def _row_copy(src_ref, src_row, dst_ref, dst_row, sem):
    return pltpu.make_async_copy(src_ref.at[pl.ds(src_row, 1)], dst_ref.at[pl.ds(dst_row, 1)], sem)


def _issue_and_drain(rows, make_copy):
    def issue(r, carry):
        for k in range(2):
            make_copy(r, k).start()
        return carry

    lax.fori_loop(0, rows, issue, 0, unroll=8)
    for _ in range(rows):
        for k in range(2):
            make_copy(0, k).wait()


def _dispatch_kernel(eidx_ref, rank_ref, start_ref, x_ref, xs_in_ref, xs_ref, sem):
    del xs_in_ref

    def make_copy(r, k):
        slot = start_ref[eidx_ref[0, k, r]] + rank_ref[0, k, r]
        return _row_copy(x_ref, r, xs_ref, slot, sem)

    _issue_and_drain(x_ref.shape[0], make_copy)


def moe_dispatch(xn, eidx3, rank3, pad_start, n_slots):
    t, d = xn.shape
    xs0 = jnp.zeros((n_slots, d), xn.dtype)
    idx_spec = pl.BlockSpec((1, 2, DMA_ROWS), lambda i: (i, 0, 0), memory_space=pltpu.SMEM)
    return pl.pallas_call(
        _dispatch_kernel,
        out_shape=jax.ShapeDtypeStruct((n_slots, d), xn.dtype),
        grid=(t // DMA_ROWS,),
        in_specs=[idx_spec, idx_spec,
                  pl.BlockSpec(memory_space=pltpu.SMEM),
                  pl.BlockSpec((DMA_ROWS, d), lambda i: (i, 0)),
                  pl.BlockSpec(memory_space=pl.ANY)],
        out_specs=pl.BlockSpec(memory_space=pl.ANY),
        scratch_shapes=[pltpu.SemaphoreType.DMA(())],
        input_output_aliases={4: 0},
        compiler_params=_cparams(1),
        name="moe_dispatch",
    )(eidx3, rank3, pad_start, xn, xs0)


def _expert_kernel(blk_e_ref, nused_ref, x_ref, wg_ref, wu_ref, wd_ref, y_ref):
    del blk_e_ref

    @pl.when(pl.program_id(0) < nused_ref[0])
    def _():
        x = x_ref[...].astype(BF16)
        gate = jnp.dot(x, wg_ref[0, 0].astype(BF16), preferred_element_type=F32)
        up = jnp.dot(x, wu_ref[0, 0].astype(BF16), preferred_element_type=F32)
        hid = (_silu(gate) * up).astype(BF16)
        y_ref[...] = jnp.dot(hid, wd_ref[0, 0].astype(BF16), preferred_element_type=F32)

    @pl.when(pl.program_id(0) >= nused_ref[0])
    def _():
        y_ref[...] = jnp.zeros_like(y_ref)


def moe_experts(xs, blk_e, nused, wg, wu, wd, layer):
    n_slots, d = xs.shape
    de = wg.shape[3]
    grid_spec = pltpu.PrefetchScalarGridSpec(
        num_scalar_prefetch=2,
        grid=(n_slots // MOE_BM,),
        in_specs=[pl.BlockSpec((MOE_BM, d), lambda i, be, nu: (i, 0)),
                  pl.BlockSpec((1, 1, d, de), lambda i, be, nu: (layer, be[i], 0, 0)),
                  pl.BlockSpec((1, 1, d, de), lambda i, be, nu: (layer, be[i], 0, 0)),
                  pl.BlockSpec((1, 1, de, d), lambda i, be, nu: (layer, be[i], 0, 0))],
        out_specs=pl.BlockSpec((MOE_BM, d), lambda i, be, nu: (i, 0)),
    )
    return pl.pallas_call(
        _expert_kernel,
        out_shape=jax.ShapeDtypeStruct((n_slots, d), F32),
        grid_spec=grid_spec,
        compiler_params=_cparams(1, VMEM_LIMIT),
        name="moe_experts",
    )(blk_e, nused, xs, wg, wu, wd)


def _combine_kernel(eidx_ref, rank_ref, eidx_nxt_ref, rank_nxt_ref, start_ref, h_ref, mod_ref, w_ref, fg_ref,
                    yb_ref, o_ref, buf_ref, sems, *, final):
    step = pl.program_id(0)
    rows = h_ref.shape[0]

    def start_gathers(e_ref, r_ref, half):
        def issue(r, carry):
            for k in range(2):
                slot = start_ref[e_ref[0, k, r]] + r_ref[0, k, r]
                _row_copy(yb_ref, slot, buf_ref.at[half, k], r, sems.at[half]).start()
            return carry

        lax.fori_loop(0, rows, issue, 0, unroll=8)

    cur = step % 2

    @pl.when(step == 0)
    def _():
        start_gathers(eidx_ref, rank_ref, 0)

    @pl.when(step + 1 < pl.num_programs(0))
    def _():
        start_gathers(eidx_nxt_ref, rank_nxt_ref, 1 - cur)

    for _ in range(rows):
        for k in range(2):
            _row_copy(yb_ref, 0, buf_ref.at[cur, k], 0, sems.at[cur]).wait()
    f = w_ref[:, 0:1] * buf_ref[cur, 0] + w_ref[:, 1:2] * buf_ref[cur, 1]
    h = h_ref[...] + mod_ref[0, 5:6, :] * f
    if final:
        ms = jnp.mean(h * h, axis=-1, keepdims=True)
        h = (h * lax.rsqrt(ms + NORM_EPS)) * fg_ref[...]
    o_ref[...] = h


def moe_combine(h, modtab, wts_t, final_g, yb, eidx3, rank3, pad_start, *, mod_index, final):
    t, d = h.shape
    kern = functools.partial(_combine_kernel, final=final)
    n_steps = t // DMA_ROWS
    idx_spec = pl.BlockSpec((1, 2, DMA_ROWS), lambda i: (i, 0, 0), memory_space=pltpu.SMEM)
    nxt_spec = pl.BlockSpec((1, 2, DMA_ROWS), lambda i: (jnp.minimum(i + 1, n_steps - 1), 0, 0),
                            memory_space=pltpu.SMEM)
    return pl.pallas_call(
        kern,
        out_shape=jax.ShapeDtypeStruct((t, d), F32),
        grid=(n_steps,),
        in_specs=[idx_spec, idx_spec, nxt_spec, nxt_spec,
                  pl.BlockSpec(memory_space=pltpu.SMEM),
                  pl.BlockSpec((DMA_ROWS, d), lambda i: (i, 0)),
                  pl.BlockSpec((1, 6, d), lambda i: (mod_index(i), 0, 0)),
                  pl.BlockSpec((DMA_ROWS, 2), lambda i: (i, 0)),
                  pl.BlockSpec((1, d), lambda i: (0, 0)),
                  pl.BlockSpec(memory_space=pl.ANY)],
        out_specs=pl.BlockSpec((DMA_ROWS, d), lambda i: (i, 0)),
        scratch_shapes=[pltpu.VMEM((2, 2, DMA_ROWS, d), F32), pltpu.SemaphoreType.DMA((2,))],
        compiler_params=_cparams(1, VMEM_LIMIT),
        name="moe_combine",
    )(eidx3, rank3, eidx3, rank3, pad_start, h, modtab, wts_t, final_g.reshape(1, d), yb)


def moe_layer(h, norm_g, modtab, rw_t, rb, wg, wu, wd, layer, final_g, *, mod_index, final):
    t, d = h.shape
    xn, eidx3, wts, rank3, counts = moe_router(h, norm_g, modtab, rw_t, rb, mod_index=mod_index)
    n_blocks = (2 * t) // MOE_BM + N_EXPERTS
    n_slots = n_blocks * MOE_BM
    cnt = counts[:, 0].astype(I32)
    padded = (cnt + MOE_BM - 1) // MOE_BM * MOE_BM
    pad_end = jnp.cumsum(padded)
    pad_start = (pad_end - padded).astype(I32)
    blk_start = jnp.arange(n_blocks, dtype=I32) * MOE_BM
    blk_e = jnp.minimum(jnp.sum((pad_end[None, :] <= blk_start[:, None]).astype(I32), axis=1), N_EXPERTS - 1)
    nused = (pad_end[-1] // MOE_BM).reshape(1).astype(I32)
    xs = moe_dispatch(xn, eidx3, rank3, pad_start, n_slots)
    yb = moe_experts(xs, blk_e.astype(I32), nused, wg, wu, wd, layer)
    return moe_combine(h, modtab, wts.T, final_g, yb, eidx3, rank3, pad_start, mod_index=mod_index, final=final)


def _rope_tables(lat):
    t = jnp.arange(lat)
    row = (t // GRID_W).astype(F32)
    col = (t % GRID_W).astype(F32)
    n_freq = HEAD_DIM // 4
    inv = ROPE_THETA ** (-jnp.arange(n_freq, dtype=F32) / n_freq)
    ang = jnp.concatenate([row[:, None] * inv, col[:, None] * inv], axis=-1)
    cos, sin = jnp.cos(ang), jnp.sin(ang)
    cos_f = jnp.tile(cos, (1, LANES // (HEAD_DIM // 2)))
    sin_s = jnp.tile(jnp.concatenate([-sin, sin], axis=-1), (1, LANES // HEAD_DIM))
    return cos_f, sin_s


def kernel(x, c, ctx, c_ctx, w_mod, b_mod, norm1_g, norm2_g, final_g, ab_w_in, ab_w_out, ssm_conv_w, ssm_conv_b, ssm_A_log, ssm_dt_bias, ssm_D, ssm_norm_g, na_rpb, gqa_w_in, gqa_w_out, gqa_q_norm, gqa_k_norm, router_w, router_bias, moe_w_gate, moe_w_up, moe_w_down):
    bsz, lat, d = x.shape
    lc = ctx.shape[1]
    lb = lc + lat
    assert lat % TM == 0 and lc % TM == 0 and lat % GRID_W == 0 and lat // GRID_W >= NA_ROWS
    assert bsz + 1 <= 8 and w_mod.shape[0] == 2
    nb = lb // TM
    cb = lc // TM
    nlb = lat // TM

    def mod_comb(i):
        return jnp.where(i % nb < cb, bsz, i // nb)

    def mod_lat(i):
        return i // nlb

    inner, xbc_w = SSM_INNER, SSM_INNER + 2 * SSM_GROUPS * SSM_STATE
    w_in = ab_w_in[0]
    o_dt = inner + xbc_w
    o_q = o_dt + 2 * SSM_HEADS
    w_z, w_x, w_bc = w_in[:, :inner], w_in[:, inner:2 * inner], w_in[:, 2 * inner:o_dt]
    w_dt = w_in[:, o_dt:o_q]
    w_qkv = w_in[:, o_q:]
    w_in_b = jnp.concatenate([w_qkv, w_z, w_x, w_bc], axis=1).astype(BF16)
    w_dt_b = jnp.pad(w_dt, ((0, 0), (0, LANES - 2 * SSM_HEADS))).astype(BF16)
    conv_w, conv_b = ssm_conv_w[0], ssm_conv_b[0]
    pad32 = lambda v: jnp.pad(v.reshape(1, -1), ((0, 0), (0, LANES - 2 * SSM_HEADS)))
    dtb_row = pad32(ssm_dt_bias[0])
    a_row = pad32(-jnp.exp(ssm_A_log[0].astype(F32)))
    d_skip = jnp.repeat(ssm_D[0], SSM_HEAD_DIM).reshape(1, inner)
    w_out = ab_w_out[0].astype(BF16)
    w_gqa = gqa_w_in[0].astype(BF16)
    w_gqa_out = gqa_w_out[0].astype(BF16)
    qg = jnp.tile(gqa_q_norm[0], LANES // HEAD_DIM).reshape(1, LANES)
    kg = jnp.tile(gqa_k_norm[0], LANES // HEAD_DIM).reshape(1, LANES)
    rw_t = router_w.T
    rb = router_bias.reshape(N_EXPERTS, 1)
    wg, wu, wd = moe_w_gate, moe_w_up, moe_w_down

    cc = jnp.concatenate([c, c_ctx[None, :], jnp.zeros((8 - bsz - 1, d), F32)], axis=0)
    mod = modulation(cc, w_mod, b_mod)[:, :bsz + 1].reshape(2, bsz + 1, 6, d)

    h0 = jnp.concatenate([ctx, x], axis=1).reshape(bsz * lb, d)

    p = norm_matmul(h0, norm1_g[0], mod[0], w_in_b, shift_idx=0, scale_idx=1, mod_index=mod_comb, out_dtype=BF16)
    dtr = norm_matmul(h0, norm1_g[0], mod[0], w_dt_b, shift_idx=0, scale_idx=1, mod_index=mod_comb, out_dtype=F32)
    chunks = lb // SSD_CHUNK
    ctx_chunks = lc // SSD_CHUNK
    xc = dwconv_silu(p, 4, inner, conv_w[:, :inner], conv_b[:inner], chunks_per_batch=chunks, ctx_chunks=ctx_chunks)
    bcc = dwconv_silu(p, 10, xbc_w - inner, conv_w[:, inner:], conv_b[inner:],
                      chunks_per_batch=chunks, ctx_chunks=ctx_chunks)
    scan = functools.partial(ssd_scan, xc, bcc, dtr, dtb_row, a_row, batch=bsz,
                             chunks_per_batch=chunks, ctx_chunks=ctx_chunks)
    y_f = scan(reverse=False)
    y_ssd = scan(reverse=True, z_src=p, z_col=3, y_prev=y_f, d_skip=d_skip, norm_g=ssm_norm_g[0].reshape(1, inner))
    bias = na_bias_tables(na_rpb[0], lat // GRID_W)
    vt_all = transpose_cols(p, 2, NA_HEADS * HEAD_DIM)
    o_na = na_attention(p, vt_all, bias, batch=bsz, lat_rows=lat // GRID_W, ctx_len=lc, q_col=0, k_col=1)
    ident = lambda i: i
    h1 = resid_matmul(h0, mod[0], [y_ssd, o_na], [w_out[:inner], w_out[inner:]],
                      gate_idx=2, n_blocks=bsz * nb, h_block=ident, mod_index=mod_comb)
    h2 = moe_layer(h1, norm2_g[0], mod[0], rw_t, rb, wg, wu, wd, 0, final_g, mod_index=mod_comb, final=False)

    p1 = norm_matmul(h2, norm1_g[1], mod[1], w_gqa, shift_idx=0, scale_idx=1, mod_index=mod_comb, out_dtype=BF16)
    cos_f, sin_s = _rope_tables(lat)
    qt, kn, vt = gqa_prep(p1, cos_f, sin_s, qg, kg, batch=bsz, blocks_per_batch=nb, ctx_blocks=cb)
    o = gqa_attention(qt, kn.reshape(bsz, lb, GQA_KV_HEADS * HEAD_DIM), vt)
    lat_block = lambda i: (i // nlb) * nb + cb + i % nlb
    h3 = resid_matmul(h2, mod[1], [o], [w_gqa_out], gate_idx=2, n_blocks=bsz * nlb, h_block=lat_block, mod_index=mod_lat)
    out = moe_layer(h3, norm2_g[1], mod[1], rw_t, rb, wg, wu, wd, 1, final_g, mod_index=mod_lat, final=True)
    return out.reshape(bsz, lat, d)
```

```python
import functools
import math

import jax
import jax.numpy as jnp
from jax import lax
from jax.experimental import pallas as pl
from jax.experimental.pallas import tpu as pltpu

F32 = jnp.float32
BF16 = jnp.bfloat16
I32 = jnp.int32
HIGHEST = lax.Precision.HIGHEST

NORM_EPS = 1e-6
GRID_W = 64
SSM_HEADS = 16
SSM_HEAD_DIM = 64
SSM_INNER = 1024
SSM_STATE = 128
SSM_GROUPS = 2
SSM_CONV_W = 5
SSD_CHUNK = 128
NA_HEADS = 16
NA_ROWS = 8
NA_COLS = 16
NA_PAIR = 2
NA_WIN_ROWS = 10
HEAD_DIM = 64
GQA_HEADS = 16
GQA_KV_HEADS = 4
ROPE_THETA = 10000.0
N_EXPERTS = 32
N_GROUPS = 8
EXPERTS_PER_GROUP = 4
D_EXPERT = 512

LANES = 128
HALO_ROWS = 16
ONES_ROWS = 16
GQA_BLOCKS_PER_TRIP = 4
TM = 256
MOE_BM = 256
DMA_ROWS = TM
NEG = -1e30
VMEM_LIMIT = 48 * 1024 * 1024


def _cparams(n_axes, vmem=None):
    return pltpu.CompilerParams(dimension_semantics=("arbitrary",) * n_axes,
                                vmem_limit_bytes=vmem)


def _silu(x):
    return x * jax.nn.sigmoid(x)


def _split3(v):
    hi = v.astype(BF16)
    r1 = v - hi.astype(F32)
    mid = r1.astype(BF16)
    lo = (r1 - mid.astype(F32)).astype(BF16)
    return hi, mid, lo


def _dot_01_right(v, m01):
    lhs = jnp.concatenate(_split3(v), axis=1)
    rhs = jnp.concatenate([m01, m01, m01], axis=0)
    return jnp.dot(lhs, rhs, preferred_element_type=F32)


def _dot_01_left(m01, v):
    lhs = jnp.concatenate([m01, m01, m01], axis=1)
    rhs = jnp.concatenate(_split3(v), axis=0)
    return jnp.dot(lhs, rhs, preferred_element_type=F32)


HI_HALF = -65536


def _pack_bf16_pairs(x):
    n = x.shape[1] // 2
    lo = lax.bitcast_convert_type(x[:, :n].astype(BF16).astype(F32), I32)
    hi = lax.bitcast_convert_type(x[:, n:].astype(BF16).astype(F32), I32)
    return lax.shift_right_logical(lo, 16) | (hi & HI_HALF)


def _unpack_bf16_pairs(u):
    lo = lax.bitcast_convert_type(lax.shift_left(u, 16), F32)
    hi = lax.bitcast_convert_type(u & HI_HALF, F32)
    return jnp.concatenate([lo, hi], axis=1)


def _norm_mod(x, g, shift, scale):
    ms = jnp.mean(x * x, axis=-1, keepdims=True)
    y = x * lax.rsqrt(ms + NORM_EPS)
    return (y * g) * (1.0 + scale) + shift


def _mod_kernel(c_ref, w_ref, b_ref, o_ref):
    a = _silu(c_ref[...]).astype(BF16)
    o_ref[0] = jnp.dot(a, w_ref[0].astype(BF16), preferred_element_type=F32) + b_ref[0]


def modulation(cc, w_mod, b_mod):
    depth, d, n = w_mod.shape
    tn = 1536
    return pl.pallas_call(
        _mod_kernel,
        out_shape=jax.ShapeDtypeStruct((depth, 8, n), F32),
        grid=(depth, n // tn),
        in_specs=[pl.BlockSpec((8, d), lambda l, j: (0, 0)),
                  pl.BlockSpec((1, d, tn), lambda l, j: (l, 0, j)),
                  pl.BlockSpec((1, 1, tn), lambda l, j: (l, 0, j))],
        out_specs=pl.BlockSpec((1, 8, tn), lambda l, j: (l, 0, j)),
        compiler_params=_cparams(2, VMEM_LIMIT),
        name="modulation",
    )(cc, w_mod, b_mod.reshape(depth, 1, n))


def _norm_matmul_kernel(h_ref, g_ref, mod_ref, *refs, shift_idx, scale_idx):
    n_w = len(refs) // 2
    xn = _norm_mod(h_ref[...], g_ref[...],
                   mod_ref[0, shift_idx:shift_idx + 1, :], mod_ref[0, scale_idx:scale_idx + 1, :]).astype(BF16)
    for w_ref, o_ref in zip(refs[:n_w], refs[n_w:]):
        o_ref[...] = jnp.dot(xn, w_ref[...], preferred_element_type=F32).astype(o_ref.dtype)


def norm_matmul(h, gain, modtab, weights, out_dtypes, *, shift_idx, scale_idx, mod_index):
    t, d = h.shape
    kern = functools.partial(_norm_matmul_kernel, shift_idx=shift_idx, scale_idx=scale_idx)
    outs = pl.pallas_call(
        kern,
        out_shape=tuple(jax.ShapeDtypeStruct((t, w.shape[1]), dt) for w, dt in zip(weights, out_dtypes)),
        grid=(t // TM,),
        in_specs=[pl.BlockSpec((TM, d), lambda i: (i, 0)),
                  pl.BlockSpec((1, d), lambda i: (0, 0)),
                  pl.BlockSpec((1, 6, d), lambda i: (mod_index(i), 0, 0))]
                 + [pl.BlockSpec(w.shape, lambda i: (0, 0)) for w in weights],
        out_specs=tuple(pl.BlockSpec((TM, w.shape[1]), lambda i: (i, 0)) for w in weights),
        compiler_params=_cparams(1, VMEM_LIMIT),
        name="norm_matmul",
    )(h, gain.reshape(1, d), modtab, *weights)
    return outs


def _resid_matmul_kernel(*refs, n_pairs, gate_idx):
    h_ref, mod_ref = refs[0], refs[1]
    a_refs = refs[2:2 + n_pairs]
    w_refs = refs[2 + n_pairs:2 + 2 * n_pairs]
    o_ref = refs[2 + 2 * n_pairs]
    acc = jnp.dot(a_refs[0][...], w_refs[0][...], preferred_element_type=F32)
    for a_ref, w_ref in zip(a_refs[1:], w_refs[1:]):
        acc = acc + jnp.dot(a_ref[...], w_ref[...], preferred_element_type=F32)
    o_ref[...] = h_ref[...] + mod_ref[0, gate_idx:gate_idx + 1, :] * acc


def resid_matmul(h, modtab, a_list, w_list, *, gate_idx, n_blocks, h_block, mod_index):
    d = h.shape[1]
    n_pairs = len(a_list)
    kern = functools.partial(_resid_matmul_kernel, n_pairs=n_pairs, gate_idx=gate_idx)
    in_specs = [pl.BlockSpec((TM, d), lambda i: (h_block(i), 0)),
                pl.BlockSpec((1, 6, d), lambda i: (mod_index(i), 0, 0))]
    in_specs += [pl.BlockSpec((TM, a.shape[1]), lambda i: (i, 0)) for a in a_list]
    in_specs += [pl.BlockSpec(w.shape, lambda i: (0, 0)) for w in w_list]
    return pl.pallas_call(
        kern,
        out_shape=jax.ShapeDtypeStruct((n_blocks * TM, d), F32),
        grid=(n_blocks,),
        in_specs=in_specs,
        out_specs=pl.BlockSpec((TM, d), lambda i: (i, 0)),
        compiler_params=_cparams(1, VMEM_LIMIT),
        name="resid_matmul",
    )(h, modtab, *a_list, *w_list)


def _dwconv_kernel(cur_ref, prev_ref, next_ref, w_ref, b_ref, o_ref, *, chunks_per_batch, ctx_chunks):
    c = pl.program_id(0) % chunks_per_batch
    is_start = jnp.logical_or(c == 0, c == ctx_chunks)
    is_end = jnp.logical_or(c == ctx_chunks - 1, c == chunks_per_batch - 1)
    half = SSM_CONV_W // 2
    hr = HALO_ROWS
    cur = cur_ref[...]
    prev = jnp.where(is_start, jnp.zeros_like(prev_ref[...]), prev_ref[...])
    nxt = jnp.where(is_end, jnp.zeros_like(next_ref[...]), next_ref[...])
    ext = jnp.concatenate([prev, cur, nxt], axis=0)
    ri = lax.broadcasted_iota(I32, (SSD_CHUNK, SSD_CHUNK + 2 * hr), 0)
    ci = lax.broadcasted_iota(I32, (SSD_CHUNK, SSD_CHUNK + 2 * hr), 1)
    acc = b_ref[...] + w_ref[half:half + 1, :] * cur.astype(F32)
    for tap in range(SSM_CONV_W):
        if tap != half:
            shift = (ci == ri + (hr - half + tap)).astype(BF16)
            acc = acc + w_ref[tap:tap + 1, :] * jnp.dot(shift, ext, preferred_element_type=F32)
    o_ref[...] = _silu(acc).astype(o_ref.dtype)


def dwconv_silu(p, col_block, width, w, b, *, chunks_per_batch, ctx_chunks):
    t = p.shape[0]
    n_chunks = t // SSD_CHUNK
    sub = SSD_CHUNK // HALO_ROWS
    kern = functools.partial(_dwconv_kernel, chunks_per_batch=chunks_per_batch, ctx_chunks=ctx_chunks)
    return pl.pallas_call(
        kern,
        out_shape=jax.ShapeDtypeStruct((t, width), BF16),
        grid=(n_chunks,),
        in_specs=[pl.BlockSpec((SSD_CHUNK, width), lambda i: (i, col_block)),
                  pl.BlockSpec((HALO_ROWS, width), lambda i: (jnp.maximum(i * sub - 1, 0), col_block)),
                  pl.BlockSpec((HALO_ROWS, width),
                               lambda i: (jnp.minimum((i + 1) * sub, t // HALO_ROWS - 1), col_block)),
                  pl.BlockSpec((SSM_CONV_W, width), lambda i: (0, 0)),
                  pl.BlockSpec((1, width), lambda i: (0, 0))],
        out_specs=pl.BlockSpec((SSD_CHUNK, width), lambda i: (i, 0)),
        compiler_params=_cparams(1),
        name="dwconv_silu",
    )(p, p, p, w, b.reshape(1, width))


def _ssd_kernel(*refs, reverse, final):
    if final:
        (x_ref, bc_ref, dt_ref, dtb_ref, a_ref, z_ref, yprev_ref, dskip_ref, ng_ref, y_ref, state_ref) = refs
    else:
        (x_ref, bc_ref, dt_ref, dtb_ref, a_ref, y_ref, state_ref) = refs
    ch = SSD_CHUNK
    n = SSM_STATE
    inner = SSM_INNER
    gw = inner // SSM_GROUPS
    col0 = SSM_HEADS if reverse else 0

    @pl.when(pl.program_id(1) == 0)
    def _():
        state_ref[...] = jnp.zeros_like(state_ref)

    x = x_ref[...].astype(F32)
    bc = bc_ref[...]
    raw = dt_ref[...] + dtb_ref[...]
    dt = jnp.maximum(raw, 0.0) + jnp.log(1.0 + jnp.exp(-jnp.abs(raw)))
    adt = dt * a_ref[...]

    ri = lax.broadcasted_iota(I32, (ch, ch), 0)
    ci = lax.broadcasted_iota(I32, (ch, ch), 1)
    if reverse:
        tri = ci >= ri
        tri_t = ri >= ci
    else:
        tri = ci <= ri
        tri_t = ri <= ci
    acum = _dot_01_left(tri.astype(BF16), adt)
    acum_t = _dot_01_right(adt.T, tri_t.astype(BF16))
    last = 0 if reverse else ch - 1
    total = acum[last:last + 1, :]

    hrow = lax.broadcasted_iota(I32, (LANES, inner), 0)
    hcol = lax.broadcasted_iota(I32, (LANES, inner), 1) // SSM_HEAD_DIM
    sel = (hrow == hcol + col0).astype(BF16)
    stack = jnp.concatenate([dt, jnp.exp(acum), jnp.exp(total - acum),
                             jnp.broadcast_to(jnp.exp(total), (8, LANES))], axis=0)
    expanded = _dot_01_right(stack, sel)
    dt_x = expanded[0:ch]
    ea_x = expanded[ch:2 * ch]
    ds_x = expanded[2 * ch:3 * ch]
    tot_x = expanded[3 * ch:3 * ch + 1]

    xd = x * dt_x
    xw = (xd * ds_x).astype(BF16)
    xd_b = xd.astype(BF16)
    state = state_ref[...]
    lane = lax.broadcasted_iota(I32, (ch, LANES), 1)
    first_half = lane < SSM_HEAD_DIM
    y_parts = []

    for g in range(SSM_GROUPS):
        b_g = bc[:, g * n:(g + 1) * n]
        c_g = bc[:, (SSM_GROUPS + g) * n:(SSM_GROUPS + g + 1) * n]
        cb = lax.dot_general(c_g, b_g, (((1,), (1,)), ((), ())), preferred_element_type=F32)
        gs = slice(g * gw, (g + 1) * gw)
        y_off = jnp.dot(c_g, state[:, gs].astype(BF16), preferred_element_type=F32) * ea_x[:, gs]
        b_t = b_g.astype(F32).T.astype(BF16)
        new_state = tot_x[:, gs] * state[:, gs] + jnp.dot(b_t, xw[:, gs], preferred_element_type=F32)
        state_ref[:, gs] = new_state
        for pair in range(gw // LANES):
            lanes = slice(g * gw + pair * LANES, g * gw + (pair + 1) * LANES)
            halves = []
            for sub in range(2):
                head = (g * gw + pair * LANES) // SSM_HEAD_DIM + sub
                cidx = col0 + head
                diff = acum[:, cidx:cidx + 1] - acum_t[cidx:cidx + 1, :]
                m_h = (cb * jnp.where(tri, jnp.exp(jnp.where(tri, diff, 0.0)), 0.0)).astype(BF16)
                halves.append(jnp.dot(m_h, xd_b[:, lanes], preferred_element_type=F32))
            y_parts.append(jnp.where(first_half, halves[0], halves[1]) + y_off[:, pair * LANES:(pair + 1) * LANES])

    y = jnp.concatenate(y_parts, axis=1)
    if final:
        y = y + yprev_ref[...] + x * dskip_ref[...]
        y = y * _silu(z_ref[...].astype(F32))
        ms = jnp.mean(y * y, axis=-1, keepdims=True)
        y = (y * lax.rsqrt(ms + NORM_EPS)) * ng_ref[...]
    y_ref[...] = y.astype(y_ref.dtype)


def ssd_scan(xc, bcc, dtr, dtb, a_row, *, batch, chunks_per_batch, ctx_chunks, reverse,
             z_src=None, z_col=None, y_prev=None, d_skip=None, norm_g=None):
    t, inner = xc.shape
    final = y_prev is not None
    npb = chunks_per_batch

    def chunk(b, s):
        if not reverse:
            c = s
        else:
            c = jnp.where(s < ctx_chunks, ctx_chunks - 1 - s, npb - 1 + ctx_chunks - s)
        return b * npb + c

    row = lambda b, s: (chunk(b, s), 0)
    const = lambda b, s: (0, 0)
    in_specs = [pl.BlockSpec((SSD_CHUNK, inner), row),
                pl.BlockSpec((SSD_CHUNK, bcc.shape[1]), row),
                pl.BlockSpec((SSD_CHUNK, LANES), row),
                pl.BlockSpec((1, LANES), const),
                pl.BlockSpec((1, LANES), const)]
    args = [xc, bcc, dtr, dtb, a_row]
    if final:
        in_specs += [pl.BlockSpec((SSD_CHUNK, inner), lambda b, s: (chunk(b, s), z_col)),
                     pl.BlockSpec((SSD_CHUNK, inner), row),
                     pl.BlockSpec((1, inner), const),
                     pl.BlockSpec((1, inner), const)]
        args += [z_src, y_prev, d_skip, norm_g]
    kern = functools.partial(_ssd_kernel, reverse=reverse, final=final)
    return pl.pallas_call(
        kern,
        out_shape=jax.ShapeDtypeStruct((t, inner), BF16 if final else F32),
        grid=(batch, npb),
        in_specs=in_specs,
        out_specs=pl.BlockSpec((SSD_CHUNK, inner), row),
        scratch_shapes=[pltpu.VMEM((SSM_STATE, inner), F32)],
        compiler_params=_cparams(2, VMEM_LIMIT),
        name="ssd_bwd" if reverse else "ssd_fwd",
    )(*args)


def _na_window_base(t, lat_rows):
    return jnp.minimum(jnp.clip(2 * t - NA_ROWS // 2, 0, lat_rows - NA_ROWS), lat_rows - NA_WIN_ROWS)


def _na_pair_config(t, lat_rows):
    clip = lambda v, lo, hi: max(lo, min(v, hi))
    starts = [clip(r - NA_ROWS // 2, 0, lat_rows - NA_ROWS) for r in (2 * t, 2 * t + 1)]
    base = min(starts[0], lat_rows - NA_WIN_ROWS)
    return (starts[0] - base, starts[1] - base, 2 * t - starts[0], 2 * t + 1 - starts[1])


NA_CONFIGS = ((0, 1, 4, 4), (0, 0, 0, 1), (0, 0, 2, 3), (2, 2, 4, 5), (2, 2, 6, 7), (99, 99, 0, 0))


def _na_config_index(t, n_pairs):
    return jnp.where(t < 0, 5, jnp.where(t == 0, 1, jnp.where(t == 1, 2, jnp.where(
        t == n_pairs - 2, 3, jnp.where(t == n_pairs - 1, 4, 0)))))


def _na_bias_kernel(rpb_ref, o_ref, toe_ref):
    head = pl.program_id(0)
    w = GRID_W
    ki = lax.broadcasted_iota(I32, (w, w), 0)
    qi = lax.broadcasted_iota(I32, (w, w), 1)
    cstart = jnp.clip(qi - NA_COLS // 2, 0, w - NA_COLS)
    col_ok = jnp.logical_and(ki >= cstart, ki < cstart + NA_COLS)
    col_idx = jnp.clip(ki - qi + NA_COLS - 1, 0, 2 * NA_COLS - 2)
    for rho in range(2 * NA_ROWS - 1):
        tile = jnp.zeros((w, w), F32)
        for tcol in range(2 * NA_COLS - 1):
            tile = jnp.where(col_idx == tcol, rpb_ref[head, rho, tcol], tile)
        toe_ref[rho] = jnp.where(col_ok, tile, NEG)
    masked = jnp.full((w, w), NEG, F32)
    for cfg, offsets in enumerate(NA_CONFIGS):
        for j in range(NA_WIN_ROWS):
            for qr in range(NA_PAIR):
                rel = j - offsets[qr]
                inside = 0 <= rel < NA_ROWS
                tile = toe_ref[rel - offsets[NA_PAIR + qr] + NA_ROWS - 1] if inside else masked
                o_ref[cfg, 0, j * w:(j + 1) * w, qr * w:(qr + 1) * w] = tile


def na_bias_tables(rpb, lat_rows):
    heads = rpb.shape[0]
    n_pairs = lat_rows // NA_PAIR
    assert lat_rows % NA_PAIR == 0 and n_pairs >= 5
    for t in range(n_pairs):
        idx = 1 if t == 0 else 2 if t == 1 else 3 if t == n_pairs - 2 else 4 if t == n_pairs - 1 else 0
        assert _na_pair_config(t, lat_rows) == NA_CONFIGS[idx]
    n_cfg = len(NA_CONFIGS)
    return pl.pallas_call(
        _na_bias_kernel,
        out_shape=jax.ShapeDtypeStruct((n_cfg, heads, NA_WIN_ROWS * GRID_W, NA_PAIR * GRID_W), F32),
        grid=(heads,),
        in_specs=[pl.BlockSpec(memory_space=pltpu.SMEM)],
        out_specs=pl.BlockSpec((n_cfg, 1, NA_WIN_ROWS * GRID_W, NA_PAIR * GRID_W), lambda h: (0, h, 0, 0)),
        scratch_shapes=[pltpu.VMEM((2 * NA_ROWS - 1, GRID_W, GRID_W), F32)],
        compiler_params=_cparams(1),
        name="na_bias_tables",
    )(rpb)


def _transpose_kernel(x_ref, o_ref):
    for tile in range(x_ref.shape[1] // LANES):
        lanes = slice(tile * LANES, (tile + 1) * LANES)
        o_ref[lanes, :] = x_ref[:, lanes].astype(F32).T.astype(o_ref.dtype)


def transpose_cols(p, col_block, width):
    t = p.shape[0]
    return pl.pallas_call(
        _transpose_kernel,
        out_shape=jax.ShapeDtypeStruct((width, t), p.dtype),
        grid=(t // TM,),
        in_specs=[pl.BlockSpec((TM, width), lambda i: (i, col_block))],
        out_specs=pl.BlockSpec((width, TM), lambda i: (0, i)),
        compiler_params=_cparams(1),
        name="transpose_cols",
    )(p)


def _na_kernel(*refs):
    nb = NA_WIN_ROWS * GRID_W // LANES
    q_ref = refs[0]
    k_refs = refs[1:1 + nb]
    kc_ref = refs[1 + nb]
    vt_refs = refs[2 + nb:2 + 2 * nb]
    vct_ref, bias_ref, o_ref, s_ref, p_ref = refs[2 + 2 * nb:]
    nq = q_ref.shape[0]
    n_lat = NA_WIN_ROWS * GRID_W
    n_keys = n_lat + kc_ref.shape[0]
    scale = HEAD_DIM ** -0.5
    top = lax.broadcasted_iota(I32, (LANES, nq), 0) < HEAD_DIM
    for pair in range(NA_HEADS // 2):
        lanes = slice(pair * LANES, (pair + 1) * LANES)
        qt = (q_ref[:, lanes] * scale).astype(F32).T
        kk = jnp.concatenate([r[:, lanes] for r in k_refs] + [kc_ref[:, lanes]], axis=0)
        for sub in range(2):
            keep = top if sub == 0 else jnp.logical_not(top)
            qz = jnp.where(keep, qt, 0.0).astype(BF16)
            s_ref[2 * pair + sub] = jnp.dot(kk, qz, preferred_element_type=F32)
    for head in range(NA_HEADS):
        s_lat = s_ref[head, 0:n_lat, :] + bias_ref[0, head]
        s_ctx = s_ref[head, n_lat:, :]
        m = jnp.maximum(jnp.max(s_lat, axis=0, keepdims=True), jnp.max(s_ctx, axis=0, keepdims=True))
        p_ref[head, 0:n_lat, :] = jnp.exp(s_lat - m).astype(BF16)
        p_ref[head, n_lat:, :] = jnp.exp(s_ctx - m).astype(BF16)
    ones = jnp.ones((ONES_ROWS, n_keys), BF16)
    for pair in range(NA_HEADS // 2):
        halves = []
        for sub in range(2):
            head = 2 * pair + sub
            rows = slice(head * HEAD_DIM, (head + 1) * HEAD_DIM)
            vt = jnp.concatenate([r[rows, :] for r in vt_refs] + [vct_ref[rows, :]], axis=1)
            pv = jnp.dot(jnp.concatenate([vt, ones], axis=0), p_ref[head], preferred_element_type=F32)
            halves.append(pv[0:HEAD_DIM] * (1.0 / pv[HEAD_DIM:HEAD_DIM + 1]))
        o_t = jnp.concatenate(halves, axis=0)
        o_ref[:, pair * LANES:(pair + 1) * LANES] = o_t.T.astype(o_ref.dtype)


def na_attention(p, vt_all, bias, *, batch, lat_rows, ctx_len, q_col, k_col):
    t = p.shape[0]
    width = NA_HEADS * HEAD_DIM
    nq = NA_PAIR * GRID_W
    assert nq == LANES and ctx_len % nq == 0
    nb = NA_WIN_ROWS * GRID_W // LANES
    ctx_steps = ctx_len // nq
    n_pairs = lat_rows // NA_PAIR
    steps = ctx_steps + n_pairs
    blocks_ctx = steps * nq // ctx_len

    def key_block(b, st, j):
        base = _na_window_base(st - ctx_steps, lat_rows)
        return b * steps + ctx_steps + base // NA_PAIR + j

    in_specs = [pl.BlockSpec((nq, width), lambda b, st: (b * steps + st, q_col))]
    in_specs += [pl.BlockSpec((LANES, width), functools.partial(lambda b, st, j: (key_block(b, st, j), k_col), j=j))
                 for j in range(nb)]
    in_specs += [pl.BlockSpec((ctx_len, width), lambda b, st: (b * blocks_ctx, k_col))]
    in_specs += [pl.BlockSpec((width, LANES), functools.partial(lambda b, st, j: (0, key_block(b, st, j)), j=j))
                 for j in range(nb)]
    in_specs += [pl.BlockSpec((width, ctx_len), lambda b, st: (0, b * blocks_ctx)),
                 pl.BlockSpec((1, NA_HEADS, NA_WIN_ROWS * GRID_W, nq),
                              lambda b, st: (_na_config_index(st - ctx_steps, n_pairs), 0, 0, 0))]
    args = [p] * (2 + nb) + [vt_all] * (1 + nb) + [bias]
    n_keys = NA_WIN_ROWS * GRID_W + ctx_len
    return pl.pallas_call(
        _na_kernel,
        out_shape=jax.ShapeDtypeStruct((t, width), BF16),
        grid=(batch, steps),
        in_specs=in_specs,
        out_specs=pl.BlockSpec((nq, width), lambda b, st: (b * steps + st, 0)),
        scratch_shapes=[pltpu.VMEM((NA_HEADS, n_keys, nq), F32),
                        pltpu.VMEM((NA_HEADS, n_keys, nq), BF16)],
        compiler_params=_cparams(2, VMEM_LIMIT),
        name="na_attention",
    )(*args)


def _group_mean_sq(x):
    gi = lax.broadcasted_iota(I32, (LANES, LANES), 0) // HEAD_DIM
    gj = lax.broadcasted_iota(I32, (LANES, LANES), 1) // HEAD_DIM
    ones = (gi == gj).astype(BF16)
    return _dot_01_right(x * x, ones) * (1.0 / HEAD_DIM)


def _rope(x, cos_f, sin_s):
    lane = lax.broadcasted_iota(I32, x.shape, 1)
    low = (lane % HEAD_DIM) < HEAD_DIM // 2
    partner = jnp.where(low, pltpu.roll(x, LANES - HEAD_DIM // 2, 1), pltpu.roll(x, HEAD_DIM // 2, 1))
    return x * cos_f + partner * sin_s


def _gqa_prep_kernel(p_ref, cos_ref, sin_ref, qg_ref, kg_ref, qt_ref, k_ref, vt_ref, *, ctx_blocks):
    j = pl.program_id(1)
    is_lat = j >= ctx_blocks
    q_w = GQA_HEADS * HEAD_DIM
    kv_w = GQA_KV_HEADS * HEAD_DIM
    cos_f = jnp.where(is_lat, cos_ref[...], 1.0)
    sin_s = jnp.where(is_lat, sin_ref[...], 0.0)

    def normed(cols, gain):
        x = p_ref[:, cols].astype(F32)
        y = x * lax.rsqrt(_group_mean_sq(x) + NORM_EPS)
        return _rope(y * gain, cos_f, sin_s)

    for tile in range(kv_w // LANES):
        cols = slice(q_w + tile * LANES, q_w + (tile + 1) * LANES)
        k_ref[:, tile * LANES:(tile + 1) * LANES] = normed(cols, kg_ref[...]).astype(k_ref.dtype)
    v = p_ref[:, q_w + kv_w:q_w + 2 * kv_w].astype(F32)
    vt_ref[0] = v.T.astype(vt_ref.dtype)

    @pl.when(is_lat)
    def _():
        scale = HEAD_DIM ** -0.5 * math.log2(math.e)
        for tile in range(q_w // LANES):
            cols = slice(tile * LANES, (tile + 1) * LANES)
            q = normed(cols, qg_ref[...]) * scale
            qt_ref[0, cols, :] = q.T.astype(qt_ref.dtype)


def gqa_prep(p1, cos_f, sin_s, qg, kg, *, batch, blocks_per_batch, ctx_blocks):
    t, n = p1.shape
    q_w = GQA_HEADS * HEAD_DIM
    kv_w = GQA_KV_HEADS * HEAD_DIM
    lb = blocks_per_batch * TM
    lat = (blocks_per_batch - ctx_blocks) * TM
    kern = functools.partial(_gqa_prep_kernel, ctx_blocks=ctx_blocks)
    lat_blk = lambda b, j: jnp.maximum(j - ctx_blocks, 0)
    return pl.pallas_call(
        kern,
        out_shape=(jax.ShapeDtypeStruct((batch, q_w, lat), BF16),
                   jax.ShapeDtypeStruct((t, kv_w), BF16),
                   jax.ShapeDtypeStruct((batch, kv_w, lb), BF16)),
        grid=(batch, blocks_per_batch),
        in_specs=[pl.BlockSpec((TM, n), lambda b, j: (b * blocks_per_batch + j, 0)),
                  pl.BlockSpec((TM, LANES), lambda b, j: (lat_blk(b, j), 0)),
                  pl.BlockSpec((TM, LANES), lambda b, j: (lat_blk(b, j), 0)),
                  pl.BlockSpec((1, LANES), lambda b, j: (0, 0)),
                  pl.BlockSpec((1, LANES), lambda b, j: (0, 0))],
        out_specs=(pl.BlockSpec((1, q_w, TM), lambda b, j: (b, 0, lat_blk(b, j))),
                   pl.BlockSpec((TM, kv_w), lambda b, j: (b * blocks_per_batch + j, 0)),
                   pl.BlockSpec((1, kv_w, TM), lambda b, j: (b, 0, j))),
        compiler_params=_cparams(2, VMEM_LIMIT),
        name="gqa_prep",
    )(p1, cos_f, sin_s, qg, kg)


def _gqa_attn_kernel(qt_ref, k_ref, vt_ref, o_ref, qz_ref, s_ref, bmax_ref, m_ref, l_ref, acc_ref, *, tk, n_kblocks):
    kv = pl.program_id(1)
    tq = qt_ref.shape[2]
    group = GQA_HEADS // GQA_KV_HEADS
    second = (kv % 2) == 1
    for hq in range(group):
        q_h = qt_ref[0, hq * HEAD_DIM:(hq + 1) * HEAD_DIM, :]
        zero = jnp.zeros_like(q_h)
        qz_ref[0:HEAD_DIM, hq * tq:(hq + 1) * tq] = jnp.where(second, zero, q_h)
        qz_ref[HEAD_DIM:2 * HEAD_DIM, hq * tq:(hq + 1) * tq] = jnp.where(second, q_h, zero)
    m_ref[...] = jnp.full_like(m_ref, NEG)
    l_ref[...] = jnp.zeros_like(l_ref)
    acc_ref[...] = jnp.zeros_like(acc_ref)

    def scores(kb, slot):
        off = pl.multiple_of(kb * tk, tk)
        s = jnp.dot(k_ref[0, pl.ds(off, tk), :], qz_ref[...], preferred_element_type=F32)
        s_ref[slot] = s
        bmax_ref[slot] = jnp.max(s, axis=0, keepdims=True)

    def update(kb, slot):
        off = pl.multiple_of(kb * tk, tk)
        s = s_ref[slot]
        m_old = m_ref[...]
        m_new = jnp.maximum(m_old, bmax_ref[slot])
        alpha = jnp.exp2(m_old - m_new)
        p = jnp.exp2(s - m_new).astype(BF16)
        v_aug = jnp.concatenate([vt_ref[0, :, pl.ds(off, tk)], jnp.ones((ONES_ROWS, tk), BF16)], axis=0)
        pv = jnp.dot(v_aug, p, preferred_element_type=F32)
        acc_ref[...] = alpha * acc_ref[...] + pv[0:HEAD_DIM]
        l_ref[...] = alpha * l_ref[...] + pv[HEAD_DIM:HEAD_DIM + 1]
        m_ref[...] = m_new

    per_trip = GQA_BLOCKS_PER_TRIP
    scores(0, 0)
    trips = (n_kblocks - 1) // per_trip

    def one_trip(i, carry):
        kb = per_trip * i
        for j in range(per_trip):
            scores(kb + j + 1, (j + 1) % 2)
            update(kb + j, j % 2)
        return carry

    lax.fori_loop(0, trips, one_trip, 0)
    base = trips * per_trip
    for j in range(n_kblocks - base):
        if base + j + 1 < n_kblocks:
            scores(base + j + 1, (j + 1) % 2)
        update(base + j, j % 2)

    inv_l = 1.0 / l_ref[...]
    for pair in range(group // 2):
        lo, hi = 2 * pair * tq, (2 * pair + 1) * tq
        o_t = jnp.concatenate([acc_ref[:, lo:lo + tq] * inv_l[:, lo:lo + tq],
                               acc_ref[:, hi:hi + tq] * inv_l[:, hi:hi + tq]], axis=0)
        o_ref[:, pair * LANES:(pair + 1) * LANES] = o_t.T.astype(o_ref.dtype)


def gqa_attention(qt, k, vt, *, tq=512, tk=256):
    batch, q_w, lat = qt.shape
    lb = k.shape[1]
    group = GQA_HEADS // GQA_KV_HEADS
    gw = group * HEAD_DIM
    tq = min(tq, lat)
    kern = functools.partial(_gqa_attn_kernel, tk=tk, n_kblocks=lb // tk)
    return pl.pallas_call(
        kern,
        out_shape=jax.ShapeDtypeStruct((batch * lat, q_w), BF16),
        grid=(batch, GQA_KV_HEADS, lat // tq),
        in_specs=[pl.BlockSpec((1, gw, tq), lambda b, h, i: (b, h, i)),
                  pl.BlockSpec((1, lb, LANES), lambda b, h, i: (b, 0, h // 2)),
                  pl.BlockSpec((1, HEAD_DIM, lb), lambda b, h, i: (b, h, 0))],
        out_specs=pl.BlockSpec((tq, gw), lambda b, h, i: (b * (lat // tq) + i, h)),
        scratch_shapes=[pltpu.VMEM((2 * HEAD_DIM, group * tq), BF16),
                        pltpu.VMEM((2, tk, group * tq), F32),
                        pltpu.VMEM((2, 1, group * tq), F32),
                        pltpu.VMEM((1, group * tq), F32),
                        pltpu.VMEM((1, group * tq), F32),
                        pltpu.VMEM((HEAD_DIM, group * tq), F32)],
        compiler_params=_cparams(3, VMEM_LIMIT),
        name="gqa_attention",
    )(qt, k, vt)


def _router_kernel(h_ref, g_ref, mod_ref, rw_ref, rb_ref, xn_ref, eidx_ref, wts_ref, rank_ref, cnt_ref, run_ref):
    @pl.when(pl.program_id(0) == 0)
    def _():
        run_ref[...] = jnp.zeros_like(run_ref)

    xn = _norm_mod(h_ref[...], g_ref[...], mod_ref[0, 3:4, :], mod_ref[0, 4:5, :])
    xn_ref[...] = _pack_bf16_pairs(xn)
    tm = xn.shape[0]
    logits = lax.dot_general(rw_ref[...], xn, (((1,), (1,)), ((), ())),
                             precision=HIGHEST, preferred_element_type=F32)
    s = jax.nn.sigmoid(logits)
    sb = s + rb_ref[...]
    row = lambda a, e: a[e:e + 1, :]

    best = None
    for g in range(N_GROUPS):
        a = [row(sb, g * EXPERTS_PER_GROUP + j) for j in range(EXPERTS_PER_GROUP)]
        score = a[0] + a[1]
        for i in range(EXPERTS_PER_GROUP):
            for j in range(i + 1, EXPERTS_PER_GROUP):
                if (i, j) != (0, 1):
                    score = jnp.maximum(score, a[i] + a[j])
        if best is None:
            best, grp = score, jnp.zeros((1, tm), I32)
        else:
            better = score > best
            best = jnp.where(better, score, best)
            grp = jnp.where(better, g, grp)

    biased, plain = [], []
    for j in range(EXPERTS_PER_GROUP):
        vb = row(sb, j)
        vp = row(s, j)
        for g in range(1, N_GROUPS):
            pick = grp == g
            vb = jnp.where(pick, row(sb, g * EXPERTS_PER_GROUP + j), vb)
            vp = jnp.where(pick, row(s, g * EXPERTS_PER_GROUP + j), vp)
        biased.append(vb)
        plain.append(vp)

    def argmax_first(vals):
        top, idx = vals[0], jnp.zeros((1, tm), I32)
        for j in range(1, len(vals)):
            better = vals[j] > top
            top = jnp.where(better, vals[j], top)
            idx = jnp.where(better, j, idx)
        return idx

    i1 = argmax_first(biased)
    i2 = argmax_first([jnp.where(i1 == j, -jnp.inf, biased[j]) for j in range(EXPERTS_PER_GROUP)])

    def pick_plain(idx):
        out = plain[0]
        for j in range(1, EXPERTS_PER_GROUP):
            out = jnp.where(idx == j, plain[j], out)
        return out

    w1, w2 = pick_plain(i1), pick_plain(i2)
    wsum = w1 + w2
    e1 = grp * EXPERTS_PER_GROUP + i1
    e2 = grp * EXPERTS_PER_GROUP + i2
    eidx_ref[0] = jnp.concatenate([e1, e2], axis=0)
    wts_ref[...] = jnp.concatenate([w1 / wsum, w2 / wsum], axis=0)

    erow = lax.broadcasted_iota(I32, (N_EXPERTS, tm), 0)
    hit1 = erow == e1
    hit2 = erow == e2
    onehot = jnp.logical_or(hit1, hit2)
    si = lax.broadcasted_iota(I32, (tm, tm), 0)
    ti = lax.broadcasted_iota(I32, (tm, tm), 1)
    before = jnp.dot(onehot.astype(BF16), (si < ti).astype(BF16), preferred_element_type=F32)
    before = before + run_ref[...][:, 0:1]
    r1 = jnp.sum(jnp.where(hit1, before, 0.0), axis=0, keepdims=True)
    r2 = jnp.sum(jnp.where(hit2, before, 0.0), axis=0, keepdims=True)
    rank_ref[0] = jnp.concatenate([r1, r2], axis=0).astype(I32)
    totals = run_ref[...] + jnp.sum(onehot.astype(F32), axis=1, keepdims=True)
    run_ref[...] = totals
    cnt_ref[...] = totals


def moe_router(h, gain, modtab, rw_t, rb, *, mod_index):
    t, d = h.shape
    return pl.pallas_call(
        _router_kernel,
        out_shape=(jax.ShapeDtypeStruct((t, d // 2), I32),
                   jax.ShapeDtypeStruct((t // TM, 2, TM), I32),
                   jax.ShapeDtypeStruct((2, t), F32),
                   jax.ShapeDtypeStruct((t // TM, 2, TM), I32),
                   jax.ShapeDtypeStruct((N_EXPERTS, LANES), F32)),
        grid=(t // TM,),
        in_specs=[pl.BlockSpec((TM, d), lambda i: (i, 0)),
                  pl.BlockSpec((1, d), lambda i: (0, 0)),
                  pl.BlockSpec((1, 6, d), lambda i: (mod_index(i), 0, 0)),
                  pl.BlockSpec((N_EXPERTS, d), lambda i: (0, 0)),
                  pl.BlockSpec((N_EXPERTS, 1), lambda i: (0, 0))],
        out_specs=(pl.BlockSpec((TM, d // 2), lambda i: (i, 0)),
                   pl.BlockSpec((1, 2, TM), lambda i: (i, 0, 0)),
                   pl.BlockSpec((2, TM), lambda i: (0, i)),
                   pl.BlockSpec((1, 2, TM), lambda i: (i, 0, 0)),
                   pl.BlockSpec((N_EXPERTS, LANES), lambda i: (0, 0))),
        scratch_shapes=[pltpu.VMEM((N_EXPERTS, LANES), F32)],
        compiler_params=_cparams(1, VMEM_LIMIT),
        name="moe_router",
    )(h, gain.reshape(1, d), modtab, rw_t, rb)


def _row_copy(src_ref, src_row, dst_ref, dst_row, sem):
    return pltpu.make_async_copy(src_ref.at[pl.ds(src_row, 1)], dst_ref.at[pl.ds(dst_row, 1)], sem)


def _issue_and_drain(rows, make_copy):
    def issue(r, carry):
        for k in range(2):
            make_copy(r, k).start()
        return carry

    lax.fori_loop(0, rows, issue, 0, unroll=8)
    for _ in range(rows):
        for k in range(2):
            make_copy(0, k).wait()


def _dispatch_kernel(eidx_ref, rank_ref, start_ref, x_ref, xs_in_ref, xs_ref, sem):
    del xs_in_ref

    def make_copy(r, k):
        slot = start_ref[eidx_ref[0, k, r]] + rank_ref[0, k, r]
        return _row_copy(x_ref, r, xs_ref, slot, sem)

    _issue_and_drain(x_ref.shape[0], make_copy)


def moe_dispatch(xn, eidx3, rank3, pad_start, n_slots):
    t, d = xn.shape
    xs0 = jnp.zeros((n_slots, d), xn.dtype)
    idx_spec = pl.BlockSpec((1, 2, DMA_ROWS), lambda i: (i, 0, 0), memory_space=pltpu.SMEM)
    return pl.pallas_call(
        _dispatch_kernel,
        out_shape=jax.ShapeDtypeStruct((n_slots, d), xn.dtype),
        grid=(t // DMA_ROWS,),
        in_specs=[idx_spec, idx_spec,
                  pl.BlockSpec(memory_space=pltpu.SMEM),
                  pl.BlockSpec((DMA_ROWS, d), lambda i: (i, 0)),
                  pl.BlockSpec(memory_space=pl.ANY)],
        out_specs=pl.BlockSpec(memory_space=pl.ANY),
        scratch_shapes=[pltpu.SemaphoreType.DMA(())],
        input_output_aliases={4: 0},
        compiler_params=_cparams(1),
        name="moe_dispatch",
    )(eidx3, rank3, pad_start, xn, xs0)


def _expert_kernel(blk_e_ref, nused_ref, x_ref, wg_ref, wu_ref, wd_ref, y_ref, wg_b, wu_b, wd_b):
    i = pl.program_id(0)

    @pl.when(i < nused_ref[0])
    def _():
        @pl.when(jnp.logical_or(i == 0, blk_e_ref[i] != blk_e_ref[jnp.maximum(i - 1, 0)]))
        def _():
            wg_b[...] = wg_ref[0, 0].astype(BF16)
            wu_b[...] = wu_ref[0, 0].astype(BF16)
            wd_b[...] = wd_ref[0, 0].astype(BF16)

        x = _unpack_bf16_pairs(x_ref[...]).astype(BF16)
        gate = jnp.dot(x, wg_b[...], preferred_element_type=F32)
        up = jnp.dot(x, wu_b[...], preferred_element_type=F32)
        hid = (_silu(gate) * up).astype(BF16)
        y_ref[...] = _pack_bf16_pairs(jnp.dot(hid, wd_b[...], preferred_element_type=F32))

    @pl.when(pl.program_id(0) >= nused_ref[0])
    def _():
        y_ref[...] = jnp.zeros_like(y_ref)


def moe_experts(xs, blk_e, nused, wg, wu, wd, layer):
    n_slots, dw = xs.shape
    d, de = wg.shape[2], wg.shape[3]
    grid_spec = pltpu.PrefetchScalarGridSpec(
        num_scalar_prefetch=2,
        grid=(n_slots // MOE_BM,),
        in_specs=[pl.BlockSpec((MOE_BM, dw), lambda i, be, nu: (i, 0)),
                  pl.BlockSpec((1, 1, d, de), lambda i, be, nu: (layer, be[i], 0, 0)),
                  pl.BlockSpec((1, 1, d, de), lambda i, be, nu: (layer, be[i], 0, 0)),
                  pl.BlockSpec((1, 1, de, d), lambda i, be, nu: (layer, be[i], 0, 0))],
        out_specs=pl.BlockSpec((MOE_BM, dw), lambda i, be, nu: (i, 0)),
        scratch_shapes=[pltpu.VMEM((d, de), BF16), pltpu.VMEM((d, de), BF16), pltpu.VMEM((de, d), BF16)],
    )
    return pl.pallas_call(
        _expert_kernel,
        out_shape=jax.ShapeDtypeStruct((n_slots, dw), I32),
        grid_spec=grid_spec,
        compiler_params=_cparams(1, VMEM_LIMIT),
        name="moe_experts",
    )(blk_e, nused, xs, wg, wu, wd)


def _combine_kernel(eidx_ref, rank_ref, eidx_nxt_ref, rank_nxt_ref, start_ref, h_ref, mod_ref, w_ref, fg_ref,
                    yb_ref, o_ref, buf_ref, sems, *, final):
    step = pl.program_id(0)
    rows = h_ref.shape[0]

    def start_gathers(e_ref, r_ref, half):
        def issue(r, carry):
            for k in range(2):
                slot = start_ref[e_ref[0, k, r]] + r_ref[0, k, r]
                _row_copy(yb_ref, slot, buf_ref.at[half, k], r, sems.at[half]).start()
            return carry

        lax.fori_loop(0, rows, issue, 0, unroll=8)

    cur = step % 2

    @pl.when(step == 0)
    def _():
        start_gathers(eidx_ref, rank_ref, 0)

    @pl.when(step + 1 < pl.num_programs(0))
    def _():
        start_gathers(eidx_nxt_ref, rank_nxt_ref, 1 - cur)

    for _ in range(rows):
        for k in range(2):
            _row_copy(yb_ref, 0, buf_ref.at[cur, k], 0, sems.at[cur]).wait()
    f = (w_ref[:, 0:1] * _unpack_bf16_pairs(buf_ref[cur, 0])
         + w_ref[:, 1:2] * _unpack_bf16_pairs(buf_ref[cur, 1]))
    h = h_ref[...] + mod_ref[0, 5:6, :] * f
    if final:
        ms = jnp.mean(h * h, axis=-1, keepdims=True)
        h = (h * lax.rsqrt(ms + NORM_EPS)) * fg_ref[...]
    o_ref[...] = h


def moe_combine(h, modtab, wts_t, final_g, yb, eidx3, rank3, pad_start, *, mod_index, final):
    t, d = h.shape
    kern = functools.partial(_combine_kernel, final=final)
    n_steps = t // DMA_ROWS
    idx_spec = pl.BlockSpec((1, 2, DMA_ROWS), lambda i: (i, 0, 0), memory_space=pltpu.SMEM)
    nxt_spec = pl.BlockSpec((1, 2, DMA_ROWS), lambda i: (jnp.minimum(i + 1, n_steps - 1), 0, 0),
                            memory_space=pltpu.SMEM)
    return pl.pallas_call(
        kern,
        out_shape=jax.ShapeDtypeStruct((t, d), F32),
        grid=(n_steps,),
        in_specs=[idx_spec, idx_spec, nxt_spec, nxt_spec,
                  pl.BlockSpec(memory_space=pltpu.SMEM),
                  pl.BlockSpec((DMA_ROWS, d), lambda i: (i, 0)),
                  pl.BlockSpec((1, 6, d), lambda i: (mod_index(i), 0, 0)),
                  pl.BlockSpec((DMA_ROWS, 2), lambda i: (i, 0)),
                  pl.BlockSpec((1, d), lambda i: (0, 0)),
                  pl.BlockSpec(memory_space=pl.ANY)],
        out_specs=pl.BlockSpec((DMA_ROWS, d), lambda i: (i, 0)),
        scratch_shapes=[pltpu.VMEM((2, 2, DMA_ROWS, yb.shape[1]), yb.dtype), pltpu.SemaphoreType.DMA((2,))],
        compiler_params=_cparams(1, VMEM_LIMIT),
        name="moe_combine",
    )(eidx3, rank3, eidx3, rank3, pad_start, h, modtab, wts_t, final_g.reshape(1, d), yb)


def moe_layer(h, norm_g, modtab, rw_t, rb, wg, wu, wd, layer, final_g, *, mod_index, final):
    t, d = h.shape
    xn, eidx3, wts, rank3, counts = moe_router(h, norm_g, modtab, rw_t, rb, mod_index=mod_index)
    n_blocks = (2 * t) // MOE_BM + N_EXPERTS
    n_slots = n_blocks * MOE_BM
    cnt = counts[:, 0].astype(I32)
    padded = (cnt + MOE_BM - 1) // MOE_BM * MOE_BM
    pad_end = jnp.cumsum(padded)
    pad_start = (pad_end - padded).astype(I32)
    blk_start = jnp.arange(n_blocks, dtype=I32) * MOE_BM
    blk_e = jnp.minimum(jnp.sum((pad_end[None, :] <= blk_start[:, None]).astype(I32), axis=1), N_EXPERTS - 1)
    nused = (pad_end[-1] // MOE_BM).reshape(1).astype(I32)
    xs = moe_dispatch(xn, eidx3, rank3, pad_start, n_slots)
    yb = moe_experts(xs, blk_e.astype(I32), nused, wg, wu, wd, layer)
    return moe_combine(h, modtab, wts.T, final_g, yb, eidx3, rank3, pad_start, mod_index=mod_index, final=final)


def _rope_tables(lat):
    t = jnp.arange(lat)
    row = (t // GRID_W).astype(F32)
    col = (t % GRID_W).astype(F32)
    n_freq = HEAD_DIM // 4
    inv = ROPE_THETA ** (-jnp.arange(n_freq, dtype=F32) / n_freq)
    ang = jnp.concatenate([row[:, None] * inv, col[:, None] * inv], axis=-1)
    cos, sin = jnp.cos(ang), jnp.sin(ang)
    cos_f = jnp.tile(cos, (1, LANES // (HEAD_DIM // 2)))
    sin_s = jnp.tile(jnp.concatenate([-sin, sin], axis=-1), (1, LANES // HEAD_DIM))
    return cos_f, sin_s


def kernel(x, c, ctx, c_ctx, w_mod, b_mod, norm1_g, norm2_g, final_g, ab_w_in, ab_w_out, ssm_conv_w, ssm_conv_b, ssm_A_log, ssm_dt_bias, ssm_D, ssm_norm_g, na_rpb, gqa_w_in, gqa_w_out, gqa_q_norm, gqa_k_norm, router_w, router_bias, moe_w_gate, moe_w_up, moe_w_down):
    bsz, lat, d = x.shape
    lc = ctx.shape[1]
    lb = lc + lat
    assert lat % TM == 0 and lc % TM == 0 and lat % GRID_W == 0 and lat // GRID_W >= NA_ROWS
    assert bsz + 1 <= 8 and w_mod.shape[0] == 2
    nb = lb // TM
    cb = lc // TM
    nlb = lat // TM

    def mod_comb(i):
        return jnp.where(i % nb < cb, bsz, i // nb)

    def mod_lat(i):
        return i // nlb

    inner, xbc_w = SSM_INNER, SSM_INNER + 2 * SSM_GROUPS * SSM_STATE
    w_in = ab_w_in[0]
    o_dt = inner + xbc_w
    o_q = o_dt + 2 * SSM_HEADS
    w_z, w_x, w_bc = w_in[:, :inner], w_in[:, inner:2 * inner], w_in[:, 2 * inner:o_dt]
    w_dt = w_in[:, o_dt:o_q]
    w_qkv = w_in[:, o_q:]
    w_in_b = jnp.concatenate([w_qkv, w_z, w_x, w_bc], axis=1).astype(BF16)
    w_dt_b = jnp.pad(w_dt, ((0, 0), (0, LANES - 2 * SSM_HEADS))).astype(BF16)
    conv_w, conv_b = ssm_conv_w[0], ssm_conv_b[0]
    pad32 = lambda v: jnp.pad(v.reshape(1, -1), ((0, 0), (0, LANES - 2 * SSM_HEADS)))
    dtb_row = pad32(ssm_dt_bias[0])
    a_row = pad32(-jnp.exp(ssm_A_log[0].astype(F32)))
    d_skip = jnp.repeat(ssm_D[0], SSM_HEAD_DIM).reshape(1, inner)
    w_out = ab_w_out[0].astype(BF16)
    w_gqa = gqa_w_in[0].astype(BF16)
    w_gqa_out = gqa_w_out[0].astype(BF16)
    qg = jnp.tile(gqa_q_norm[0], LANES // HEAD_DIM).reshape(1, LANES)
    kg = jnp.tile(gqa_k_norm[0], LANES // HEAD_DIM).reshape(1, LANES)
    rw_t = router_w.T
    rb = router_bias.reshape(N_EXPERTS, 1)
    wg, wu, wd = moe_w_gate, moe_w_up, moe_w_down

    cc = jnp.concatenate([c, c_ctx[None, :], jnp.zeros((8 - bsz - 1, d), F32)], axis=0)
    mod = modulation(cc, w_mod, b_mod)[:, :bsz + 1].reshape(2, bsz + 1, 6, d)

    h0 = jnp.concatenate([ctx, x], axis=1).reshape(bsz * lb, d)

    p, dtr = norm_matmul(h0, norm1_g[0], mod[0], [w_in_b, w_dt_b], [BF16, F32],
                         shift_idx=0, scale_idx=1, mod_index=mod_comb)
    chunks = lb // SSD_CHUNK
    ctx_chunks = lc // SSD_CHUNK
    xc = dwconv_silu(p, 4, inner, conv_w[:, :inner], conv_b[:inner], chunks_per_batch=chunks, ctx_chunks=ctx_chunks)
    bcc = dwconv_silu(p, 10, xbc_w - inner, conv_w[:, inner:], conv_b[inner:],
                      chunks_per_batch=chunks, ctx_chunks=ctx_chunks)
    scan = functools.partial(ssd_scan, xc, bcc, dtr, dtb_row, a_row, batch=bsz,
                             chunks_per_batch=chunks, ctx_chunks=ctx_chunks)
    y_f = scan(reverse=False)
    y_ssd = scan(reverse=True, z_src=p, z_col=3, y_prev=y_f, d_skip=d_skip, norm_g=ssm_norm_g[0].reshape(1, inner))
    bias = na_bias_tables(na_rpb[0], lat // GRID_W)
    vt_all = transpose_cols(p, 2, NA_HEADS * HEAD_DIM)
    o_na = na_attention(p, vt_all, bias, batch=bsz, lat_rows=lat // GRID_W, ctx_len=lc, q_col=0, k_col=1)
    ident = lambda i: i
    h1 = resid_matmul(h0, mod[0], [y_ssd, o_na], [w_out[:inner], w_out[inner:]],
                      gate_idx=2, n_blocks=bsz * nb, h_block=ident, mod_index=mod_comb)
    h2 = moe_layer(h1, norm2_g[0], mod[0], rw_t, rb, wg, wu, wd, 0, final_g, mod_index=mod_comb, final=False)

    (p1,) = norm_matmul(h2, norm1_g[1], mod[1], [w_gqa], [BF16], shift_idx=0, scale_idx=1, mod_index=mod_comb)
    cos_f, sin_s = _rope_tables(lat)
    qt, kn, vt = gqa_prep(p1, cos_f, sin_s, qg, kg, batch=bsz, blocks_per_batch=nb, ctx_blocks=cb)
    o = gqa_attention(qt, kn.reshape(bsz, lb, GQA_KV_HEADS * HEAD_DIM), vt)
    lat_block = lambda i: (i // nlb) * nb + cb + i % nlb
    h3 = resid_matmul(h2, mod[1], [o], [w_gqa_out], gate_idx=2, n_blocks=bsz * nlb, h_block=lat_block, mod_index=mod_lat)
    out = moe_layer(h3, norm2_g[1], mod[1], rw_t, rb, wg, wu, wd, 1, final_g, mod_index=mod_lat, final=True)
    return out.reshape(bsz, lat, d)
```

```python
import functools
import math

import jax
import jax.numpy as jnp
from jax import lax
from jax.experimental import pallas as pl
from jax.experimental.pallas import tpu as pltpu

F32 = jnp.float32
BF16 = jnp.bfloat16
I32 = jnp.int32
HIGHEST = lax.Precision.HIGHEST

NORM_EPS = 1e-6
GRID_W = 64
SSM_HEADS = 16
SSM_HEAD_DIM = 64
SSM_INNER = 1024
SSM_STATE = 128
SSM_GROUPS = 2
SSM_CONV_W = 5
SSD_CHUNK = 128
NA_HEADS = 16
NA_ROWS = 8
NA_COLS = 16
NA_PAIR = 2
NA_WIN_ROWS = 10
HEAD_DIM = 64
GQA_HEADS = 16
GQA_KV_HEADS = 4
ROPE_THETA = 10000.0
N_EXPERTS = 32
N_GROUPS = 8
EXPERTS_PER_GROUP = 4
D_EXPERT = 512

LANES = 128
HALO_ROWS = 16
ONES_ROWS = 16
GQA_BLOCKS_PER_TRIP = 8
TM = 256
MOE_BM = 512
DMA_ROWS = TM
NEG = -1e30
VMEM_LIMIT = 48 * 1024 * 1024


def _cparams(n_axes, vmem=None):
    return pltpu.CompilerParams(dimension_semantics=("arbitrary",) * n_axes,
                                vmem_limit_bytes=vmem)


def _silu(x):
    return x * jax.nn.sigmoid(x)


def _split3(v):
    hi = v.astype(BF16)
    r1 = v - hi.astype(F32)
    mid = r1.astype(BF16)
    lo = (r1 - mid.astype(F32)).astype(BF16)
    return hi, mid, lo


def _dot_01_right(v, m01):
    lhs = jnp.concatenate(_split3(v), axis=1)
    rhs = jnp.concatenate([m01, m01, m01], axis=0)
    return jnp.dot(lhs, rhs, preferred_element_type=F32)


def _dot_01_left(m01, v):
    lhs = jnp.concatenate([m01, m01, m01], axis=1)
    rhs = jnp.concatenate(_split3(v), axis=0)
    return jnp.dot(lhs, rhs, preferred_element_type=F32)


HI_HALF = -65536


def _pack_bf16_pairs(x):
    n = x.shape[1] // 2
    lo = lax.bitcast_convert_type(x[:, :n].astype(BF16).astype(F32), I32)
    hi = lax.bitcast_convert_type(x[:, n:].astype(BF16).astype(F32), I32)
    return lax.shift_right_logical(lo, 16) | (hi & HI_HALF)


def _unpack_bf16_pairs(u):
    lo = lax.bitcast_convert_type(lax.shift_left(u, 16), F32)
    hi = lax.bitcast_convert_type(u & HI_HALF, F32)
    return jnp.concatenate([lo, hi], axis=1)


def _norm_mod(x, g, shift, scale):
    ms = jnp.mean(x * x, axis=-1, keepdims=True)
    y = x * lax.rsqrt(ms + NORM_EPS)
    return (y * g) * (1.0 + scale) + shift


def _mod_kernel(c_ref, w_ref, b_ref, o_ref):
    a = _silu(c_ref[...]).astype(BF16)
    o_ref[0] = jnp.dot(a, w_ref[0].astype(BF16), preferred_element_type=F32) + b_ref[0]


def modulation(cc, w_mod, b_mod):
    depth, d, n = w_mod.shape
    tn = 1536
    return pl.pallas_call(
        _mod_kernel,
        out_shape=jax.ShapeDtypeStruct((depth, 8, n), F32),
        grid=(depth, n // tn),
        in_specs=[pl.BlockSpec((8, d), lambda l, j: (0, 0)),
                  pl.BlockSpec((1, d, tn), lambda l, j: (l, 0, j)),
                  pl.BlockSpec((1, 1, tn), lambda l, j: (l, 0, j))],
        out_specs=pl.BlockSpec((1, 8, tn), lambda l, j: (l, 0, j)),
        compiler_params=_cparams(2, VMEM_LIMIT),
        name="modulation",
    )(cc, w_mod, b_mod.reshape(depth, 1, n))


def _norm_matmul_kernel(h_ref, g_ref, mod_ref, *refs, shift_idx, scale_idx):
    n_w = len(refs) // 2
    xn = _norm_mod(h_ref[...], g_ref[...],
                   mod_ref[0, shift_idx:shift_idx + 1, :], mod_ref[0, scale_idx:scale_idx + 1, :]).astype(BF16)
    for w_ref, o_ref in zip(refs[:n_w], refs[n_w:]):
        o_ref[...] = jnp.dot(xn, w_ref[...], preferred_element_type=F32).astype(o_ref.dtype)


def norm_matmul(h, gain, modtab, weights, out_dtypes, *, shift_idx, scale_idx, mod_index):
    t, d = h.shape
    kern = functools.partial(_norm_matmul_kernel, shift_idx=shift_idx, scale_idx=scale_idx)
    outs = pl.pallas_call(
        kern,
        out_shape=tuple(jax.ShapeDtypeStruct((t, w.shape[1]), dt) for w, dt in zip(weights, out_dtypes)),
        grid=(t // TM,),
        in_specs=[pl.BlockSpec((TM, d), lambda i: (i, 0)),
                  pl.BlockSpec((1, d), lambda i: (0, 0)),
                  pl.BlockSpec((1, 6, d), lambda i: (mod_index(i), 0, 0))]
                 + [pl.BlockSpec(w.shape, lambda i: (0, 0)) for w in weights],
        out_specs=tuple(pl.BlockSpec((TM, w.shape[1]), lambda i: (i, 0)) for w in weights),
        compiler_params=_cparams(1, VMEM_LIMIT),
        name="norm_matmul",
    )(h, gain.reshape(1, d), modtab, *weights)
    return outs


def _resid_matmul_kernel(*refs, n_pairs, gate_idx):
    h_ref, mod_ref = refs[0], refs[1]
    a_refs = refs[2:2 + n_pairs]
    w_refs = refs[2 + n_pairs:2 + 2 * n_pairs]
    o_ref = refs[2 + 2 * n_pairs]
    acc = jnp.dot(a_refs[0][...], w_refs[0][...], preferred_element_type=F32)
    for a_ref, w_ref in zip(a_refs[1:], w_refs[1:]):
        acc = acc + jnp.dot(a_ref[...], w_ref[...], preferred_element_type=F32)
    o_ref[...] = h_ref[...] + mod_ref[0, gate_idx:gate_idx + 1, :] * acc


def resid_matmul(h, modtab, a_list, w_list, *, gate_idx, n_blocks, h_block, mod_index):
    d = h.shape[1]
    n_pairs = len(a_list)
    kern = functools.partial(_resid_matmul_kernel, n_pairs=n_pairs, gate_idx=gate_idx)
    in_specs = [pl.BlockSpec((TM, d), lambda i: (h_block(i), 0)),
                pl.BlockSpec((1, 6, d), lambda i: (mod_index(i), 0, 0))]
    in_specs += [pl.BlockSpec((TM, a.shape[1]), lambda i: (i, 0)) for a in a_list]
    in_specs += [pl.BlockSpec(w.shape, lambda i: (0, 0)) for w in w_list]
    return pl.pallas_call(
        kern,
        out_shape=jax.ShapeDtypeStruct((n_blocks * TM, d), F32),
        grid=(n_blocks,),
        in_specs=in_specs,
        out_specs=pl.BlockSpec((TM, d), lambda i: (i, 0)),
        compiler_params=_cparams(1, VMEM_LIMIT),
        name="resid_matmul",
    )(h, modtab, *a_list, *w_list)


def _dwconv_kernel(cur_ref, prev_ref, next_ref, w_ref, b_ref, o_ref, *, chunks_per_batch, ctx_chunks):
    c = pl.program_id(0) % chunks_per_batch
    is_start = jnp.logical_or(c == 0, c == ctx_chunks)
    is_end = jnp.logical_or(c == ctx_chunks - 1, c == chunks_per_batch - 1)
    half = SSM_CONV_W // 2
    hr = HALO_ROWS
    cur = cur_ref[...]
    prev = jnp.where(is_start, jnp.zeros_like(prev_ref[...]), prev_ref[...])
    nxt = jnp.where(is_end, jnp.zeros_like(next_ref[...]), next_ref[...])
    ext = jnp.concatenate([prev, cur, nxt], axis=0)
    ri = lax.broadcasted_iota(I32, (SSD_CHUNK, SSD_CHUNK + 2 * hr), 0)
    ci = lax.broadcasted_iota(I32, (SSD_CHUNK, SSD_CHUNK + 2 * hr), 1)
    acc = b_ref[...] + w_ref[half:half + 1, :] * cur.astype(F32)
    for tap in range(SSM_CONV_W):
        if tap != half:
            shift = (ci == ri + (hr - half + tap)).astype(BF16)
            acc = acc + w_ref[tap:tap + 1, :] * jnp.dot(shift, ext, preferred_element_type=F32)
    o_ref[...] = _silu(acc).astype(o_ref.dtype)


def dwconv_silu(p, col_block, width, w, b, *, chunks_per_batch, ctx_chunks):
    t = p.shape[0]
    n_chunks = t // SSD_CHUNK
    sub = SSD_CHUNK // HALO_ROWS
    kern = functools.partial(_dwconv_kernel, chunks_per_batch=chunks_per_batch, ctx_chunks=ctx_chunks)
    return pl.pallas_call(
        kern,
        out_shape=jax.ShapeDtypeStruct((t, width), BF16),
        grid=(n_chunks,),
        in_specs=[pl.BlockSpec((SSD_CHUNK, width), lambda i: (i, col_block)),
                  pl.BlockSpec((HALO_ROWS, width), lambda i: (jnp.maximum(i * sub - 1, 0), col_block)),
                  pl.BlockSpec((HALO_ROWS, width),
                               lambda i: (jnp.minimum((i + 1) * sub, t // HALO_ROWS - 1), col_block)),
                  pl.BlockSpec((SSM_CONV_W, width), lambda i: (0, 0)),
                  pl.BlockSpec((1, width), lambda i: (0, 0))],
        out_specs=pl.BlockSpec((SSD_CHUNK, width), lambda i: (i, 0)),
        compiler_params=_cparams(1),
        name="dwconv_silu",
    )(p, p, p, w, b.reshape(1, width))


def _ssd_kernel(*refs, reverse, final):
    if final:
        (x_ref, bc_ref, dt_ref, dtb_ref, a_ref, z_ref, yprev_ref, dskip_ref, ng_ref, y_ref, state_ref) = refs
    else:
        (x_ref, bc_ref, dt_ref, dtb_ref, a_ref, y_ref, state_ref) = refs
    ch = SSD_CHUNK
    n = SSM_STATE
    inner = SSM_INNER
    gw = inner // SSM_GROUPS
    col0 = SSM_HEADS if reverse else 0

    @pl.when(pl.program_id(1) == 0)
    def _():
        state_ref[...] = jnp.zeros_like(state_ref)

    x = x_ref[...].astype(F32)
    bc = bc_ref[...]
    raw = dt_ref[...] + dtb_ref[...]
    dt = jnp.maximum(raw, 0.0) + jnp.log(1.0 + jnp.exp(-jnp.abs(raw)))
    adt = dt * a_ref[...]

    ri = lax.broadcasted_iota(I32, (ch, ch), 0)
    ci = lax.broadcasted_iota(I32, (ch, ch), 1)
    if reverse:
        tri = ci >= ri
        tri_t = ri >= ci
    else:
        tri = ci <= ri
        tri_t = ri <= ci
    acum = _dot_01_left(tri.astype(BF16), adt)
    acum_t = _dot_01_right(adt.T, tri_t.astype(BF16))
    last = 0 if reverse else ch - 1
    total = acum[last:last + 1, :]

    hrow = lax.broadcasted_iota(I32, (LANES, inner), 0)
    hcol = lax.broadcasted_iota(I32, (LANES, inner), 1) // SSM_HEAD_DIM
    sel = (hrow == hcol + col0).astype(BF16)
    stack = jnp.concatenate([dt, jnp.exp(acum), jnp.exp(total - acum),
                             jnp.broadcast_to(jnp.exp(total), (8, LANES))], axis=0)
    expanded = _dot_01_right(stack, sel)
    dt_x = expanded[0:ch]
    ea_x = expanded[ch:2 * ch]
    ds_x = expanded[2 * ch:3 * ch]
    tot_x = expanded[3 * ch:3 * ch + 1]

    xd = x * dt_x
    xw = (xd * ds_x).astype(BF16)
    xd_b = xd.astype(BF16)
    state = state_ref[...]
    lane = lax.broadcasted_iota(I32, (ch, LANES), 1)
    first_half = lane < SSM_HEAD_DIM
    y_parts = []

    for g in range(SSM_GROUPS):
        b_g = bc[:, g * n:(g + 1) * n]
        c_g = bc[:, (SSM_GROUPS + g) * n:(SSM_GROUPS + g + 1) * n]
        cb = lax.dot_general(c_g, b_g, (((1,), (1,)), ((), ())), preferred_element_type=F32)
        gs = slice(g * gw, (g + 1) * gw)
        y_off = jnp.dot(c_g, state[:, gs].astype(BF16), preferred_element_type=F32) * ea_x[:, gs]
        b_t = b_g.astype(F32).T.astype(BF16)
        new_state = tot_x[:, gs] * state[:, gs] + jnp.dot(b_t, xw[:, gs], preferred_element_type=F32)
        state_ref[:, gs] = new_state
        for pair in range(gw // LANES):
            lanes = slice(g * gw + pair * LANES, g * gw + (pair + 1) * LANES)
            halves = []
            for sub in range(2):
                head = (g * gw + pair * LANES) // SSM_HEAD_DIM + sub
                cidx = col0 + head
                diff = acum[:, cidx:cidx + 1] - acum_t[cidx:cidx + 1, :]
                m_h = (cb * jnp.where(tri, jnp.exp(jnp.where(tri, diff, 0.0)), 0.0)).astype(BF16)
                halves.append(jnp.dot(m_h, xd_b[:, lanes], preferred_element_type=F32))
            y_parts.append(jnp.where(first_half, halves[0], halves[1]) + y_off[:, pair * LANES:(pair + 1) * LANES])

    y = jnp.concatenate(y_parts, axis=1)
    if final:
        y = y + yprev_ref[...] + x * dskip_ref[...]
        y = y * _silu(z_ref[...].astype(F32))
        ms = jnp.mean(y * y, axis=-1, keepdims=True)
        y = (y * lax.rsqrt(ms + NORM_EPS)) * ng_ref[...]
    y_ref[...] = y.astype(y_ref.dtype)


def ssd_scan(xc, bcc, dtr, dtb, a_row, *, batch, chunks_per_batch, ctx_chunks, reverse,
             z_src=None, z_col=None, y_prev=None, d_skip=None, norm_g=None):
    t, inner = xc.shape
    final = y_prev is not None
    npb = chunks_per_batch

    def chunk(b, s):
        if not reverse:
            c = s
        else:
            c = jnp.where(s < ctx_chunks, ctx_chunks - 1 - s, npb - 1 + ctx_chunks - s)
        return b * npb + c

    row = lambda b, s: (chunk(b, s), 0)
    const = lambda b, s: (0, 0)
    in_specs = [pl.BlockSpec((SSD_CHUNK, inner), row),
                pl.BlockSpec((SSD_CHUNK, bcc.shape[1]), row),
                pl.BlockSpec((SSD_CHUNK, LANES), row),
                pl.BlockSpec((1, LANES), const),
                pl.BlockSpec((1, LANES), const)]
    args = [xc, bcc, dtr, dtb, a_row]
    if final:
        in_specs += [pl.BlockSpec((SSD_CHUNK, inner), lambda b, s: (chunk(b, s), z_col)),
                     pl.BlockSpec((SSD_CHUNK, inner), row),
                     pl.BlockSpec((1, inner), const),
                     pl.BlockSpec((1, inner), const)]
        args += [z_src, y_prev, d_skip, norm_g]
    kern = functools.partial(_ssd_kernel, reverse=reverse, final=final)
    return pl.pallas_call(
        kern,
        out_shape=jax.ShapeDtypeStruct((t, inner), BF16 if final else F32),
        grid=(batch, npb),
        in_specs=in_specs,
        out_specs=pl.BlockSpec((SSD_CHUNK, inner), row),
        scratch_shapes=[pltpu.VMEM((SSM_STATE, inner), F32)],
        compiler_params=_cparams(2, VMEM_LIMIT),
        name="ssd_bwd" if reverse else "ssd_fwd",
    )(*args)


def _na_window_base(t, lat_rows):
    return jnp.minimum(jnp.clip(2 * t - NA_ROWS // 2, 0, lat_rows - NA_ROWS), lat_rows - NA_WIN_ROWS)


def _na_pair_config(t, lat_rows):
    clip = lambda v, lo, hi: max(lo, min(v, hi))
    starts = [clip(r - NA_ROWS // 2, 0, lat_rows - NA_ROWS) for r in (2 * t, 2 * t + 1)]
    base = min(starts[0], lat_rows - NA_WIN_ROWS)
    return (starts[0] - base, starts[1] - base, 2 * t - starts[0], 2 * t + 1 - starts[1])


NA_CONFIGS = ((0, 1, 4, 4), (0, 0, 0, 1), (0, 0, 2, 3), (2, 2, 4, 5), (2, 2, 6, 7), (99, 99, 0, 0))


def _na_config_index(t, n_pairs):
    return jnp.where(t < 0, 5, jnp.where(t == 0, 1, jnp.where(t == 1, 2, jnp.where(
        t == n_pairs - 2, 3, jnp.where(t == n_pairs - 1, 4, 0)))))


def _na_bias_kernel(rpb_ref, o_ref, toe_ref):
    head = pl.program_id(0)
    w = GRID_W
    ki = lax.broadcasted_iota(I32, (w, w), 0)
    qi = lax.broadcasted_iota(I32, (w, w), 1)
    cstart = jnp.clip(qi - NA_COLS // 2, 0, w - NA_COLS)
    col_ok = jnp.logical_and(ki >= cstart, ki < cstart + NA_COLS)
    col_idx = jnp.clip(ki - qi + NA_COLS - 1, 0, 2 * NA_COLS - 2)
    for rho in range(2 * NA_ROWS - 1):
        tile = jnp.zeros((w, w), F32)
        for tcol in range(2 * NA_COLS - 1):
            tile = jnp.where(col_idx == tcol, rpb_ref[head, rho, tcol], tile)
        toe_ref[rho] = jnp.where(col_ok, tile, NEG)
    masked = jnp.full((w, w), NEG, F32)
    for cfg, offsets in enumerate(NA_CONFIGS):
        for j in range(NA_WIN_ROWS):
            for qr in range(NA_PAIR):
                rel = j - offsets[qr]
                inside = 0 <= rel < NA_ROWS
                tile = toe_ref[rel - offsets[NA_PAIR + qr] + NA_ROWS - 1] if inside else masked
                o_ref[cfg, 0, j * w:(j + 1) * w, qr * w:(qr + 1) * w] = tile


def na_bias_tables(rpb, lat_rows):
    heads = rpb.shape[0]
    n_pairs = lat_rows // NA_PAIR
    assert lat_rows % NA_PAIR == 0 and n_pairs >= 5
    for t in range(n_pairs):
        idx = 1 if t == 0 else 2 if t == 1 else 3 if t == n_pairs - 2 else 4 if t == n_pairs - 1 else 0
        assert _na_pair_config(t, lat_rows) == NA_CONFIGS[idx]
    n_cfg = len(NA_CONFIGS)
    return pl.pallas_call(
        _na_bias_kernel,
        out_shape=jax.ShapeDtypeStruct((n_cfg, heads, NA_WIN_ROWS * GRID_W, NA_PAIR * GRID_W), F32),
        grid=(heads,),
        in_specs=[pl.BlockSpec(memory_space=pltpu.SMEM)],
        out_specs=pl.BlockSpec((n_cfg, 1, NA_WIN_ROWS * GRID_W, NA_PAIR * GRID_W), lambda h: (0, h, 0, 0)),
        scratch_shapes=[pltpu.VMEM((2 * NA_ROWS - 1, GRID_W, GRID_W), F32)],
        compiler_params=_cparams(1),
        name="na_bias_tables",
    )(rpb)


def _transpose_kernel(x_ref, o_ref):
    for tile in range(x_ref.shape[1] // LANES):
        lanes = slice(tile * LANES, (tile + 1) * LANES)
        o_ref[lanes, :] = x_ref[:, lanes].astype(F32).T.astype(o_ref.dtype)


def transpose_cols(p, col_block, width):
    t = p.shape[0]
    return pl.pallas_call(
        _transpose_kernel,
        out_shape=jax.ShapeDtypeStruct((width, t), p.dtype),
        grid=(t // TM,),
        in_specs=[pl.BlockSpec((TM, width), lambda i: (i, col_block))],
        out_specs=pl.BlockSpec((width, TM), lambda i: (0, i)),
        compiler_params=_cparams(1),
        name="transpose_cols",
    )(p)


def _na_kernel(*refs):
    nb = NA_WIN_ROWS * GRID_W // LANES
    q_ref = refs[0]
    k_refs = refs[1:1 + nb]
    kc_ref = refs[1 + nb]
    vt_refs = refs[2 + nb:2 + 2 * nb]
    vct_ref, bias_ref, o_ref, s_ref, p_ref = refs[2 + 2 * nb:]
    nq = q_ref.shape[0]
    n_lat = NA_WIN_ROWS * GRID_W
    n_keys = n_lat + kc_ref.shape[0]
    scale = HEAD_DIM ** -0.5
    top = lax.broadcasted_iota(I32, (LANES, nq), 0) < HEAD_DIM
    for pair in range(NA_HEADS // 2):
        lanes = slice(pair * LANES, (pair + 1) * LANES)
        qt = (q_ref[:, lanes] * scale).astype(F32).T
        kk = jnp.concatenate([r[:, lanes] for r in k_refs] + [kc_ref[:, lanes]], axis=0)
        for sub in range(2):
            keep = top if sub == 0 else jnp.logical_not(top)
            qz = jnp.where(keep, qt, 0.0).astype(BF16)
            s_ref[2 * pair + sub] = jnp.dot(kk, qz, preferred_element_type=F32)
    for head in range(NA_HEADS):
        s_lat = s_ref[head, 0:n_lat, :] + bias_ref[0, head]
        s_ctx = s_ref[head, n_lat:, :]
        m = jnp.maximum(jnp.max(s_lat, axis=0, keepdims=True), jnp.max(s_ctx, axis=0, keepdims=True))
        p_ref[head, 0:n_lat, :] = jnp.exp(s_lat - m).astype(BF16)
        p_ref[head, n_lat:, :] = jnp.exp(s_ctx - m).astype(BF16)
    ones = jnp.ones((ONES_ROWS, n_keys), BF16)
    for pair in range(NA_HEADS // 2):
        halves = []
        for sub in range(2):
            head = 2 * pair + sub
            rows = slice(head * HEAD_DIM, (head + 1) * HEAD_DIM)
            vt = jnp.concatenate([r[rows, :] for r in vt_refs] + [vct_ref[rows, :]], axis=1)
            pv = jnp.dot(jnp.concatenate([vt, ones], axis=0), p_ref[head], preferred_element_type=F32)
            halves.append(pv[0:HEAD_DIM] * (1.0 / pv[HEAD_DIM:HEAD_DIM + 1]))
        o_t = jnp.concatenate(halves, axis=0)
        o_ref[:, pair * LANES:(pair + 1) * LANES] = o_t.T.astype(o_ref.dtype)


def na_attention(p, vt_all, bias, *, batch, lat_rows, ctx_len, q_col, k_col):
    t = p.shape[0]
    width = NA_HEADS * HEAD_DIM
    nq = NA_PAIR * GRID_W
    assert nq == LANES and ctx_len % nq == 0
    nb = NA_WIN_ROWS * GRID_W // LANES
    ctx_steps = ctx_len // nq
    n_pairs = lat_rows // NA_PAIR
    steps = ctx_steps + n_pairs
    blocks_ctx = steps * nq // ctx_len

    def key_block(b, st, j):
        base = _na_window_base(st - ctx_steps, lat_rows)
        return b * steps + ctx_steps + base // NA_PAIR + j

    in_specs = [pl.BlockSpec((nq, width), lambda b, st: (b * steps + st, q_col))]
    in_specs += [pl.BlockSpec((LANES, width), functools.partial(lambda b, st, j: (key_block(b, st, j), k_col), j=j))
                 for j in range(nb)]
    in_specs += [pl.BlockSpec((ctx_len, width), lambda b, st: (b * blocks_ctx, k_col))]
    in_specs += [pl.BlockSpec((width, LANES), functools.partial(lambda b, st, j: (0, key_block(b, st, j)), j=j))
                 for j in range(nb)]
    in_specs += [pl.BlockSpec((width, ctx_len), lambda b, st: (0, b * blocks_ctx)),
                 pl.BlockSpec((1, NA_HEADS, NA_WIN_ROWS * GRID_W, nq),
                              lambda b, st: (_na_config_index(st - ctx_steps, n_pairs), 0, 0, 0))]
    args = [p] * (2 + nb) + [vt_all] * (1 + nb) + [bias]
    n_keys = NA_WIN_ROWS * GRID_W + ctx_len
    return pl.pallas_call(
        _na_kernel,
        out_shape=jax.ShapeDtypeStruct((t, width), BF16),
        grid=(batch, steps),
        in_specs=in_specs,
        out_specs=pl.BlockSpec((nq, width), lambda b, st: (b * steps + st, 0)),
        scratch_shapes=[pltpu.VMEM((NA_HEADS, n_keys, nq), F32),
                        pltpu.VMEM((NA_HEADS, n_keys, nq), BF16)],
        compiler_params=_cparams(2, VMEM_LIMIT),
        name="na_attention",
    )(*args)


def _group_mean_sq(x):
    gi = lax.broadcasted_iota(I32, (LANES, LANES), 0) // HEAD_DIM
    gj = lax.broadcasted_iota(I32, (LANES, LANES), 1) // HEAD_DIM
    ones = (gi == gj).astype(BF16)
    return _dot_01_right(x * x, ones) * (1.0 / HEAD_DIM)


def _rope(x, cos_f, sin_s):
    lane = lax.broadcasted_iota(I32, x.shape, 1)
    low = (lane % HEAD_DIM) < HEAD_DIM // 2
    partner = jnp.where(low, pltpu.roll(x, LANES - HEAD_DIM // 2, 1), pltpu.roll(x, HEAD_DIM // 2, 1))
    return x * cos_f + partner * sin_s


def _gqa_prep_kernel(p_ref, cos_ref, sin_ref, qg_ref, kg_ref, qt_ref, k_ref, vt_ref, *, ctx_blocks):
    j = pl.program_id(1)
    is_lat = j >= ctx_blocks
    q_w = GQA_HEADS * HEAD_DIM
    kv_w = GQA_KV_HEADS * HEAD_DIM
    cos_f = jnp.where(is_lat, cos_ref[...], 1.0)
    sin_s = jnp.where(is_lat, sin_ref[...], 0.0)

    def normed(cols, gain):
        x = p_ref[:, cols].astype(F32)
        y = x * lax.rsqrt(_group_mean_sq(x) + NORM_EPS)
        return _rope(y * gain, cos_f, sin_s)

    for tile in range(kv_w // LANES):
        cols = slice(q_w + tile * LANES, q_w + (tile + 1) * LANES)
        k_ref[:, tile * LANES:(tile + 1) * LANES] = normed(cols, kg_ref[...]).astype(k_ref.dtype)
    v = p_ref[:, q_w + kv_w:q_w + 2 * kv_w].astype(F32)
    vt_ref[0] = v.T.astype(vt_ref.dtype)

    @pl.when(is_lat)
    def _():
        scale = HEAD_DIM ** -0.5 * math.log2(math.e)
        for tile in range(q_w // LANES):
            cols = slice(tile * LANES, (tile + 1) * LANES)
            q = normed(cols, qg_ref[...]) * scale
            qt_ref[0, cols, :] = q.T.astype(qt_ref.dtype)


def gqa_prep(p1, cos_f, sin_s, qg, kg, *, batch, blocks_per_batch, ctx_blocks):
    t, n = p1.shape
    q_w = GQA_HEADS * HEAD_DIM
    kv_w = GQA_KV_HEADS * HEAD_DIM
    lb = blocks_per_batch * TM
    lat = (blocks_per_batch - ctx_blocks) * TM
    kern = functools.partial(_gqa_prep_kernel, ctx_blocks=ctx_blocks)
    lat_blk = lambda b, j: jnp.maximum(j - ctx_blocks, 0)
    return pl.pallas_call(
        kern,
        out_shape=(jax.ShapeDtypeStruct((batch, q_w, lat), BF16),
                   jax.ShapeDtypeStruct((t, kv_w), BF16),
                   jax.ShapeDtypeStruct((batch, kv_w, lb), BF16)),
        grid=(batch, blocks_per_batch),
        in_specs=[pl.BlockSpec((TM, n), lambda b, j: (b * blocks_per_batch + j, 0)),
                  pl.BlockSpec((TM, LANES), lambda b, j: (lat_blk(b, j), 0)),
                  pl.BlockSpec((TM, LANES), lambda b, j: (lat_blk(b, j), 0)),
                  pl.BlockSpec((1, LANES), lambda b, j: (0, 0)),
                  pl.BlockSpec((1, LANES), lambda b, j: (0, 0))],
        out_specs=(pl.BlockSpec((1, q_w, TM), lambda b, j: (b, 0, lat_blk(b, j))),
                   pl.BlockSpec((TM, kv_w), lambda b, j: (b * blocks_per_batch + j, 0)),
                   pl.BlockSpec((1, kv_w, TM), lambda b, j: (b, 0, j))),
        compiler_params=_cparams(2, VMEM_LIMIT),
        name="gqa_prep",
    )(p1, cos_f, sin_s, qg, kg)


def _gqa_attn_kernel(qt_ref, k_ref, vt_ref, o_ref, qz_ref, s_ref, bmax_ref, m_ref, l_ref, acc_ref, *, tk, n_kblocks):
    kv = pl.program_id(1)
    tq = qt_ref.shape[2]
    group = GQA_HEADS // GQA_KV_HEADS
    second = (kv % 2) == 1
    for hq in range(group):
        q_h = qt_ref[0, hq * HEAD_DIM:(hq + 1) * HEAD_DIM, :]
        zero = jnp.zeros_like(q_h)
        qz_ref[0:HEAD_DIM, hq * tq:(hq + 1) * tq] = jnp.where(second, zero, q_h)
        qz_ref[HEAD_DIM:2 * HEAD_DIM, hq * tq:(hq + 1) * tq] = jnp.where(second, q_h, zero)
    m_ref[...] = jnp.full_like(m_ref, NEG)
    l_ref[...] = jnp.zeros_like(l_ref)
    acc_ref[...] = jnp.zeros_like(acc_ref)

    def scores(kb, slot):
        off = pl.multiple_of(kb * tk, tk)
        s = jnp.dot(k_ref[0, pl.ds(off, tk), :], qz_ref[...], preferred_element_type=F32)
        s_ref[slot] = s
        bmax_ref[slot] = jnp.max(s, axis=0, keepdims=True)

    def update(kb, slot):
        off = pl.multiple_of(kb * tk, tk)
        s = s_ref[slot]
        m_old = m_ref[...]
        m_new = jnp.maximum(m_old, bmax_ref[slot])
        alpha = jnp.exp2(m_old - m_new)
        p = jnp.exp2(s - m_new).astype(BF16)
        v_aug = jnp.concatenate([vt_ref[0, :, pl.ds(off, tk)], jnp.ones((ONES_ROWS, tk), BF16)], axis=0)
        pv = jnp.dot(v_aug, p, preferred_element_type=F32)
        acc_ref[...] = alpha * acc_ref[...] + pv[0:HEAD_DIM]
        l_ref[...] = alpha * l_ref[...] + pv[HEAD_DIM:HEAD_DIM + 1]
        m_ref[...] = m_new

    per_trip = GQA_BLOCKS_PER_TRIP
    scores(0, 0)
    trips = (n_kblocks - 1) // per_trip

    def one_trip(i, carry):
        kb = per_trip * i
        for j in range(per_trip):
            scores(kb + j + 1, (j + 1) % 2)
            update(kb + j, j % 2)
        return carry

    lax.fori_loop(0, trips, one_trip, 0)
    base = trips * per_trip
    for j in range(n_kblocks - base):
        if base + j + 1 < n_kblocks:
            scores(base + j + 1, (j + 1) % 2)
        update(base + j, j % 2)

    inv_l = 1.0 / l_ref[...]
    for pair in range(group // 2):
        lo, hi = 2 * pair * tq, (2 * pair + 1) * tq
        o_t = jnp.concatenate([acc_ref[:, lo:lo + tq] * inv_l[:, lo:lo + tq],
                               acc_ref[:, hi:hi + tq] * inv_l[:, hi:hi + tq]], axis=0)
        o_ref[:, pair * LANES:(pair + 1) * LANES] = o_t.T.astype(o_ref.dtype)


def gqa_attention(qt, k, vt, *, tq=512, tk=256):
    batch, q_w, lat = qt.shape
    lb = k.shape[1]
    group = GQA_HEADS // GQA_KV_HEADS
    gw = group * HEAD_DIM
    tq = min(tq, lat)
    kern = functools.partial(_gqa_attn_kernel, tk=tk, n_kblocks=lb // tk)
    return pl.pallas_call(
        kern,
        out_shape=jax.ShapeDtypeStruct((batch * lat, q_w), BF16),
        grid=(batch, GQA_KV_HEADS, lat // tq),
        in_specs=[pl.BlockSpec((1, gw, tq), lambda b, h, i: (b, h, i)),
                  pl.BlockSpec((1, lb, LANES), lambda b, h, i: (b, 0, h // 2)),
                  pl.BlockSpec((1, HEAD_DIM, lb), lambda b, h, i: (b, h, 0))],
        out_specs=pl.BlockSpec((tq, gw), lambda b, h, i: (b * (lat // tq) + i, h)),
        scratch_shapes=[pltpu.VMEM((2 * HEAD_DIM, group * tq), BF16),
                        pltpu.VMEM((2, tk, group * tq), F32),
                        pltpu.VMEM((2, 1, group * tq), F32),
                        pltpu.VMEM((1, group * tq), F32),
                        pltpu.VMEM((1, group * tq), F32),
                        pltpu.VMEM((HEAD_DIM, group * tq), F32)],
        compiler_params=_cparams(3, VMEM_LIMIT),
        name="gqa_attention",
    )(qt, k, vt)


def _router_kernel(h_ref, g_ref, mod_ref, rw_ref, rb_ref, xn_ref, eidx_ref, wts_ref, rank_ref, cnt_ref, run_ref):
    @pl.when(pl.program_id(0) == 0)
    def _():
        run_ref[...] = jnp.zeros_like(run_ref)

    xn = _norm_mod(h_ref[...], g_ref[...], mod_ref[0, 3:4, :], mod_ref[0, 4:5, :])
    xn_ref[...] = _pack_bf16_pairs(xn)
    tm = xn.shape[0]
    logits = lax.dot_general(rw_ref[...], xn, (((1,), (1,)), ((), ())),
                             precision=HIGHEST, preferred_element_type=F32)
    s = jax.nn.sigmoid(logits)
    sb = s + rb_ref[...]
    row = lambda a, e: a[e:e + 1, :]

    best = None
    for g in range(N_GROUPS):
        a = [row(sb, g * EXPERTS_PER_GROUP + j) for j in range(EXPERTS_PER_GROUP)]
        score = a[0] + a[1]
        for i in range(EXPERTS_PER_GROUP):
            for j in range(i + 1, EXPERTS_PER_GROUP):
                if (i, j) != (0, 1):
                    score = jnp.maximum(score, a[i] + a[j])
        if best is None:
            best, grp = score, jnp.zeros((1, tm), I32)
        else:
            better = score > best
            best = jnp.where(better, score, best)
            grp = jnp.where(better, g, grp)

    biased, plain = [], []
    for j in range(EXPERTS_PER_GROUP):
        vb = row(sb, j)
        vp = row(s, j)
        for g in range(1, N_GROUPS):
            pick = grp == g
            vb = jnp.where(pick, row(sb, g * EXPERTS_PER_GROUP + j), vb)
            vp = jnp.where(pick, row(s, g * EXPERTS_PER_GROUP + j), vp)
        biased.append(vb)
        plain.append(vp)

    def argmax_first(vals):
        top, idx = vals[0], jnp.zeros((1, tm), I32)
        for j in range(1, len(vals)):
            better = vals[j] > top
            top = jnp.where(better, vals[j], top)
            idx = jnp.where(better, j, idx)
        return idx

    i1 = argmax_first(biased)
    i2 = argmax_first([jnp.where(i1 == j, -jnp.inf, biased[j]) for j in range(EXPERTS_PER_GROUP)])

    def pick_plain(idx):
        out = plain[0]
        for j in range(1, EXPERTS_PER_GROUP):
            out = jnp.where(idx == j, plain[j], out)
        return out

    w1, w2 = pick_plain(i1), pick_plain(i2)
    wsum = w1 + w2
    e1 = grp * EXPERTS_PER_GROUP + i1
    e2 = grp * EXPERTS_PER_GROUP + i2
    eidx_ref[0] = jnp.concatenate([e1, e2], axis=0)
    wts_ref[...] = jnp.concatenate([w1 / wsum, w2 / wsum], axis=0)

    erow = lax.broadcasted_iota(I32, (N_EXPERTS, tm), 0)
    hit1 = erow == e1
    hit2 = erow == e2
    onehot = jnp.logical_or(hit1, hit2)
    si = lax.broadcasted_iota(I32, (tm, tm), 0)
    ti = lax.broadcasted_iota(I32, (tm, tm), 1)
    before = jnp.dot(onehot.astype(BF16), (si < ti).astype(BF16), preferred_element_type=F32)
    before = before + run_ref[...][:, 0:1]
    r1 = jnp.sum(jnp.where(hit1, before, 0.0), axis=0, keepdims=True)
    r2 = jnp.sum(jnp.where(hit2, before, 0.0), axis=0, keepdims=True)
    rank_ref[0] = jnp.concatenate([r1, r2], axis=0).astype(I32)
    totals = run_ref[...] + jnp.sum(onehot.astype(F32), axis=1, keepdims=True)
    run_ref[...] = totals
    cnt_ref[...] = totals


def moe_router(h, gain, modtab, rw_t, rb, *, mod_index):
    t, d = h.shape
    return pl.pallas_call(
        _router_kernel,
        out_shape=(jax.ShapeDtypeStruct((t, d // 2), I32),
                   jax.ShapeDtypeStruct((t // TM, 2, TM), I32),
                   jax.ShapeDtypeStruct((2, t), F32),
                   jax.ShapeDtypeStruct((t // TM, 2, TM), I32),
                   jax.ShapeDtypeStruct((N_EXPERTS, LANES), F32)),
        grid=(t // TM,),
        in_specs=[pl.BlockSpec((TM, d), lambda i: (i, 0)),
                  pl.BlockSpec((1, d), lambda i: (0, 0)),
                  pl.BlockSpec((1, 6, d), lambda i: (mod_index(i), 0, 0)),
                  pl.BlockSpec((N_EXPERTS, d), lambda i: (0, 0)),
                  pl.BlockSpec((N_EXPERTS, 1), lambda i: (0, 0))],
        out_specs=(pl.BlockSpec((TM, d // 2), lambda i: (i, 0)),
                   pl.BlockSpec((1, 2, TM), lambda i: (i, 0, 0)),
                   pl.BlockSpec((2, TM), lambda i: (0, i)),
                   pl.BlockSpec((1, 2, TM), lambda i: (i, 0, 0)),
                   pl.BlockSpec((N_EXPERTS, LANES), lambda i: (0, 0))),
        scratch_shapes=[pltpu.VMEM((N_EXPERTS, LANES), F32)],
        compiler_params=_cparams(1, VMEM_LIMIT),
        name="moe_router",
    )(h, gain.reshape(1, d), modtab, rw_t, rb)


def _row_copy(src_ref, src_row, dst_ref, dst_row, sem):
    return pltpu.make_async_copy(src_ref.at[pl.ds(src_row, 1)], dst_ref.at[pl.ds(dst_row, 1)], sem)


def _slots_kernel(start_ref, eidx_ref, rank_ref, slot_ref):
    e = eidx_ref[...]
    slot = rank_ref[...]
    for ex in range(N_EXPERTS):
        slot = slot + jnp.where(e == ex, start_ref[ex], 0)
    slot_ref[...] = slot


def moe_slots(pad_start, eidx3, rank3):
    return pl.pallas_call(
        _slots_kernel,
        out_shape=jax.ShapeDtypeStruct(eidx3.shape, I32),
        in_specs=[pl.BlockSpec(memory_space=pltpu.SMEM),
                  pl.BlockSpec(memory_space=pltpu.VMEM), pl.BlockSpec(memory_space=pltpu.VMEM)],
        out_specs=pl.BlockSpec(memory_space=pltpu.VMEM),
        name="moe_slots",
    )(pad_start, eidx3, rank3)


def _dispatch_kernel(slot_ref, x_ref, xs_in_ref, xs_ref, sem):
    del xs_in_ref
    rows = x_ref.shape[0]
    for r in range(rows):
        for k in range(2):
            _row_copy(x_ref, r, xs_ref, slot_ref[0, k, r], sem).start()
    for _ in range(2 * rows):
        _row_copy(x_ref, 0, xs_ref, 0, sem).wait()


def moe_dispatch(xn, slot3, n_slots):
    t, d = xn.shape
    xs0 = jnp.zeros((n_slots, d), xn.dtype)
    return pl.pallas_call(
        _dispatch_kernel,
        out_shape=jax.ShapeDtypeStruct((n_slots, d), xn.dtype),
        grid=(t // DMA_ROWS,),
        in_specs=[pl.BlockSpec((1, 2, DMA_ROWS), lambda i: (i, 0, 0), memory_space=pltpu.SMEM),
                  pl.BlockSpec((DMA_ROWS, d), lambda i: (i, 0)),
                  pl.BlockSpec(memory_space=pl.ANY)],
        out_specs=pl.BlockSpec(memory_space=pl.ANY),
        scratch_shapes=[pltpu.SemaphoreType.DMA(())],
        input_output_aliases={2: 0},
        compiler_params=_cparams(1),
        name="moe_dispatch",
    )(slot3, xn, xs0)


def _expert_kernel(blk_e_ref, nused_ref, x_ref, wg_ref, wu_ref, wd_ref, y_ref, wg_b, wu_b, wd_b):
    i = pl.program_id(0)

    @pl.when(i < nused_ref[0])
    def _():
        @pl.when(jnp.logical_or(i == 0, blk_e_ref[i] != blk_e_ref[jnp.maximum(i - 1, 0)]))
        def _():
            wg_b[...] = wg_ref[0, 0].astype(BF16)
            wu_b[...] = wu_ref[0, 0].astype(BF16)
            wd_b[...] = wd_ref[0, 0].astype(BF16)

        x = _unpack_bf16_pairs(x_ref[...]).astype(BF16)
        gate = jnp.dot(x, wg_b[...], preferred_element_type=F32)
        up = jnp.dot(x, wu_b[...], preferred_element_type=F32)
        hid = (_silu(gate) * up).astype(BF16)
        y_ref[...] = _pack_bf16_pairs(jnp.dot(hid, wd_b[...], preferred_element_type=F32))

    @pl.when(pl.program_id(0) >= nused_ref[0])
    def _():
        y_ref[...] = jnp.zeros_like(y_ref)


def moe_experts(xs, blk_e, nused, wg, wu, wd, layer):
    n_slots, dw = xs.shape
    d, de = wg.shape[2], wg.shape[3]
    grid_spec = pltpu.PrefetchScalarGridSpec(
        num_scalar_prefetch=2,
        grid=(n_slots // MOE_BM,),
        in_specs=[pl.BlockSpec((MOE_BM, dw), lambda i, be, nu: (i, 0)),
                  pl.BlockSpec((1, 1, d, de), lambda i, be, nu: (layer, be[i], 0, 0)),
                  pl.BlockSpec((1, 1, d, de), lambda i, be, nu: (layer, be[i], 0, 0)),
                  pl.BlockSpec((1, 1, de, d), lambda i, be, nu: (layer, be[i], 0, 0))],
        out_specs=pl.BlockSpec((MOE_BM, dw), lambda i, be, nu: (i, 0)),
        scratch_shapes=[pltpu.VMEM((d, de), BF16), pltpu.VMEM((d, de), BF16), pltpu.VMEM((de, d), BF16)],
    )
    return pl.pallas_call(
        _expert_kernel,
        out_shape=jax.ShapeDtypeStruct((n_slots, dw), I32),
        grid_spec=grid_spec,
        compiler_params=_cparams(1, VMEM_LIMIT),
        name="moe_experts",
    )(blk_e, nused, xs, wg, wu, wd)


def _combine_kernel(slot_ref, slot_nxt_ref, h_ref, mod_ref, w_ref, fg_ref, yb_ref, o_ref, buf_ref, sems, *, final):
    step = pl.program_id(0)
    rows = h_ref.shape[0]

    def start_gathers(s_ref, half):
        for r in range(rows):
            for k in range(2):
                _row_copy(yb_ref, s_ref[0, k, r], buf_ref.at[half, k], r, sems.at[half]).start()

    cur = step % 2

    @pl.when(step == 0)
    def _():
        start_gathers(slot_ref, 0)

    @pl.when(step + 1 < pl.num_programs(0))
    def _():
        start_gathers(slot_nxt_ref, 1 - cur)

    for _ in range(rows):
        for k in range(2):
            _row_copy(yb_ref, 0, buf_ref.at[cur, k], 0, sems.at[cur]).wait()
    f = (w_ref[:, 0:1] * _unpack_bf16_pairs(buf_ref[cur, 0])
         + w_ref[:, 1:2] * _unpack_bf16_pairs(buf_ref[cur, 1]))
    h = h_ref[...] + mod_ref[0, 5:6, :] * f
    if final:
        ms = jnp.mean(h * h, axis=-1, keepdims=True)
        h = (h * lax.rsqrt(ms + NORM_EPS)) * fg_ref[...]
    o_ref[...] = h


def moe_combine(h, modtab, wts_t, final_g, yb, slot3, *, mod_index, final):
    t, d = h.shape
    kern = functools.partial(_combine_kernel, final=final)
    n_steps = t // DMA_ROWS
    idx_spec = pl.BlockSpec((1, 2, DMA_ROWS), lambda i: (i, 0, 0), memory_space=pltpu.SMEM)
    nxt_spec = pl.BlockSpec((1, 2, DMA_ROWS), lambda i: (jnp.minimum(i + 1, n_steps - 1), 0, 0),
                            memory_space=pltpu.SMEM)
    return pl.pallas_call(
        kern,
        out_shape=jax.ShapeDtypeStruct((t, d), F32),
        grid=(n_steps,),
        in_specs=[idx_spec, nxt_spec,
                  pl.BlockSpec((DMA_ROWS, d), lambda i: (i, 0)),
                  pl.BlockSpec((1, 6, d), lambda i: (mod_index(i), 0, 0)),
                  pl.BlockSpec((DMA_ROWS, 2), lambda i: (i, 0)),
                  pl.BlockSpec((1, d), lambda i: (0, 0)),
                  pl.BlockSpec(memory_space=pl.ANY)],
        out_specs=pl.BlockSpec((DMA_ROWS, d), lambda i: (i, 0)),
        scratch_shapes=[pltpu.VMEM((2, 2, DMA_ROWS, yb.shape[1]), yb.dtype), pltpu.SemaphoreType.DMA((2,))],
        compiler_params=_cparams(1, VMEM_LIMIT),
        name="moe_combine",
    )(slot3, slot3, h, modtab, wts_t, final_g.reshape(1, d), yb)


def moe_layer(h, norm_g, modtab, rw_t, rb, wg, wu, wd, layer, final_g, *, mod_index, final):
    t, d = h.shape
    xn, eidx3, wts, rank3, counts = moe_router(h, norm_g, modtab, rw_t, rb, mod_index=mod_index)
    n_blocks = (2 * t) // MOE_BM + N_EXPERTS
    n_slots = n_blocks * MOE_BM
    cnt = counts[:, 0].astype(I32)
    padded = (cnt + MOE_BM - 1) // MOE_BM * MOE_BM
    pad_end = jnp.cumsum(padded)
    pad_start = (pad_end - padded).astype(I32)
    blk_start = jnp.arange(n_blocks, dtype=I32) * MOE_BM
    blk_e = jnp.minimum(jnp.sum((pad_end[None, :] <= blk_start[:, None]).astype(I32), axis=1), N_EXPERTS - 1)
    nused = (pad_end[-1] // MOE_BM).reshape(1).astype(I32)
    slot3 = moe_slots(pad_start, eidx3, rank3)
    xs = moe_dispatch(xn, slot3, n_slots)
    yb = moe_experts(xs, blk_e.astype(I32), nused, wg, wu, wd, layer)
    return moe_combine(h, modtab, wts.T, final_g, yb, slot3, mod_index=mod_index, final=final)


def _rope_tables(lat):
    t = jnp.arange(lat)
    row = (t // GRID_W).astype(F32)
    col = (t % GRID_W).astype(F32)
    n_freq = HEAD_DIM // 4
    inv = ROPE_THETA ** (-jnp.arange(n_freq, dtype=F32) / n_freq)
    ang = jnp.concatenate([row[:, None] * inv, col[:, None] * inv], axis=-1)
    cos, sin = jnp.cos(ang), jnp.sin(ang)
    cos_f = jnp.tile(cos, (1, LANES // (HEAD_DIM // 2)))
    sin_s = jnp.tile(jnp.concatenate([-sin, sin], axis=-1), (1, LANES // HEAD_DIM))
    return cos_f, sin_s


def kernel(x, c, ctx, c_ctx, w_mod, b_mod, norm1_g, norm2_g, final_g, ab_w_in, ab_w_out, ssm_conv_w, ssm_conv_b, ssm_A_log, ssm_dt_bias, ssm_D, ssm_norm_g, na_rpb, gqa_w_in, gqa_w_out, gqa_q_norm, gqa_k_norm, router_w, router_bias, moe_w_gate, moe_w_up, moe_w_down):
    bsz, lat, d = x.shape
    lc = ctx.shape[1]
    lb = lc + lat
    assert lat % TM == 0 and lc % TM == 0 and lat % GRID_W == 0 and lat // GRID_W >= NA_ROWS
    assert bsz + 1 <= 8 and w_mod.shape[0] == 2
    nb = lb // TM
    cb = lc // TM
    nlb = lat // TM

    def mod_comb(i):
        return jnp.where(i % nb < cb, bsz, i // nb)

    def mod_lat(i):
        return i // nlb

    inner, xbc_w = SSM_INNER, SSM_INNER + 2 * SSM_GROUPS * SSM_STATE
    w_in = ab_w_in[0]
    o_dt = inner + xbc_w
    o_q = o_dt + 2 * SSM_HEADS
    w_z, w_x, w_bc = w_in[:, :inner], w_in[:, inner:2 * inner], w_in[:, 2 * inner:o_dt]
    w_dt = w_in[:, o_dt:o_q]
    w_qkv = w_in[:, o_q:]
    w_in_b = jnp.concatenate([w_qkv, w_z, w_x, w_bc], axis=1).astype(BF16)
    w_dt_b = jnp.pad(w_dt, ((0, 0), (0, LANES - 2 * SSM_HEADS))).astype(BF16)
    conv_w, conv_b = ssm_conv_w[0], ssm_conv_b[0]
    pad32 = lambda v: jnp.pad(v.reshape(1, -1), ((0, 0), (0, LANES - 2 * SSM_HEADS)))
    dtb_row = pad32(ssm_dt_bias[0])
    a_row = pad32(-jnp.exp(ssm_A_log[0].astype(F32)))
    d_skip = jnp.repeat(ssm_D[0], SSM_HEAD_DIM).reshape(1, inner)
    w_out = ab_w_out[0].astype(BF16)
    w_gqa = gqa_w_in[0].astype(BF16)
    w_gqa_out = gqa_w_out[0].astype(BF16)
    qg = jnp.tile(gqa_q_norm[0], LANES // HEAD_DIM).reshape(1, LANES)
    kg = jnp.tile(gqa_k_norm[0], LANES // HEAD_DIM).reshape(1, LANES)
    rw_t = router_w.T
    rb = router_bias.reshape(N_EXPERTS, 1)
    wg, wu, wd = moe_w_gate, moe_w_up, moe_w_down

    cc = jnp.concatenate([c, c_ctx[None, :], jnp.zeros((8 - bsz - 1, d), F32)], axis=0)
    mod = modulation(cc, w_mod, b_mod)[:, :bsz + 1].reshape(2, bsz + 1, 6, d)

    h0 = jnp.concatenate([ctx, x], axis=1).reshape(bsz * lb, d)

    p, dtr = norm_matmul(h0, norm1_g[0], mod[0], [w_in_b, w_dt_b], [BF16, F32],
                         shift_idx=0, scale_idx=1, mod_index=mod_comb)
    chunks = lb // SSD_CHUNK
    ctx_chunks = lc // SSD_CHUNK
    xc = dwconv_silu(p, 4, inner, conv_w[:, :inner], conv_b[:inner], chunks_per_batch=chunks, ctx_chunks=ctx_chunks)
    bcc = dwconv_silu(p, 10, xbc_w - inner, conv_w[:, inner:], conv_b[inner:],
                      chunks_per_batch=chunks, ctx_chunks=ctx_chunks)
    scan = functools.partial(ssd_scan, xc, bcc, dtr, dtb_row, a_row, batch=bsz,
                             chunks_per_batch=chunks, ctx_chunks=ctx_chunks)
    y_f = scan(reverse=False)
    y_ssd = scan(reverse=True, z_src=p, z_col=3, y_prev=y_f, d_skip=d_skip, norm_g=ssm_norm_g[0].reshape(1, inner))
    bias = na_bias_tables(na_rpb[0], lat // GRID_W)
    vt_all = transpose_cols(p, 2, NA_HEADS * HEAD_DIM)
    o_na = na_attention(p, vt_all, bias, batch=bsz, lat_rows=lat // GRID_W, ctx_len=lc, q_col=0, k_col=1)
    ident = lambda i: i
    h1 = resid_matmul(h0, mod[0], [y_ssd, o_na], [w_out[:inner], w_out[inner:]],
                      gate_idx=2, n_blocks=bsz * nb, h_block=ident, mod_index=mod_comb)
    h2 = moe_layer(h1, norm2_g[0], mod[0], rw_t, rb, wg, wu, wd, 0, final_g, mod_index=mod_comb, final=False)

    (p1,) = norm_matmul(h2, norm1_g[1], mod[1], [w_gqa], [BF16], shift_idx=0, scale_idx=1, mod_index=mod_comb)
    cos_f, sin_s = _rope_tables(lat)
    qt, kn, vt = gqa_prep(p1, cos_f, sin_s, qg, kg, batch=bsz, blocks_per_batch=nb, ctx_blocks=cb)
    o = gqa_attention(qt, kn.reshape(bsz, lb, GQA_KV_HEADS * HEAD_DIM), vt)
    lat_block = lambda i: (i // nlb) * nb + cb + i % nlb
    h3 = resid_matmul(h2, mod[1], [o], [w_gqa_out], gate_idx=2, n_blocks=bsz * nlb, h_block=lat_block, mod_index=mod_lat)
    out = moe_layer(h3, norm2_g[1], mod[1], rw_t, rb, wg, wu, wd, 1, final_g, mod_index=mod_lat, final=True)
    return out.reshape(bsz, lat, d)
```

```python
import functools
import math

import jax
import jax.numpy as jnp
from jax import lax
from jax.experimental import pallas as pl
from jax.experimental.pallas import tpu as pltpu

F32 = jnp.float32
BF16 = jnp.bfloat16
I32 = jnp.int32
HIGHEST = lax.Precision.HIGHEST

NORM_EPS = 1e-6
GRID_W = 64
SSM_HEADS = 16
SSM_HEAD_DIM = 64
SSM_INNER = 1024
SSM_STATE = 128
SSM_GROUPS = 2
SSM_CONV_W = 5
SSD_CHUNK = 128
NA_HEADS = 16
NA_ROWS = 8
NA_COLS = 16
NA_PAIR = 2
NA_WIN_ROWS = 10
HEAD_DIM = 64
GQA_HEADS = 16
GQA_KV_HEADS = 4
ROPE_THETA = 10000.0
N_EXPERTS = 32
N_GROUPS = 8
EXPERTS_PER_GROUP = 4
D_EXPERT = 512

LANES = 128
HALO_ROWS = 16
ONES_ROWS = 16
GQA_BLOCKS_PER_TRIP = 8
TM = 256
MOE_BM = 512
DMA_ROWS = TM
NEG = -1e30
VMEM_LIMIT = 48 * 1024 * 1024


def _cparams(n_axes, vmem=None):
    return pltpu.CompilerParams(dimension_semantics=("arbitrary",) * n_axes,
                                vmem_limit_bytes=vmem)


def _silu(x):
    return x * jax.nn.sigmoid(x)


def _split3(v):
    hi = v.astype(BF16)
    r1 = v - hi.astype(F32)
    mid = r1.astype(BF16)
    lo = (r1 - mid.astype(F32)).astype(BF16)
    return hi, mid, lo


def _dot_01_right(v, m01):
    lhs = jnp.concatenate(_split3(v), axis=1)
    rhs = jnp.concatenate([m01, m01, m01], axis=0)
    return jnp.dot(lhs, rhs, preferred_element_type=F32)


def _dot_01_left(m01, v):
    lhs = jnp.concatenate([m01, m01, m01], axis=1)
    rhs = jnp.concatenate(_split3(v), axis=0)
    return jnp.dot(lhs, rhs, preferred_element_type=F32)


HI_HALF = -65536


def _pack_bf16_pairs(x):
    n = x.shape[1] // 2
    lo = lax.bitcast_convert_type(x[:, :n].astype(BF16).astype(F32), I32)
    hi = lax.bitcast_convert_type(x[:, n:].astype(BF16).astype(F32), I32)
    return lax.shift_right_logical(lo, 16) | (hi & HI_HALF)


def _unpack_bf16_pairs(u):
    lo = lax.bitcast_convert_type(lax.shift_left(u, 16), F32)
    hi = lax.bitcast_convert_type(u & HI_HALF, F32)
    return jnp.concatenate([lo, hi], axis=1)


def _norm_mod(x, g, shift, scale):
    ms = jnp.mean(x * x, axis=-1, keepdims=True)
    y = x * lax.rsqrt(ms + NORM_EPS)
    return (y * g) * (1.0 + scale) + shift


def _mod_kernel(c_ref, w_ref, b_ref, o_ref):
    a = _silu(c_ref[...]).astype(BF16)
    o_ref[0] = jnp.dot(a, w_ref[0].astype(BF16), preferred_element_type=F32) + b_ref[0]


def modulation(cc, w_mod, b_mod):
    depth, d, n = w_mod.shape
    tn = 1536
    return pl.pallas_call(
        _mod_kernel,
        out_shape=jax.ShapeDtypeStruct((depth, 8, n), F32),
        grid=(depth, n // tn),
        in_specs=[pl.BlockSpec((8, d), lambda l, j: (0, 0)),
                  pl.BlockSpec((1, d, tn), lambda l, j: (l, 0, j)),
                  pl.BlockSpec((1, 1, tn), lambda l, j: (l, 0, j))],
        out_specs=pl.BlockSpec((1, 8, tn), lambda l, j: (l, 0, j)),
        compiler_params=_cparams(2, VMEM_LIMIT),
        name="modulation",
    )(cc, w_mod, b_mod.reshape(depth, 1, n))


def _norm_matmul_kernel(h_ref, g_ref, mod_ref, *refs, shift_idx, scale_idx):
    n_w = len(refs) // 2
    xn = _norm_mod(h_ref[...], g_ref[...],
                   mod_ref[0, shift_idx:shift_idx + 1, :], mod_ref[0, scale_idx:scale_idx + 1, :]).astype(BF16)
    for w_ref, o_ref in zip(refs[:n_w], refs[n_w:]):
        o_ref[...] = jnp.dot(xn, w_ref[...], preferred_element_type=F32).astype(o_ref.dtype)


def norm_matmul(h, gain, modtab, weights, out_dtypes, *, shift_idx, scale_idx, mod_index):
    t, d = h.shape
    kern = functools.partial(_norm_matmul_kernel, shift_idx=shift_idx, scale_idx=scale_idx)
    outs = pl.pallas_call(
        kern,
        out_shape=tuple(jax.ShapeDtypeStruct((t, w.shape[1]), dt) for w, dt in zip(weights, out_dtypes)),
        grid=(t // TM,),
        in_specs=[pl.BlockSpec((TM, d), lambda i: (i, 0)),
                  pl.BlockSpec((1, d), lambda i: (0, 0)),
                  pl.BlockSpec((1, 6, d), lambda i: (mod_index(i), 0, 0))]
                 + [pl.BlockSpec(w.shape, lambda i: (0, 0)) for w in weights],
        out_specs=tuple(pl.BlockSpec((TM, w.shape[1]), lambda i: (i, 0)) for w in weights),
        compiler_params=_cparams(1, VMEM_LIMIT),
        name="norm_matmul",
    )(h, gain.reshape(1, d), modtab, *weights)
    return outs


def _resid_matmul_kernel(*refs, n_pairs, gate_idx):
    h_ref, mod_ref = refs[0], refs[1]
    a_refs = refs[2:2 + n_pairs]
    w_refs = refs[2 + n_pairs:2 + 2 * n_pairs]
    o_ref = refs[2 + 2 * n_pairs]
    acc = jnp.dot(a_refs[0][...], w_refs[0][...], preferred_element_type=F32)
    for a_ref, w_ref in zip(a_refs[1:], w_refs[1:]):
        acc = acc + jnp.dot(a_ref[...], w_ref[...], preferred_element_type=F32)
    o_ref[...] = h_ref[...] + mod_ref[0, gate_idx:gate_idx + 1, :] * acc


def resid_matmul(h, modtab, a_list, w_list, *, gate_idx, n_blocks, h_block, mod_index):
    d = h.shape[1]
    n_pairs = len(a_list)
    kern = functools.partial(_resid_matmul_kernel, n_pairs=n_pairs, gate_idx=gate_idx)
    in_specs = [pl.BlockSpec((TM, d), lambda i: (h_block(i), 0)),
                pl.BlockSpec((1, 6, d), lambda i: (mod_index(i), 0, 0))]
    in_specs += [pl.BlockSpec((TM, a.shape[1]), lambda i: (i, 0)) for a in a_list]
    in_specs += [pl.BlockSpec(w.shape, lambda i: (0, 0)) for w in w_list]
    return pl.pallas_call(
        kern,
        out_shape=jax.ShapeDtypeStruct((n_blocks * TM, d), F32),
        grid=(n_blocks,),
        in_specs=in_specs,
        out_specs=pl.BlockSpec((TM, d), lambda i: (i, 0)),
        compiler_params=_cparams(1, VMEM_LIMIT),
        name="resid_matmul",
    )(h, modtab, *a_list, *w_list)


def _dwconv_kernel(cur_ref, prev_ref, next_ref, w_ref, b_ref, o_ref, *, chunks_per_batch, ctx_chunks):
    c = pl.program_id(0) % chunks_per_batch
    is_start = jnp.logical_or(c == 0, c == ctx_chunks)
    is_end = jnp.logical_or(c == ctx_chunks - 1, c == chunks_per_batch - 1)
    half = SSM_CONV_W // 2
    hr = HALO_ROWS
    cur = cur_ref[...]
    prev = jnp.where(is_start, jnp.zeros_like(prev_ref[...]), prev_ref[...])
    nxt = jnp.where(is_end, jnp.zeros_like(next_ref[...]), next_ref[...])
    ext = jnp.concatenate([prev, cur, nxt], axis=0)
    ri = lax.broadcasted_iota(I32, (SSD_CHUNK, SSD_CHUNK + 2 * hr), 0)
    ci = lax.broadcasted_iota(I32, (SSD_CHUNK, SSD_CHUNK + 2 * hr), 1)
    acc = b_ref[...] + w_ref[half:half + 1, :] * cur.astype(F32)
    for tap in range(SSM_CONV_W):
        if tap != half:
            shift = (ci == ri + (hr - half + tap)).astype(BF16)
            acc = acc + w_ref[tap:tap + 1, :] * jnp.dot(shift, ext, preferred_element_type=F32)
    o_ref[...] = _silu(acc).astype(o_ref.dtype)


def dwconv_silu(p, col_block, width, w, b, *, chunks_per_batch, ctx_chunks):
    t = p.shape[0]
    n_chunks = t // SSD_CHUNK
    sub = SSD_CHUNK // HALO_ROWS
    kern = functools.partial(_dwconv_kernel, chunks_per_batch=chunks_per_batch, ctx_chunks=ctx_chunks)
    return pl.pallas_call(
        kern,
        out_shape=jax.ShapeDtypeStruct((t, width), BF16),
        grid=(n_chunks,),
        in_specs=[pl.BlockSpec((SSD_CHUNK, width), lambda i: (i, col_block)),
                  pl.BlockSpec((HALO_ROWS, width), lambda i: (jnp.maximum(i * sub - 1, 0), col_block)),
                  pl.BlockSpec((HALO_ROWS, width),
                               lambda i: (jnp.minimum((i + 1) * sub, t // HALO_ROWS - 1), col_block)),
                  pl.BlockSpec((SSM_CONV_W, width), lambda i: (0, 0)),
                  pl.BlockSpec((1, width), lambda i: (0, 0))],
        out_specs=pl.BlockSpec((SSD_CHUNK, width), lambda i: (i, 0)),
        compiler_params=_cparams(1),
        name="dwconv_silu",
    )(p, p, p, w, b.reshape(1, width))


def _ssd_kernel(*refs, reverse, final):
    if final:
        (x_ref, bc_ref, dt_ref, dtb_ref, a_ref, z_ref, yprev_ref, dskip_ref, ng_ref, y_ref, state_ref) = refs
    else:
        (x_ref, bc_ref, dt_ref, dtb_ref, a_ref, y_ref, state_ref) = refs
    ch = SSD_CHUNK
    n = SSM_STATE
    inner = SSM_INNER
    gw = inner // SSM_GROUPS
    col0 = SSM_HEADS if reverse else 0

    @pl.when(pl.program_id(1) == 0)
    def _():
        state_ref[...] = jnp.zeros_like(state_ref)

    x = x_ref[...].astype(F32)
    bc = bc_ref[...]
    raw = dt_ref[...] + dtb_ref[...]
    dt = jnp.maximum(raw, 0.0) + jnp.log(1.0 + jnp.exp(-jnp.abs(raw)))
    adt = dt * a_ref[...]

    ri = lax.broadcasted_iota(I32, (ch, ch), 0)
    ci = lax.broadcasted_iota(I32, (ch, ch), 1)
    if reverse:
        tri = ci >= ri
        tri_t = ri >= ci
    else:
        tri = ci <= ri
        tri_t = ri <= ci
    acum = _dot_01_left(tri.astype(BF16), adt)
    acum_t = _dot_01_right(adt.T, tri_t.astype(BF16))
    last = 0 if reverse else ch - 1
    total = acum[last:last + 1, :]

    hrow = lax.broadcasted_iota(I32, (LANES, inner), 0)
    hcol = lax.broadcasted_iota(I32, (LANES, inner), 1) // SSM_HEAD_DIM
    sel = (hrow == hcol + col0).astype(BF16)
    stack = jnp.concatenate([dt, jnp.exp(acum), jnp.exp(total - acum),
                             jnp.broadcast_to(jnp.exp(total), (8, LANES))], axis=0)
    expanded = _dot_01_right(stack, sel)
    dt_x = expanded[0:ch]
    ea_x = expanded[ch:2 * ch]
    ds_x = expanded[2 * ch:3 * ch]
    tot_x = expanded[3 * ch:3 * ch + 1]

    xd = x * dt_x
    xw = (xd * ds_x).astype(BF16)
    xd_b = xd.astype(BF16)
    state = state_ref[...]
    lane = lax.broadcasted_iota(I32, (ch, LANES), 1)
    first_half = lane < SSM_HEAD_DIM
    y_parts = []

    for g in range(SSM_GROUPS):
        b_g = bc[:, g * n:(g + 1) * n]
        c_g = bc[:, (SSM_GROUPS + g) * n:(SSM_GROUPS + g + 1) * n]
        cb = lax.dot_general(c_g, b_g, (((1,), (1,)), ((), ())), preferred_element_type=F32)
        gs = slice(g * gw, (g + 1) * gw)
        y_off = jnp.dot(c_g, state[:, gs].astype(BF16), preferred_element_type=F32) * ea_x[:, gs]
        b_t = b_g.astype(F32).T.astype(BF16)
        new_state = tot_x[:, gs] * state[:, gs] + jnp.dot(b_t, xw[:, gs], preferred_element_type=F32)
        state_ref[:, gs] = new_state
        for pair in range(gw // LANES):
            lanes = slice(g * gw + pair * LANES, g * gw + (pair + 1) * LANES)
            halves = []
            for sub in range(2):
                head = (g * gw + pair * LANES) // SSM_HEAD_DIM + sub
                cidx = col0 + head
                diff = acum[:, cidx:cidx + 1] - acum_t[cidx:cidx + 1, :]
                m_h = (cb * jnp.where(tri, jnp.exp(jnp.where(tri, diff, 0.0)), 0.0)).astype(BF16)
                halves.append(jnp.dot(m_h, xd_b[:, lanes], preferred_element_type=F32))
            y_parts.append(jnp.where(first_half, halves[0], halves[1]) + y_off[:, pair * LANES:(pair + 1) * LANES])

    y = jnp.concatenate(y_parts, axis=1)
    if final:
        y = y + yprev_ref[...] + x * dskip_ref[...]
        y = y * _silu(z_ref[...].astype(F32))
        ms = jnp.mean(y * y, axis=-1, keepdims=True)
        y = (y * lax.rsqrt(ms + NORM_EPS)) * ng_ref[...]
    y_ref[...] = y.astype(y_ref.dtype)


def ssd_scan(xc, bcc, dtr, dtb, a_row, *, batch, chunks_per_batch, ctx_chunks, reverse,
             z_src=None, z_col=None, y_prev=None, d_skip=None, norm_g=None):
    t, inner = xc.shape
    final = y_prev is not None
    npb = chunks_per_batch

    def chunk(b, s):
        if not reverse:
            c = s
        else:
            c = jnp.where(s < ctx_chunks, ctx_chunks - 1 - s, npb - 1 + ctx_chunks - s)
        return b * npb + c

    row = lambda b, s: (chunk(b, s), 0)
    const = lambda b, s: (0, 0)
    in_specs = [pl.BlockSpec((SSD_CHUNK, inner), row),
                pl.BlockSpec((SSD_CHUNK, bcc.shape[1]), row),
                pl.BlockSpec((SSD_CHUNK, LANES), row),
                pl.BlockSpec((1, LANES), const),
                pl.BlockSpec((1, LANES), const)]
    args = [xc, bcc, dtr, dtb, a_row]
    if final:
        in_specs += [pl.BlockSpec((SSD_CHUNK, inner), lambda b, s: (chunk(b, s), z_col)),
                     pl.BlockSpec((SSD_CHUNK, inner), row),
                     pl.BlockSpec((1, inner), const),
                     pl.BlockSpec((1, inner), const)]
        args += [z_src, y_prev, d_skip, norm_g]
    kern = functools.partial(_ssd_kernel, reverse=reverse, final=final)
    return pl.pallas_call(
        kern,
        out_shape=jax.ShapeDtypeStruct((t, inner), BF16 if final else F32),
        grid=(batch, npb),
        in_specs=in_specs,
        out_specs=pl.BlockSpec((SSD_CHUNK, inner), row),
        scratch_shapes=[pltpu.VMEM((SSM_STATE, inner), F32)],
        compiler_params=_cparams(2, VMEM_LIMIT),
        name="ssd_bwd" if reverse else "ssd_fwd",
    )(*args)


def _na_window_base(t, lat_rows):
    return jnp.minimum(jnp.clip(2 * t - NA_ROWS // 2, 0, lat_rows - NA_ROWS), lat_rows - NA_WIN_ROWS)


def _na_pair_config(t, lat_rows):
    clip = lambda v, lo, hi: max(lo, min(v, hi))
    starts = [clip(r - NA_ROWS // 2, 0, lat_rows - NA_ROWS) for r in (2 * t, 2 * t + 1)]
    base = min(starts[0], lat_rows - NA_WIN_ROWS)
    return (starts[0] - base, starts[1] - base, 2 * t - starts[0], 2 * t + 1 - starts[1])


NA_CONFIGS = ((0, 1, 4, 4), (0, 0, 0, 1), (0, 0, 2, 3), (2, 2, 4, 5), (2, 2, 6, 7), (99, 99, 0, 0))


def _na_config_index(t, n_pairs):
    return jnp.where(t < 0, 5, jnp.where(t == 0, 1, jnp.where(t == 1, 2, jnp.where(
        t == n_pairs - 2, 3, jnp.where(t == n_pairs - 1, 4, 0)))))


def _na_bias_kernel(rpb_ref, o_ref, toe_ref):
    head = pl.program_id(0)
    w = GRID_W
    ki = lax.broadcasted_iota(I32, (w, w), 0)
    qi = lax.broadcasted_iota(I32, (w, w), 1)
    cstart = jnp.clip(qi - NA_COLS // 2, 0, w - NA_COLS)
    col_ok = jnp.logical_and(ki >= cstart, ki < cstart + NA_COLS)
    col_idx = jnp.clip(ki - qi + NA_COLS - 1, 0, 2 * NA_COLS - 2)
    for rho in range(2 * NA_ROWS - 1):
        tile = jnp.zeros((w, w), F32)
        for tcol in range(2 * NA_COLS - 1):
            tile = jnp.where(col_idx == tcol, rpb_ref[head, rho, tcol], tile)
        toe_ref[rho] = jnp.where(col_ok, tile, NEG)
    masked = jnp.full((w, w), NEG, F32)
    for cfg, offsets in enumerate(NA_CONFIGS):
        for j in range(NA_WIN_ROWS):
            for qr in range(NA_PAIR):
                rel = j - offsets[qr]
                inside = 0 <= rel < NA_ROWS
                tile = toe_ref[rel - offsets[NA_PAIR + qr] + NA_ROWS - 1] if inside else masked
                o_ref[cfg, 0, j * w:(j + 1) * w, qr * w:(qr + 1) * w] = tile


def na_bias_tables(rpb, lat_rows):
    heads = rpb.shape[0]
    n_pairs = lat_rows // NA_PAIR
    assert lat_rows % NA_PAIR == 0 and n_pairs >= 5
    for t in range(n_pairs):
        idx = 1 if t == 0 else 2 if t == 1 else 3 if t == n_pairs - 2 else 4 if t == n_pairs - 1 else 0
        assert _na_pair_config(t, lat_rows) == NA_CONFIGS[idx]
    n_cfg = len(NA_CONFIGS)
    return pl.pallas_call(
        _na_bias_kernel,
        out_shape=jax.ShapeDtypeStruct((n_cfg, heads, NA_WIN_ROWS * GRID_W, NA_PAIR * GRID_W), F32),
        grid=(heads,),
        in_specs=[pl.BlockSpec(memory_space=pltpu.SMEM)],
        out_specs=pl.BlockSpec((n_cfg, 1, NA_WIN_ROWS * GRID_W, NA_PAIR * GRID_W), lambda h: (0, h, 0, 0)),
        scratch_shapes=[pltpu.VMEM((2 * NA_ROWS - 1, GRID_W, GRID_W), F32)],
        compiler_params=_cparams(1),
        name="na_bias_tables",
    )(rpb)


def _transpose_kernel(x_ref, o_ref):
    for tile in range(x_ref.shape[1] // LANES):
        lanes = slice(tile * LANES, (tile + 1) * LANES)
        o_ref[lanes, :] = x_ref[:, lanes].astype(F32).T.astype(o_ref.dtype)


def transpose_cols(p, col_block, width):
    t = p.shape[0]
    return pl.pallas_call(
        _transpose_kernel,
        out_shape=jax.ShapeDtypeStruct((width, t), p.dtype),
        grid=(t // TM,),
        in_specs=[pl.BlockSpec((TM, width), lambda i: (i, col_block))],
        out_specs=pl.BlockSpec((width, TM), lambda i: (0, i)),
        compiler_params=_cparams(1),
        name="transpose_cols",
    )(p)


def _na_kernel(*refs):
    nb = NA_WIN_ROWS * GRID_W // LANES
    q_ref = refs[0]
    k_refs = refs[1:1 + nb]
    kc_ref = refs[1 + nb]
    vt_refs = refs[2 + nb:2 + 2 * nb]
    vct_ref, bias_ref, o_ref, s_ref, p_ref = refs[2 + 2 * nb:]
    nq = q_ref.shape[0]
    n_lat = NA_WIN_ROWS * GRID_W
    n_keys = n_lat + kc_ref.shape[0]
    scale = HEAD_DIM ** -0.5
    top = lax.broadcasted_iota(I32, (LANES, nq), 0) < HEAD_DIM
    for pair in range(NA_HEADS // 2):
        lanes = slice(pair * LANES, (pair + 1) * LANES)
        qt = (q_ref[:, lanes] * scale).astype(F32).T
        kk = jnp.concatenate([r[:, lanes] for r in k_refs] + [kc_ref[:, lanes]], axis=0)
        for sub in range(2):
            keep = top if sub == 0 else jnp.logical_not(top)
            qz = jnp.where(keep, qt, 0.0).astype(BF16)
            s_ref[2 * pair + sub] = jnp.dot(kk, qz, preferred_element_type=F32)
    for head in range(NA_HEADS):
        s_lat = s_ref[head, 0:n_lat, :] + bias_ref[0, head]
        s_ctx = s_ref[head, n_lat:, :]
        m = jnp.maximum(jnp.max(s_lat, axis=0, keepdims=True), jnp.max(s_ctx, axis=0, keepdims=True))
        p_ref[head, 0:n_lat, :] = jnp.exp(s_lat - m).astype(BF16)
        p_ref[head, n_lat:, :] = jnp.exp(s_ctx - m).astype(BF16)
    ones = jnp.ones((ONES_ROWS, n_keys), BF16)
    for pair in range(NA_HEADS // 2):
        halves = []
        for sub in range(2):
            head = 2 * pair + sub
            rows = slice(head * HEAD_DIM, (head + 1) * HEAD_DIM)
            vt = jnp.concatenate([r[rows, :] for r in vt_refs] + [vct_ref[rows, :]], axis=1)
            pv = jnp.dot(jnp.concatenate([vt, ones], axis=0), p_ref[head], preferred_element_type=F32)
            halves.append(pv[0:HEAD_DIM] * (1.0 / pv[HEAD_DIM:HEAD_DIM + 1]))
        o_t = jnp.concatenate(halves, axis=0)
        o_ref[:, pair * LANES:(pair + 1) * LANES] = o_t.T.astype(o_ref.dtype)


def na_attention(p, vt_all, bias, *, batch, lat_rows, ctx_len, q_col, k_col):
    t = p.shape[0]
    width = NA_HEADS * HEAD_DIM
    nq = NA_PAIR * GRID_W
    assert nq == LANES and ctx_len % nq == 0
    nb = NA_WIN_ROWS * GRID_W // LANES
    ctx_steps = ctx_len // nq
    n_pairs = lat_rows // NA_PAIR
    steps = ctx_steps + n_pairs
    blocks_ctx = steps * nq // ctx_len

    def key_block(b, st, j):
        base = _na_window_base(st - ctx_steps, lat_rows)
        return b * steps + ctx_steps + base // NA_PAIR + j

    in_specs = [pl.BlockSpec((nq, width), lambda b, st: (b * steps + st, q_col))]
    in_specs += [pl.BlockSpec((LANES, width), functools.partial(lambda b, st, j: (key_block(b, st, j), k_col), j=j))
                 for j in range(nb)]
    in_specs += [pl.BlockSpec((ctx_len, width), lambda b, st: (b * blocks_ctx, k_col))]
    in_specs += [pl.BlockSpec((width, LANES), functools.partial(lambda b, st, j: (0, key_block(b, st, j)), j=j))
                 for j in range(nb)]
    in_specs += [pl.BlockSpec((width, ctx_len), lambda b, st: (0, b * blocks_ctx)),
                 pl.BlockSpec((1, NA_HEADS, NA_WIN_ROWS * GRID_W, nq),
                              lambda b, st: (_na_config_index(st - ctx_steps, n_pairs), 0, 0, 0))]
    args = [p] * (2 + nb) + [vt_all] * (1 + nb) + [bias]
    n_keys = NA_WIN_ROWS * GRID_W + ctx_len
    return pl.pallas_call(
        _na_kernel,
        out_shape=jax.ShapeDtypeStruct((t, width), BF16),
        grid=(batch, steps),
        in_specs=in_specs,
        out_specs=pl.BlockSpec((nq, width), lambda b, st: (b * steps + st, 0)),
        scratch_shapes=[pltpu.VMEM((NA_HEADS, n_keys, nq), F32),
                        pltpu.VMEM((NA_HEADS, n_keys, nq), BF16)],
        compiler_params=_cparams(2, VMEM_LIMIT),
        name="na_attention",
    )(*args)


def _group_mean_sq(x):
    gi = lax.broadcasted_iota(I32, (LANES, LANES), 0) // HEAD_DIM
    gj = lax.broadcasted_iota(I32, (LANES, LANES), 1) // HEAD_DIM
    ones = (gi == gj).astype(BF16)
    return _dot_01_right(x * x, ones) * (1.0 / HEAD_DIM)


def _rope(x, cos_f, sin_s):
    lane = lax.broadcasted_iota(I32, x.shape, 1)
    low = (lane % HEAD_DIM) < HEAD_DIM // 2
    partner = jnp.where(low, pltpu.roll(x, LANES - HEAD_DIM // 2, 1), pltpu.roll(x, HEAD_DIM // 2, 1))
    return x * cos_f + partner * sin_s


def _gqa_prep_kernel(p_ref, cos_ref, sin_ref, qg_ref, kg_ref, qt_ref, k_ref, vt_ref, *, ctx_blocks):
    j = pl.program_id(1)
    is_lat = j >= ctx_blocks
    q_w = GQA_HEADS * HEAD_DIM
    kv_w = GQA_KV_HEADS * HEAD_DIM
    cos_f = jnp.where(is_lat, cos_ref[...], 1.0)
    sin_s = jnp.where(is_lat, sin_ref[...], 0.0)

    def normed(cols, gain):
        x = p_ref[:, cols].astype(F32)
        y = x * lax.rsqrt(_group_mean_sq(x) + NORM_EPS)
        return _rope(y * gain, cos_f, sin_s)

    for tile in range(kv_w // LANES):
        cols = slice(q_w + tile * LANES, q_w + (tile + 1) * LANES)
        k_ref[:, tile * LANES:(tile + 1) * LANES] = normed(cols, kg_ref[...]).astype(k_ref.dtype)
    v = p_ref[:, q_w + kv_w:q_w + 2 * kv_w].astype(F32)
    vt_ref[0] = v.T.astype(vt_ref.dtype)

    @pl.when(is_lat)
    def _():
        scale = HEAD_DIM ** -0.5 * math.log2(math.e)
        for tile in range(q_w // LANES):
            cols = slice(tile * LANES, (tile + 1) * LANES)
            q = normed(cols, qg_ref[...]) * scale
            qt_ref[0, cols, :] = q.T.astype(qt_ref.dtype)


def gqa_prep(p1, cos_f, sin_s, qg, kg, *, batch, blocks_per_batch, ctx_blocks):
    t, n = p1.shape
    q_w = GQA_HEADS * HEAD_DIM
    kv_w = GQA_KV_HEADS * HEAD_DIM
    lb = blocks_per_batch * TM
    lat = (blocks_per_batch - ctx_blocks) * TM
    kern = functools.partial(_gqa_prep_kernel, ctx_blocks=ctx_blocks)
    lat_blk = lambda b, j: jnp.maximum(j - ctx_blocks, 0)
    return pl.pallas_call(
        kern,
        out_shape=(jax.ShapeDtypeStruct((batch, q_w, lat), BF16),
                   jax.ShapeDtypeStruct((t, kv_w), BF16),
                   jax.ShapeDtypeStruct((batch, kv_w, lb), BF16)),
        grid=(batch, blocks_per_batch),
        in_specs=[pl.BlockSpec((TM, n), lambda b, j: (b * blocks_per_batch + j, 0)),
                  pl.BlockSpec((TM, LANES), lambda b, j: (lat_blk(b, j), 0)),
                  pl.BlockSpec((TM, LANES), lambda b, j: (lat_blk(b, j), 0)),
                  pl.BlockSpec((1, LANES), lambda b, j: (0, 0)),
                  pl.BlockSpec((1, LANES), lambda b, j: (0, 0))],
        out_specs=(pl.BlockSpec((1, q_w, TM), lambda b, j: (b, 0, lat_blk(b, j))),
                   pl.BlockSpec((TM, kv_w), lambda b, j: (b * blocks_per_batch + j, 0)),
                   pl.BlockSpec((1, kv_w, TM), lambda b, j: (b, 0, j))),
        compiler_params=_cparams(2, VMEM_LIMIT),
        name="gqa_prep",
    )(p1, cos_f, sin_s, qg, kg)


def _gqa_attn_kernel(qt_ref, k_ref, vt_ref, o_ref, qz_ref, s_ref, bmax_ref, m_ref, l_ref, acc_ref, *, tk, n_kblocks):
    kv = pl.program_id(1)
    tq = qt_ref.shape[2]
    group = GQA_HEADS // GQA_KV_HEADS
    second = (kv % 2) == 1
    for hq in range(group):
        q_h = qt_ref[0, hq * HEAD_DIM:(hq + 1) * HEAD_DIM, :]
        zero = jnp.zeros_like(q_h)
        qz_ref[0:HEAD_DIM, hq * tq:(hq + 1) * tq] = jnp.where(second, zero, q_h)
        qz_ref[HEAD_DIM:2 * HEAD_DIM, hq * tq:(hq + 1) * tq] = jnp.where(second, q_h, zero)
    m_ref[...] = jnp.full_like(m_ref, NEG)
    l_ref[...] = jnp.zeros_like(l_ref)
    acc_ref[...] = jnp.zeros_like(acc_ref)

    def scores(kb, slot):
        off = pl.multiple_of(kb * tk, tk)
        s = jnp.dot(k_ref[0, pl.ds(off, tk), :], qz_ref[...], preferred_element_type=F32)
        s_ref[slot] = s
        bmax_ref[slot] = jnp.max(s, axis=0, keepdims=True)

    def update(kb, slot):
        off = pl.multiple_of(kb * tk, tk)
        s = s_ref[slot]
        m_old = m_ref[...]
        m_new = jnp.maximum(m_old, bmax_ref[slot])
        alpha = jnp.exp2(m_old - m_new)
        p = jnp.exp2(s - m_new).astype(BF16)
        v_aug = jnp.concatenate([vt_ref[0, :, pl.ds(off, tk)], jnp.ones((ONES_ROWS, tk), BF16)], axis=0)
        pv = jnp.dot(v_aug, p, preferred_element_type=F32)
        acc_ref[...] = alpha * acc_ref[...] + pv[0:HEAD_DIM]
        l_ref[...] = alpha * l_ref[...] + pv[HEAD_DIM:HEAD_DIM + 1]
        m_ref[...] = m_new

    per_trip = GQA_BLOCKS_PER_TRIP
    scores(0, 0)
    trips = (n_kblocks - 1) // per_trip

    def one_trip(i, carry):
        kb = per_trip * i
        for j in range(per_trip):
            scores(kb + j + 1, (j + 1) % 2)
            update(kb + j, j % 2)
        return carry

    lax.fori_loop(0, trips, one_trip, 0)
    base = trips * per_trip
    for j in range(n_kblocks - base):
        if base + j + 1 < n_kblocks:
            scores(base + j + 1, (j + 1) % 2)
        update(base + j, j % 2)

    inv_l = 1.0 / l_ref[...]
    for pair in range(group // 2):
        lo, hi = 2 * pair * tq, (2 * pair + 1) * tq
        o_t = jnp.concatenate([acc_ref[:, lo:lo + tq] * inv_l[:, lo:lo + tq],
                               acc_ref[:, hi:hi + tq] * inv_l[:, hi:hi + tq]], axis=0)
        o_ref[:, pair * LANES:(pair + 1) * LANES] = o_t.T.astype(o_ref.dtype)


def gqa_attention(qt, k, vt, *, tq=512, tk=256):
    batch, q_w, lat = qt.shape
    lb = k.shape[1]
    group = GQA_HEADS // GQA_KV_HEADS
    gw = group * HEAD_DIM
    tq = min(tq, lat)
    kern = functools.partial(_gqa_attn_kernel, tk=tk, n_kblocks=lb // tk)
    return pl.pallas_call(
        kern,
        out_shape=jax.ShapeDtypeStruct((batch * lat, q_w), BF16),
        grid=(batch, GQA_KV_HEADS, lat // tq),
        in_specs=[pl.BlockSpec((1, gw, tq), lambda b, h, i: (b, h, i)),
                  pl.BlockSpec((1, lb, LANES), lambda b, h, i: (b, 0, h // 2)),
                  pl.BlockSpec((1, HEAD_DIM, lb), lambda b, h, i: (b, h, 0))],
        out_specs=pl.BlockSpec((tq, gw), lambda b, h, i: (b * (lat // tq) + i, h)),
        scratch_shapes=[pltpu.VMEM((2 * HEAD_DIM, group * tq), BF16),
                        pltpu.VMEM((2, tk, group * tq), F32),
                        pltpu.VMEM((2, 1, group * tq), F32),
                        pltpu.VMEM((1, group * tq), F32),
                        pltpu.VMEM((1, group * tq), F32),
                        pltpu.VMEM((HEAD_DIM, group * tq), F32)],
        compiler_params=_cparams(3, VMEM_LIMIT),
        name="gqa_attention",
    )(qt, k, vt)


def _router_kernel(h_ref, g_ref, mod_ref, rw_ref, rb_ref, xn_ref, eidx_ref, wts_ref, rank_ref, cnt_ref, run_ref):
    @pl.when(pl.program_id(0) == 0)
    def _():
        run_ref[...] = jnp.zeros_like(run_ref)

    xn = _norm_mod(h_ref[...], g_ref[...], mod_ref[0, 3:4, :], mod_ref[0, 4:5, :])
    xn_ref[...] = _pack_bf16_pairs(xn)
    tm = xn.shape[0]
    logits = lax.dot_general(rw_ref[...], xn, (((1,), (1,)), ((), ())),
                             precision=HIGHEST, preferred_element_type=F32)
    s = jax.nn.sigmoid(logits)
    sb = s + rb_ref[...]
    row = lambda a, e: a[e:e + 1, :]

    best = None
    for g in range(N_GROUPS):
        a = [row(sb, g * EXPERTS_PER_GROUP + j) for j in range(EXPERTS_PER_GROUP)]
        score = a[0] + a[1]
        for i in range(EXPERTS_PER_GROUP):
            for j in range(i + 1, EXPERTS_PER_GROUP):
                if (i, j) != (0, 1):
                    score = jnp.maximum(score, a[i] + a[j])
        if best is None:
            best, grp = score, jnp.zeros((1, tm), I32)
        else:
            better = score > best
            best = jnp.where(better, score, best)
            grp = jnp.where(better, g, grp)

    biased, plain = [], []
    for j in range(EXPERTS_PER_GROUP):
        vb = row(sb, j)
        vp = row(s, j)
        for g in range(1, N_GROUPS):
            pick = grp == g
            vb = jnp.where(pick, row(sb, g * EXPERTS_PER_GROUP + j), vb)
            vp = jnp.where(pick, row(s, g * EXPERTS_PER_GROUP + j), vp)
        biased.append(vb)
        plain.append(vp)

    def argmax_first(vals):
        top, idx = vals[0], jnp.zeros((1, tm), I32)
        for j in range(1, len(vals)):
            better = vals[j] > top
            top = jnp.where(better, vals[j], top)
            idx = jnp.where(better, j, idx)
        return idx

    i1 = argmax_first(biased)
    i2 = argmax_first([jnp.where(i1 == j, -jnp.inf, biased[j]) for j in range(EXPERTS_PER_GROUP)])

    def pick_plain(idx):
        out = plain[0]
        for j in range(1, EXPERTS_PER_GROUP):
            out = jnp.where(idx == j, plain[j], out)
        return out

    w1, w2 = pick_plain(i1), pick_plain(i2)
    wsum = w1 + w2
    e1 = grp * EXPERTS_PER_GROUP + i1
    e2 = grp * EXPERTS_PER_GROUP + i2
    eidx_ref[0] = jnp.concatenate([e1, e2], axis=0)
    wts_ref[...] = jnp.concatenate([w1 / wsum, w2 / wsum], axis=0)

    erow = lax.broadcasted_iota(I32, (N_EXPERTS, tm), 0)
    hit1 = erow == e1
    hit2 = erow == e2
    onehot = jnp.logical_or(hit1, hit2)
    si = lax.broadcasted_iota(I32, (tm, tm), 0)
    ti = lax.broadcasted_iota(I32, (tm, tm), 1)
    before = jnp.dot(onehot.astype(BF16), (si < ti).astype(BF16), preferred_element_type=F32)
    before = before + run_ref[...][:, 0:1]
    r1 = jnp.sum(jnp.where(hit1, before, 0.0), axis=0, keepdims=True)
    r2 = jnp.sum(jnp.where(hit2, before, 0.0), axis=0, keepdims=True)
    rank_ref[0] = jnp.concatenate([r1, r2], axis=0).astype(I32)
    totals = run_ref[...] + jnp.sum(onehot.astype(F32), axis=1, keepdims=True)
    run_ref[...] = totals
    cnt_ref[...] = totals


def moe_router(h, gain, modtab, rw_t, rb, *, mod_index):
    t, d = h.shape
    return pl.pallas_call(
        _router_kernel,
        out_shape=(jax.ShapeDtypeStruct((t, d // 2), I32),
                   jax.ShapeDtypeStruct((t // TM, 2, TM), I32),
                   jax.ShapeDtypeStruct((2, t), F32),
                   jax.ShapeDtypeStruct((t // TM, 2, TM), I32),
                   jax.ShapeDtypeStruct((N_EXPERTS, LANES), F32)),
        grid=(t // TM,),
        in_specs=[pl.BlockSpec((TM, d), lambda i: (i, 0)),
                  pl.BlockSpec((1, d), lambda i: (0, 0)),
                  pl.BlockSpec((1, 6, d), lambda i: (mod_index(i), 0, 0)),
                  pl.BlockSpec((N_EXPERTS, d), lambda i: (0, 0)),
                  pl.BlockSpec((N_EXPERTS, 1), lambda i: (0, 0))],
        out_specs=(pl.BlockSpec((TM, d // 2), lambda i: (i, 0)),
                   pl.BlockSpec((1, 2, TM), lambda i: (i, 0, 0)),
                   pl.BlockSpec((2, TM), lambda i: (0, i)),
                   pl.BlockSpec((1, 2, TM), lambda i: (i, 0, 0)),
                   pl.BlockSpec((N_EXPERTS, LANES), lambda i: (0, 0))),
        scratch_shapes=[pltpu.VMEM((N_EXPERTS, LANES), F32)],
        compiler_params=_cparams(1, VMEM_LIMIT),
        name="moe_router",
    )(h, gain.reshape(1, d), modtab, rw_t, rb)


def _row_copy(src_ref, src_row, dst_ref, dst_row, sem):
    return pltpu.make_async_copy(src_ref.at[pl.ds(src_row, 1)], dst_ref.at[pl.ds(dst_row, 1)], sem)


def _slots_kernel(start_ref, eidx_ref, rank_ref, slot_ref):
    e = eidx_ref[...]
    slot = rank_ref[...]
    for ex in range(N_EXPERTS):
        slot = slot + jnp.where(e == ex, start_ref[ex], 0)
    slot_ref[...] = slot


def moe_slots(pad_start, eidx3, rank3):
    return pl.pallas_call(
        _slots_kernel,
        out_shape=jax.ShapeDtypeStruct(eidx3.shape, I32),
        in_specs=[pl.BlockSpec(memory_space=pltpu.SMEM),
                  pl.BlockSpec(memory_space=pltpu.VMEM), pl.BlockSpec(memory_space=pltpu.VMEM)],
        out_specs=pl.BlockSpec(memory_space=pltpu.VMEM),
        name="moe_slots",
    )(pad_start, eidx3, rank3)


def _dispatch_kernel(slot_ref, x_ref, xs_in_ref, xs_ref, sem):
    del xs_in_ref
    rows = x_ref.shape[0]
    for r in range(rows):
        for k in range(2):
            _row_copy(x_ref, r, xs_ref, slot_ref[0, k, r], sem).start(priority=k)
    for _ in range(2 * rows):
        _row_copy(x_ref, 0, xs_ref, 0, sem).wait()


def moe_dispatch(xn, slot3, n_slots):
    t, d = xn.shape
    xs0 = jnp.zeros((n_slots, d), xn.dtype)
    return pl.pallas_call(
        _dispatch_kernel,
        out_shape=jax.ShapeDtypeStruct((n_slots, d), xn.dtype),
        grid=(t // DMA_ROWS,),
        in_specs=[pl.BlockSpec((1, 2, DMA_ROWS), lambda i: (i, 0, 0), memory_space=pltpu.SMEM),
                  pl.BlockSpec((DMA_ROWS, d), lambda i: (i, 0)),
                  pl.BlockSpec(memory_space=pl.ANY)],
        out_specs=pl.BlockSpec(memory_space=pl.ANY),
        scratch_shapes=[pltpu.SemaphoreType.DMA(())],
        input_output_aliases={2: 0},
        compiler_params=_cparams(1),
        name="moe_dispatch",
    )(slot3, xn, xs0)


def _expert_kernel(blk_e_ref, nused_ref, x_ref, wg_ref, wu_ref, wd_ref, y_ref, wg_b, wu_b, wd_b):
    i = pl.program_id(0)

    @pl.when(i < nused_ref[0])
    def _():
        @pl.when(jnp.logical_or(i == 0, blk_e_ref[i] != blk_e_ref[jnp.maximum(i - 1, 0)]))
        def _():
            wg_b[...] = wg_ref[0, 0].astype(BF16)
            wu_b[...] = wu_ref[0, 0].astype(BF16)
            wd_b[...] = wd_ref[0, 0].astype(BF16)

        x = _unpack_bf16_pairs(x_ref[...]).astype(BF16)
        gate = jnp.dot(x, wg_b[...], preferred_element_type=F32)
        up = jnp.dot(x, wu_b[...], preferred_element_type=F32)
        hid = (_silu(gate) * up).astype(BF16)
        y_ref[...] = _pack_bf16_pairs(jnp.dot(hid, wd_b[...], preferred_element_type=F32))

    @pl.when(pl.program_id(0) >= nused_ref[0])
    def _():
        y_ref[...] = jnp.zeros_like(y_ref)


def moe_experts(xs, blk_e, nused, wg, wu, wd, layer):
    n_slots, dw = xs.shape
    d, de = wg.shape[2], wg.shape[3]
    grid_spec = pltpu.PrefetchScalarGridSpec(
        num_scalar_prefetch=2,
        grid=(n_slots // MOE_BM,),
        in_specs=[pl.BlockSpec((MOE_BM, dw), lambda i, be, nu: (i, 0)),
                  pl.BlockSpec((1, 1, d, de), lambda i, be, nu: (layer, be[i], 0, 0)),
                  pl.BlockSpec((1, 1, d, de), lambda i, be, nu: (layer, be[i], 0, 0)),
                  pl.BlockSpec((1, 1, de, d), lambda i, be, nu: (layer, be[i], 0, 0))],
        out_specs=pl.BlockSpec((MOE_BM, dw), lambda i, be, nu: (i, 0)),
        scratch_shapes=[pltpu.VMEM((d, de), BF16), pltpu.VMEM((d, de), BF16), pltpu.VMEM((de, d), BF16)],
    )
    return pl.pallas_call(
        _expert_kernel,
        out_shape=jax.ShapeDtypeStruct((n_slots, dw), I32),
        grid_spec=grid_spec,
        compiler_params=_cparams(1, VMEM_LIMIT),
        name="moe_experts",
    )(blk_e, nused, xs, wg, wu, wd)


def _combine_kernel(slot_ref, slot_nxt_ref, h_ref, mod_ref, w_ref, fg_ref, yb_ref, o_ref, buf_ref, sems, *, final):
    step = pl.program_id(0)
    rows = h_ref.shape[0]

    def start_gathers(s_ref, half):
        for r in range(rows):
            for k in range(2):
                _row_copy(yb_ref, s_ref[0, k, r], buf_ref.at[half, k], r, sems.at[half]).start(priority=k)

    cur = step % 2

    @pl.when(step == 0)
    def _():
        start_gathers(slot_ref, 0)

    @pl.when(step + 1 < pl.num_programs(0))
    def _():
        start_gathers(slot_nxt_ref, 1 - cur)

    for _ in range(rows):
        for k in range(2):
            _row_copy(yb_ref, 0, buf_ref.at[cur, k], 0, sems.at[cur]).wait()
    f = (w_ref[:, 0:1] * _unpack_bf16_pairs(buf_ref[cur, 0])
         + w_ref[:, 1:2] * _unpack_bf16_pairs(buf_ref[cur, 1]))
    h = h_ref[...] + mod_ref[0, 5:6, :] * f
    if final:
        ms = jnp.mean(h * h, axis=-1, keepdims=True)
        h = (h * lax.rsqrt(ms + NORM_EPS)) * fg_ref[...]
    o_ref[...] = h


def moe_combine(h, modtab, wts_t, final_g, yb, slot3, *, mod_index, final):
    t, d = h.shape
    kern = functools.partial(_combine_kernel, final=final)
    n_steps = t // DMA_ROWS
    idx_spec = pl.BlockSpec((1, 2, DMA_ROWS), lambda i: (i, 0, 0), memory_space=pltpu.SMEM)
    nxt_spec = pl.BlockSpec((1, 2, DMA_ROWS), lambda i: (jnp.minimum(i + 1, n_steps - 1), 0, 0),
                            memory_space=pltpu.SMEM)
    return pl.pallas_call(
        kern,
        out_shape=jax.ShapeDtypeStruct((t, d), F32),
        grid=(n_steps,),
        in_specs=[idx_spec, nxt_spec,
                  pl.BlockSpec((DMA_ROWS, d), lambda i: (i, 0)),
                  pl.BlockSpec((1, 6, d), lambda i: (mod_index(i), 0, 0)),
                  pl.BlockSpec((DMA_ROWS, 2), lambda i: (i, 0)),
                  pl.BlockSpec((1, d), lambda i: (0, 0)),
                  pl.BlockSpec(memory_space=pl.ANY)],
        out_specs=pl.BlockSpec((DMA_ROWS, d), lambda i: (i, 0)),
        scratch_shapes=[pltpu.VMEM((2, 2, DMA_ROWS, yb.shape[1]), yb.dtype), pltpu.SemaphoreType.DMA((2,))],
        compiler_params=_cparams(1, VMEM_LIMIT),
        name="moe_combine",
    )(slot3, slot3, h, modtab, wts_t, final_g.reshape(1, d), yb)


def moe_layer(h, norm_g, modtab, rw_t, rb, wg, wu, wd, layer, final_g, *, mod_index, final):
    t, d = h.shape
    xn, eidx3, wts, rank3, counts = moe_router(h, norm_g, modtab, rw_t, rb, mod_index=mod_index)
    n_blocks = (2 * t) // MOE_BM + N_EXPERTS
    n_slots = n_blocks * MOE_BM
    cnt = counts[:, 0].astype(I32)
    padded = (cnt + MOE_BM - 1) // MOE_BM * MOE_BM
    pad_end = jnp.cumsum(padded)
    pad_start = (pad_end - padded).astype(I32)
    blk_start = jnp.arange(n_blocks, dtype=I32) * MOE_BM
    blk_e = jnp.minimum(jnp.sum((pad_end[None, :] <= blk_start[:, None]).astype(I32), axis=1), N_EXPERTS - 1)
    nused = (pad_end[-1] // MOE_BM).reshape(1).astype(I32)
    slot3 = moe_slots(pad_start, eidx3, rank3)
    xs = moe_dispatch(xn, slot3, n_slots)
    yb = moe_experts(xs, blk_e.astype(I32), nused, wg, wu, wd, layer)
    return moe_combine(h, modtab, wts.T, final_g, yb, slot3, mod_index=mod_index, final=final)


def _rope_tables(lat):
    t = jnp.arange(lat)
    row = (t // GRID_W).astype(F32)
    col = (t % GRID_W).astype(F32)
    n_freq = HEAD_DIM // 4
    inv = ROPE_THETA ** (-jnp.arange(n_freq, dtype=F32) / n_freq)
    ang = jnp.concatenate([row[:, None] * inv, col[:, None] * inv], axis=-1)
    cos, sin = jnp.cos(ang), jnp.sin(ang)
    cos_f = jnp.tile(cos, (1, LANES // (HEAD_DIM // 2)))
    sin_s = jnp.tile(jnp.concatenate([-sin, sin], axis=-1), (1, LANES // HEAD_DIM))
    return cos_f, sin_s


def kernel(x, c, ctx, c_ctx, w_mod, b_mod, norm1_g, norm2_g, final_g, ab_w_in, ab_w_out, ssm_conv_w, ssm_conv_b, ssm_A_log, ssm_dt_bias, ssm_D, ssm_norm_g, na_rpb, gqa_w_in, gqa_w_out, gqa_q_norm, gqa_k_norm, router_w, router_bias, moe_w_gate, moe_w_up, moe_w_down):
    bsz, lat, d = x.shape
    lc = ctx.shape[1]
    lb = lc + lat
    assert lat % TM == 0 and lc % TM == 0 and lat % GRID_W == 0 and lat // GRID_W >= NA_ROWS
    assert bsz + 1 <= 8 and w_mod.shape[0] == 2
    nb = lb // TM
    cb = lc // TM
    nlb = lat // TM

    def mod_comb(i):
        return jnp.where(i % nb < cb, bsz, i // nb)

    def mod_lat(i):
        return i // nlb

    inner, xbc_w = SSM_INNER, SSM_INNER + 2 * SSM_GROUPS * SSM_STATE
    w_in = ab_w_in[0]
    o_dt = inner + xbc_w
    o_q = o_dt + 2 * SSM_HEADS
    w_z, w_x, w_bc = w_in[:, :inner], w_in[:, inner:2 * inner], w_in[:, 2 * inner:o_dt]
    w_dt = w_in[:, o_dt:o_q]
    w_qkv = w_in[:, o_q:]
    w_in_b = jnp.concatenate([w_qkv, w_z, w_x, w_bc], axis=1).astype(BF16)
    w_dt_b = jnp.pad(w_dt, ((0, 0), (0, LANES - 2 * SSM_HEADS))).astype(BF16)
    conv_w, conv_b = ssm_conv_w[0], ssm_conv_b[0]
    pad32 = lambda v: jnp.pad(v.reshape(1, -1), ((0, 0), (0, LANES - 2 * SSM_HEADS)))
    dtb_row = pad32(ssm_dt_bias[0])
    a_row = pad32(-jnp.exp(ssm_A_log[0].astype(F32)))
    d_skip = jnp.repeat(ssm_D[0], SSM_HEAD_DIM).reshape(1, inner)
    w_out = ab_w_out[0].astype(BF16)
    w_gqa = gqa_w_in[0].astype(BF16)
    w_gqa_out = gqa_w_out[0].astype(BF16)
    qg = jnp.tile(gqa_q_norm[0], LANES // HEAD_DIM).reshape(1, LANES)
    kg = jnp.tile(gqa_k_norm[0], LANES // HEAD_DIM).reshape(1, LANES)
    rw_t = router_w.T
    rb = router_bias.reshape(N_EXPERTS, 1)
    wg, wu, wd = moe_w_gate, moe_w_up, moe_w_down

    cc = jnp.concatenate([c, c_ctx[None, :], jnp.zeros((8 - bsz - 1, d), F32)], axis=0)
    mod = modulation(cc, w_mod, b_mod)[:, :bsz + 1].reshape(2, bsz + 1, 6, d)

    h0 = jnp.concatenate([ctx, x], axis=1).reshape(bsz * lb, d)

    p, dtr = norm_matmul(h0, norm1_g[0], mod[0], [w_in_b, w_dt_b], [BF16, F32],
                         shift_idx=0, scale_idx=1, mod_index=mod_comb)
    chunks = lb // SSD_CHUNK
    ctx_chunks = lc // SSD_CHUNK
    xc = dwconv_silu(p, 4, inner, conv_w[:, :inner], conv_b[:inner], chunks_per_batch=chunks, ctx_chunks=ctx_chunks)
    bcc = dwconv_silu(p, 10, xbc_w - inner, conv_w[:, inner:], conv_b[inner:],
                      chunks_per_batch=chunks, ctx_chunks=ctx_chunks)
    scan = functools.partial(ssd_scan, xc, bcc, dtr, dtb_row, a_row, batch=bsz,
                             chunks_per_batch=chunks, ctx_chunks=ctx_chunks)
    y_f = scan(reverse=False)
    y_ssd = scan(reverse=True, z_src=p, z_col=3, y_prev=y_f, d_skip=d_skip, norm_g=ssm_norm_g[0].reshape(1, inner))
    bias = na_bias_tables(na_rpb[0], lat // GRID_W)
    vt_all = transpose_cols(p, 2, NA_HEADS * HEAD_DIM)
    o_na = na_attention(p, vt_all, bias, batch=bsz, lat_rows=lat // GRID_W, ctx_len=lc, q_col=0, k_col=1)
    ident = lambda i: i
    h1 = resid_matmul(h0, mod[0], [y_ssd, o_na], [w_out[:inner], w_out[inner:]],
                      gate_idx=2, n_blocks=bsz * nb, h_block=ident, mod_index=mod_comb)
    h2 = moe_layer(h1, norm2_g[0], mod[0], rw_t, rb, wg, wu, wd, 0, final_g, mod_index=mod_comb, final=False)

    (p1,) = norm_matmul(h2, norm1_g[1], mod[1], [w_gqa], [BF16], shift_idx=0, scale_idx=1, mod_index=mod_comb)
    cos_f, sin_s = _rope_tables(lat)
    qt, kn, vt = gqa_prep(p1, cos_f, sin_s, qg, kg, batch=bsz, blocks_per_batch=nb, ctx_blocks=cb)
    o = gqa_attention(qt, kn.reshape(bsz, lb, GQA_KV_HEADS * HEAD_DIM), vt)
    lat_block = lambda i: (i // nlb) * nb + cb + i % nlb
    h3 = resid_matmul(h2, mod[1], [o], [w_gqa_out], gate_idx=2, n_blocks=bsz * nlb, h_block=lat_block, mod_index=mod_lat)
    out = moe_layer(h3, norm2_g[1], mod[1], rw_t, rb, wg, wu, wd, 1, final_g, mod_index=mod_lat, final=True)
    return out.reshape(bsz, lat, d)
```
